```python
import math
import jax
import jax.numpy as jnp
from jax import lax
import numpy as np

D_MODEL = 4096
BATCH = 1
SEQ = 8192
DEPTH = 4

BRANCH_WIDTH = 1024
N_BRANCH = 3
ATT_HEADS = 8
ATT_HEAD_DIM = 128
MOBA_BLOCK = 256
MOBA_TOPK = 3
Q_CHUNK = 128
REL_BUCKETS = 32
REL_MAX_DIST = 128
S5_GROUP = 16
S5_GROUPS = BRANCH_WIDTH // S5_GROUP
S5_STATE = 64
GLA_HEADS = 4
GLA_KEY = BRANCH_WIDTH // 2
GLA_VAL = BRANCH_WIDTH
GLA_GATE_RANK = 16
GLA_GATE_TAU = 16.0
GLA_CHUNK = 64
MERGE_RANK = 256
COND_RANK = 512
D_FF = 8192
N_EXPERTS = 8
TOP_K = 2
D_FF_EXPERT = 1792
DN_ALPHA = (2 * DEPTH) ** 0.25
DN_BETA = (8 * DEPTH) ** -0.25
LN_EPS = 1e-5
NORM_EPS = 1e-6

IN_WIDTHS = (BRANCH_WIDTH, BRANCH_WIDTH, BRANCH_WIDTH,
             BRANCH_WIDTH,
             GLA_KEY, GLA_KEY, GLA_VAL, GLA_VAL,
             GLA_GATE_RANK,
             MERGE_RANK)
D_IN = sum(IN_WIDTHS)
IN_SPLIT_IDX = tuple(int(s) for s in np.cumsum(IN_WIDTHS)[:-1])

kernel_name = 'hybrid_moba_s5_gla_moe_deepnorm'


def layer_norm(x, g, b):
    xf = x.astype(jnp.float32)
    mu = jnp.mean(xf, axis=-1, keepdims=True)
    var = jnp.mean(jnp.square(xf - mu), axis=-1, keepdims=True)
    y = (xf - mu) * lax.rsqrt(var + LN_EPS)
    return (y * g.astype(jnp.float32) + b.astype(jnp.float32)).astype(x.dtype)


def rel_bucket(dist):
    n = jnp.maximum(dist, 0)
    max_exact = REL_BUCKETS // 2
    nf = jnp.maximum(n, 1).astype(jnp.float32)
    large = max_exact + (jnp.log(nf / max_exact) / math.log(REL_MAX_DIST / max_exact)
                         * (REL_BUCKETS - max_exact)).astype(jnp.int32)
    large = jnp.minimum(large, REL_BUCKETS - 1)
    return jnp.where(n < max_exact, n, large)


def moba_attention(q, k, v, rel_bias):
    b, s, h, dh = q.shape
    nb = -(-s // MOBA_BLOCK)
    sp = nb * MOBA_BLOCK
    n_sel = min(MOBA_TOPK, nb)
    scale = dh ** -0.5

    def prep(t):
        return jnp.pad(t, ((0, 0), (0, sp - s), (0, 0), (0, 0))).transpose(0, 2, 1, 3)

    q, k, v = prep(q), prep(k), prep(v)
    kb = k.reshape(b, h, nb, MOBA_BLOCK, dh)
    vb = v.reshape(b, h, nb, MOBA_BLOCK, dh)
    k_mean = jnp.mean(kb.astype(jnp.float32), axis=3)
    bias_hb = rel_bias.astype(jnp.float32).T
    head_ix = jnp.arange(h)[:, None, None, None]
    offs = jnp.arange(MOBA_BLOCK)
    gather_blocks = jax.vmap(jax.vmap(lambda t, i: t[i]))

    def one_chunk(ci):
        q0 = ci * Q_CHUNK
        j = q0 // MOBA_BLOCK
        qc = lax.dynamic_slice_in_dim(q, q0, Q_CHUNK, axis=2)
        qpos = q0 + jnp.arange(Q_CHUNK)
        k_own = lax.dynamic_index_in_dim(kb, j, axis=2, keepdims=False)
        v_own = lax.dynamic_index_in_dim(vb, j, axis=2, keepdims=False)
        d_own = qpos[:, None] - (j * MOBA_BLOCK + offs)[None, :]
        s_own = jnp.einsum('bhqd,bhkd->bhqk', qc, k_own).astype(jnp.float32) * scale
        s_own = s_own + bias_hb[:, rel_bucket(d_own)]
        s_own = jnp.where(d_own >= 0, s_own, -jnp.inf)
        score = jnp.einsum('bhqd,bhnd->bhqn', qc.astype(jnp.float32), k_mean)
        score = jnp.where(jnp.arange(nb) < j, score, -jnp.inf)
        _, idx = lax.top_k(score, n_sel)
        k_sel = gather_blocks(kb, idx)
        v_sel = gather_blocks(vb, idx)
        d_sel = qpos[:, None, None] - (idx[..., None] * MOBA_BLOCK + offs)
        s_sel = jnp.einsum('bhqd,bhqnkd->bhqnk', qc, k_sel).astype(jnp.float32) * scale
        s_sel = s_sel + bias_hb[head_ix, rel_bucket(d_sel)]
        s_sel = jnp.where((jnp.arange(n_sel) < j)[:, None], s_sel, -jnp.inf)
        logits = jnp.concatenate(
            [s_own, s_sel.reshape(b, h, Q_CHUNK, n_sel * MOBA_BLOCK)], axis=-1)
        p = jax.nn.softmax(logits, axis=-1).astype(v.dtype)
        p_own = p[..., :MOBA_BLOCK]
        p_sel = p[..., MOBA_BLOCK:].reshape(b, h, Q_CHUNK, n_sel, MOBA_BLOCK)
        return (jnp.einsum('bhqk,bhkd->bhqd', p_own, v_own)
                + jnp.einsum('bhqnk,bhqnkd->bhqd', p_sel, v_sel))

    out = lax.map(one_chunk, jnp.arange(sp // Q_CHUNK))
    out = out.transpose(1, 0, 3, 2, 4).reshape(b, sp, h * dh)
    return out[:, :s]


def s5_ssm(u, lam_re, lam_im, log_dt, b_re, b_im, c_re, c_im, d_skip, w_glu, b_glu):
    bsz, s, _ = u.shape
    f32 = jnp.float32
    uf = u.astype(f32).reshape(bsz, s, S5_GROUPS, S5_GROUP)
    lam = lax.complex(lam_re.astype(f32), lam_im.astype(f32))
    dt = jnp.exp(log_dt.astype(f32))[:, None]
    lam_bar = jnp.exp(lam * dt)
    b_mat = lax.complex(b_re.astype(f32), b_im.astype(f32))
    b_bar = ((lam_bar - 1.0) / lam)[..., None] * b_mat
    c_mat = lax.complex(c_re.astype(f32), c_im.astype(f32))
    bu = jnp.einsum('bsgh,gph->bsgp', uf.astype(jnp.complex64), b_bar)
    a = jnp.broadcast_to(lam_bar, bu.shape)

    def combine(e1, e2):
        a1, x1 = e1
        a2, x2 = e2
        return a1 * a2, a2 * x1 + x2

    _, state = lax.associative_scan(combine, (a, bu), axis=1)
    y = jnp.real(jnp.einsum('bsgp,ghp->bsgh', state, c_mat)) + d_skip.astype(f32) * uf
    y = jax.nn.gelu(y.reshape(bsz, s, BRANCH_WIDTH)).astype(u.dtype)
    return y * jax.nn.sigmoid(y @ w_glu + b_glu)


def gla(q, k, v, r, g_low, w_gate, b_gate, norm_g):
    bsz, s, _ = q.shape
    f32 = jnp.float32
    h = GLA_HEADS
    dk = GLA_KEY // h
    dv = GLA_VAL // h
    nc = s // GLA_CHUNK
    log_a = jax.nn.log_sigmoid((g_low @ w_gate + b_gate).astype(f32)) / GLA_GATE_TAU

    def to_chunks(t, d):
        return t.astype(f32).reshape(bsz, nc, GLA_CHUNK, h, d).transpose(1, 0, 3, 2, 4)

    qf = to_chunks(q, dk) * dk ** -0.5
    kf = to_chunks(k, dk)
    vf = to_chunks(v, dv)
    gf = to_chunks(log_a, dk)
    causal = jnp.tril(jnp.ones((GLA_CHUNK, GLA_CHUNK), dtype=bool))

    def step(state, inp):
        qc, kc, vc, gc = inp
        bcum = jnp.cumsum(gc, axis=2)
        o_inter = jnp.einsum('bhcd,bhde->bhce', qc * jnp.exp(bcum), state)
        decay = jnp.exp(jnp.where(causal[:, :, None],
                                  bcum[:, :, :, None, :] - bcum[:, :, None, :, :], -jnp.inf))
        attn = jnp.einsum('bhid,bhjd,bhijd->bhij', qc, kc, decay)
        o = o_inter + jnp.einsum('bhij,bhje->bhie', attn, vc)
        b_last = bcum[:, :, -1:, :]
        state = (jnp.exp(b_last[:, :, 0, :])[..., None] * state
                 + jnp.einsum('bhcd,bhce->bhde', kc * jnp.exp(b_last - bcum), vc))
        return state, o

    state0 = jnp.zeros((bsz, h, dk, dv), f32)
    _, o = lax.scan(step, state0, (qf, kf, vf, gf))
    o = o.transpose(1, 0, 3, 2, 4).reshape(bsz, s, h, dv)
    o = o * lax.rsqrt(jnp.mean(o * o, axis=-1, keepdims=True) + NORM_EPS)
    o = o.reshape(bsz, s, GLA_VAL) * norm_g.astype(f32)
    return (o * jax.nn.silu(r.astype(f32))).astype(q.dtype)


def hybrid_mixer(hm, rel_bias, w_in, s5_lambda_re, s5_lambda_im, s5_log_dt, s5_b_re, s5_b_im,
                 s5_c_re, s5_c_im, s5_d, s5_w_glu, s5_b_glu, gla_w_gate, gla_b_gate, gla_norm_g,
                 w_branch, w_merge_gate, b_merge_gate, w_out):
    bsz, s, _ = hm.shape
    proj = hm @ w_in
    aq, ak, av, su, gq, gk, gv, gr, gg, mz = jnp.split(proj, IN_SPLIT_IDX, axis=-1)

    def heads(t):
        return t.reshape(bsz, s, ATT_HEADS, ATT_HEAD_DIM)

    y_att = moba_attention(heads(aq), heads(ak), heads(av), rel_bias)
    y_s5 = s5_ssm(su, s5_lambda_re, s5_lambda_im, s5_log_dt, s5_b_re, s5_b_im,
                  s5_c_re, s5_c_im, s5_d, s5_w_glu, s5_b_glu)
    y_gla = gla(gq, gk, gv, gr, gg, gla_w_gate, gla_b_gate, gla_norm_g)
    branches = jnp.stack([y_att.astype(hm.dtype), y_s5.astype(hm.dtype), y_gla], axis=2)
    up = jnp.einsum('bsnw,nwd->bsnd', branches, w_branch)
    gates = jax.nn.sigmoid(mz @ w_merge_gate + b_merge_gate).reshape(bsz, s, N_BRANCH, D_MODEL)
    merged = jnp.sum(gates * up, axis=2)
    return merged @ w_out


def swiglu(h, w1, w3, w2):
    return (jax.nn.silu(h @ w1) * (h @ w3)) @ w2


def moe_swiglu(h, router_w, router_b, e_w1, e_w3, e_w2):
    logits = (h @ router_w).astype(jnp.float32) + router_b.astype(jnp.float32)
    top_v, top_i = lax.top_k(logits, TOP_K)
    top_p = jax.nn.softmax(top_v, axis=-1)
    combine = jnp.sum(jax.nn.one_hot(top_i, N_EXPERTS, dtype=jnp.float32) * top_p[..., None], axis=-2)
    out = jnp.zeros_like(h)
    for e in range(N_EXPERTS):
        out = out + combine[..., e:e + 1].astype(h.dtype) * swiglu(h, e_w1[e], e_w3[e], e_w2[e])
    return out


def setup_inputs(seed: int = 0) -> dict:
    key = jax.random.key(seed)
    ks = iter(jax.random.split(key, 48))
    f32 = jnp.float32

    def nrm(shape, scale):
        return jax.random.normal(next(ks), shape, f32) * scale

    L = DEPTH
    LD = (DEPTH + 1) // 2
    LM = DEPTH // 2
    G, P, H = S5_GROUPS, S5_STATE, S5_GROUP
    D = D_MODEL
    x = nrm((BATCH, SEQ, D), 1.0)
    c = nrm((BATCH, D), 1.0)
    w_cond = nrm((D, COND_RANK), D ** -0.5)
    b_cond = nrm((COND_RANK,), 0.02)
    rel_bias = nrm((REL_BUCKETS, ATT_HEADS), 0.5)
    w_mod = nrm((L, COND_RANK, 6 * D), 0.1 * COND_RANK ** -0.5)
    b_mod = nrm((L, 6 * D), 0.02)
    w_in = nrm((L, D, D_IN), D ** -0.5)
    s5_lambda_re = -0.5 + nrm((L, G, P), 0.01)
    s5_lambda_im = math.pi * jnp.arange(P, dtype=f32) + nrm((L, G, P), 0.01)
    s5_log_dt = jax.random.uniform(next(ks), (L, G), f32, math.log(1e-3), math.log(1e-1))
    s5_b_re = nrm((L, G, P, H), (2 * H) ** -0.5)
    s5_b_im = nrm((L, G, P, H), (2 * H) ** -0.5)
    s5_c_re = nrm((L, G, H, P), P ** -0.5)
    s5_c_im = nrm((L, G, H, P), P ** -0.5)
    s5_d = nrm((L, G, H), 1.0)
    s5_w_glu = nrm((L, BRANCH_WIDTH, BRANCH_WIDTH), BRANCH_WIDTH ** -0.5)
    s5_b_glu = nrm((L, BRANCH_WIDTH), 0.02)
    gla_w_gate = nrm((L, GLA_GATE_RANK, GLA_KEY), GLA_GATE_RANK ** -0.5)
    gla_b_gate = nrm((L, GLA_KEY), 0.1)
    gla_norm_g = 1.0 + nrm((L, GLA_VAL), 0.02)
    w_branch = nrm((L, N_BRANCH, BRANCH_WIDTH, D), BRANCH_WIDTH ** -0.5)
    w_merge_gate = nrm((L, MERGE_RANK, N_BRANCH * D), MERGE_RANK ** -0.5)
    b_merge_gate = nrm((L, N_BRANCH * D), 0.02)
    w_out = nrm((L, D, D), DN_BETA * D ** -0.5)
    ln1_g = 1.0 + nrm((L, D), 0.02)
    ln1_b = nrm((L, D), 0.02)
    ffn_w1 = nrm((LD, D, D_FF), D ** -0.5)
    ffn_w3 = nrm((LD, D, D_FF), D ** -0.5)
    ffn_w2 = nrm((LD, D_FF, D), DN_BETA * D_FF ** -0.5)
    router_w = nrm((LM, D, N_EXPERTS), D ** -0.5)
    router_b = nrm((LM, N_EXPERTS), 0.01)
    exp_w1 = nrm((LM, N_EXPERTS, D, D_FF_EXPERT), D ** -0.5)
    exp_w3 = nrm((LM, N_EXPERTS, D, D_FF_EXPERT), D ** -0.5)
    exp_w2 = nrm((LM, N_EXPERTS, D_FF_EXPERT, D), DN_BETA * D_FF_EXPERT ** -0.5)
    ln2_g = 1.0 + nrm((L, D), 0.02)
    ln2_b = nrm((L, D), 0.02)
    return {'x': x, 'c': c, 'w_cond': w_cond, 'b_cond': b_cond, 'rel_bias': rel_bias,
            'w_mod': w_mod, 'b_mod': b_mod, 'w_in': w_in,
            's5_lambda_re': s5_lambda_re, 's5_lambda_im': s5_lambda_im, 's5_log_dt': s5_log_dt,
            's5_b_re': s5_b_re, 's5_b_im': s5_b_im, 's5_c_re': s5_c_re, 's5_c_im': s5_c_im,
            's5_d': s5_d, 's5_w_glu': s5_w_glu, 's5_b_glu': s5_b_glu,
            'gla_w_gate': gla_w_gate, 'gla_b_gate': gla_b_gate, 'gla_norm_g': gla_norm_g,
            'w_branch': w_branch, 'w_merge_gate': w_merge_gate, 'b_merge_gate': b_merge_gate,
            'w_out': w_out, 'ln1_g': ln1_g, 'ln1_b': ln1_b,
            'ffn_w1': ffn_w1, 'ffn_w3': ffn_w3, 'ffn_w2': ffn_w2,
            'router_w': router_w, 'router_b': router_b,
            'exp_w1': exp_w1, 'exp_w3': exp_w3, 'exp_w2': exp_w2,
            'ln2_g': ln2_g, 'ln2_b': ln2_b}


def reference(x, c, w_cond, b_cond, rel_bias, w_mod, b_mod, w_in,
              s5_lambda_re, s5_lambda_im, s5_log_dt, s5_b_re, s5_b_im, s5_c_re, s5_c_im,
              s5_d, s5_w_glu, s5_b_glu, gla_w_gate, gla_b_gate, gla_norm_g,
              w_branch, w_merge_gate, b_merge_gate, w_out, ln1_g, ln1_b,
              ffn_w1, ffn_w3, ffn_w2, router_w, router_b, exp_w1, exp_w3, exp_w2,
              ln2_g, ln2_b):
    bsz = x.shape[0]
    cond = jax.nn.silu(c @ w_cond + b_cond)
    for l in range(DEPTH):
        mod = (cond @ w_mod[l] + b_mod[l]).reshape(bsz, 6, D_MODEL)
        shift_m, scale_m, gate_m = mod[:, 0, None], mod[:, 1, None], mod[:, 2, None]
        shift_f, scale_f, gate_f = mod[:, 3, None], mod[:, 4, None], mod[:, 5, None]
        hm = x * (1.0 + scale_m) + shift_m
        y = hybrid_mixer(hm, rel_bias, w_in[l], s5_lambda_re[l], s5_lambda_im[l], s5_log_dt[l],
                         s5_b_re[l], s5_b_im[l], s5_c_re[l], s5_c_im[l], s5_d[l],
                         s5_w_glu[l], s5_b_glu[l], gla_w_gate[l], gla_b_gate[l], gla_norm_g[l],
                         w_branch[l], w_merge_gate[l], b_merge_gate[l], w_out[l])
        x = layer_norm(DN_ALPHA * x + (1.0 + gate_m) * y, ln1_g[l], ln1_b[l])
        hf = x * (1.0 + scale_f) + shift_f
        if l % 2 == 0:
            f = swiglu(hf, ffn_w1[l // 2], ffn_w3[l // 2], ffn_w2[l // 2])
        else:
            f = moe_swiglu(hf, router_w[l // 2], router_b[l // 2],
                           exp_w1[l // 2], exp_w3[l // 2], exp_w2[l // 2])
        x = layer_norm(DN_ALPHA * x + (1.0 + gate_f) * f, ln2_g[l], ln2_b[l])
    return x
```

```python
import functools
import math

import jax
import jax.numpy as jnp
from jax import lax
from jax.experimental import pallas as pl
from jax.experimental.pallas import tpu as pltpu

F32 = jnp.float32
BF16 = jnp.bfloat16
HIGHEST = lax.Precision.HIGHEST

D_MODEL = 4096
DEPTH = 4
BRANCH_WIDTH = 1024
N_BRANCH = 3
ATT_HEADS = 8
ATT_HEAD_DIM = 128
MOBA_BLOCK = 256
MOBA_TOPK = 3
REL_BUCKETS = 32
REL_MAX_DIST = 128
S5_GROUP = 16
S5_GROUPS = 64
S5_STATE = 64
S5_CHUNK = 16
S5_TILE_GROUPS = 8
S5_TILE_STATE = S5_TILE_GROUPS * S5_STATE
GLA_HEADS = 4
GLA_KEY = 512
GLA_VAL = 1024
GLA_GATE_RANK = 16
GLA_GATE_TAU = 16.0
GLA_CHUNK = 64
GLA_SUB = 16
MERGE_RANK = 256
COND_RANK = 512
D_FF = 8192
N_EXPERTS = 8
TOP_K = 2
D_FF_EXPERT = 1792
DN_ALPHA = (2 * DEPTH) ** 0.25
LN_EPS = 1e-5
NORM_EPS = 1e-6

D_IN = 7440
D_IN_PAD = 7680
TAIL_COL = 7168
TAIL_W = 512
LANES = 128
VMEM_LIMIT = 56 * 1024 * 1024
NEG_INF = float("-inf")


def _cparams(sem):
    return pltpu.CompilerParams(dimension_semantics=sem, vmem_limit_bytes=VMEM_LIMIT)


def _cond_kernel(c_ref, w_ref, b_ref, o_ref):
    z = jnp.dot(c_ref[...], w_ref[...], preferred_element_type=F32, precision=HIGHEST) + b_ref[...]
    o_ref[...] = z * jax.nn.sigmoid(z)


def _mod_kernel(cond_ref, w_ref, b_ref, o_ref):
    o_ref[0] = jnp.dot(cond_ref[...], w_ref[0], preferred_element_type=F32,
                       precision=HIGHEST) + b_ref[0]


def _conditioning(c, w_cond, b_cond, w_mod, b_mod):
    c8 = jnp.broadcast_to(c, (8, D_MODEL))
    cond = pl.pallas_call(
        _cond_kernel,
        out_shape=jax.ShapeDtypeStruct((8, COND_RANK), F32),
        compiler_params=_cparams(None),
        name="cond",
    )(c8, w_cond, b_cond.reshape(1, COND_RANK))
    n_mod = 6 * D_MODEL
    tn = 3072
    mod = pl.pallas_call(
        _mod_kernel,
        grid=(DEPTH, n_mod // tn),
        in_specs=[pl.BlockSpec((8, COND_RANK), lambda l, n: (0, 0)),
                  pl.BlockSpec((1, COND_RANK, tn), lambda l, n: (l, 0, n)),
                  pl.BlockSpec((1, 1, tn), lambda l, n: (l, 0, n))],
        out_specs=pl.BlockSpec((1, 8, tn), lambda l, n: (l, 0, n)),
        out_shape=jax.ShapeDtypeStruct((DEPTH, 8, n_mod), F32),
        compiler_params=_cparams(("arbitrary", "arbitrary")),
        name="mod",
    )(cond, w_mod, b_mod.reshape(DEPTH, 1, n_mod))
    return mod[:, 0, :].reshape(DEPTH, 6, 1, D_MODEL)


def _modulate_kernel(x_ref, scale_ref, shift_ref, o_ref):
    o_ref[...] = (x_ref[...] * (1.0 + scale_ref[...]) + shift_ref[...]).astype(o_ref.dtype)


def _modulate(x, scale, shift, tm=512):
    s = x.shape[0]
    vec = pl.BlockSpec((1, D_MODEL), lambda m: (0, 0))
    return pl.pallas_call(
        _modulate_kernel,
        grid=(s // tm,),
        in_specs=[pl.BlockSpec((tm, D_MODEL), lambda m: (m, 0)), vec, vec],
        out_specs=pl.BlockSpec((tm, D_MODEL), lambda m: (m, 0)),
        out_shape=jax.ShapeDtypeStruct((s, D_MODEL), BF16),
        compiler_params=_cparams(("arbitrary",)),
        name="modulate",
    )(x, scale, shift)


def _route_top2(logits):
    lane = lax.broadcasted_iota(jnp.int32, logits.shape, 1)
    m1 = jnp.max(logits, axis=-1, keepdims=True)
    i1 = jnp.min(jnp.where(logits == m1, lane, LANES), axis=-1, keepdims=True)
    rest = jnp.where(lane == i1, NEG_INF, logits)
    m2 = jnp.max(rest, axis=-1, keepdims=True)
    i2 = jnp.min(jnp.where(rest == m2, lane, LANES), axis=-1, keepdims=True)
    e2 = jnp.exp(m2 - m1)
    denom = 1.0 + e2
    return jnp.where(lane == i1, 1.0 / denom, 0.0) + jnp.where(lane == i2, e2 / denom, 0.0)


def _ln_kernel(*refs, has_next, has_router):
    x_ref, y_ref, gate_ref, g_ref, b_ref = refs[:5]
    pos = 5
    if has_next:
        scale_ref, shift_ref = refs[pos:pos + 2]
        pos += 2
    if has_router:
        rw_ref, rb_ref = refs[pos:pos + 2]
        pos += 2
    xo_ref = refs[pos]
    pos += 1
    z = DN_ALPHA * x_ref[...] + (1.0 + gate_ref[...]) * y_ref[...].astype(F32)
    mu = jnp.mean(z, axis=-1, keepdims=True)
    zc = z - mu
    var = jnp.mean(zc * zc, axis=-1, keepdims=True)
    xn = zc * lax.rsqrt(var + LN_EPS) * g_ref[...] + b_ref[...]
    xo_ref[...] = xn
    if has_next:
        ho_ref = refs[pos]
        pos += 1
        h = xn * (1.0 + scale_ref[...]) + shift_ref[...]
        ho_ref[...] = h.astype(ho_ref.dtype)
        if has_router:
            co_ref = refs[pos]
            logits = jnp.dot(h, rw_ref[...], preferred_element_type=F32, precision=HIGHEST)
            lane = lax.broadcasted_iota(jnp.int32, logits.shape, 1)
            logits = jnp.where(lane < N_EXPERTS, logits + rb_ref[...], NEG_INF)
            co_ref[...] = _route_top2(logits)


def _deepnorm_ln(x, y, gate, g, b, nxt=None, router=None, tm=256):
    s = x.shape[0]
    row = pl.BlockSpec((tm, D_MODEL), lambda m: (m, 0))
    vec = pl.BlockSpec((1, D_MODEL), lambda m: (0, 0))
    args = [x, y, gate, g.reshape(1, D_MODEL), b.reshape(1, D_MODEL)]
    in_specs = [row, row, vec, vec, vec]
    out_shape = [jax.ShapeDtypeStruct((s, D_MODEL), F32)]
    out_specs = [row]
    if nxt is not None:
        args += [nxt[0], nxt[1]]
        in_specs += [vec, vec]
        out_shape.append(jax.ShapeDtypeStruct((s, D_MODEL), BF16))
        out_specs.append(row)
    if router is not None:
        rw, rb = router
        rw_pad = jnp.pad(rw, ((0, 0), (0, LANES - N_EXPERTS)))
        rb_pad = jnp.pad(rb, (0, LANES - N_EXPERTS)).reshape(1, LANES)
        args += [rw_pad, rb_pad]
        in_specs += [pl.BlockSpec((D_MODEL, LANES), lambda m: (0, 0)),
                     pl.BlockSpec((1, LANES), lambda m: (0, 0))]
        out_shape.append(jax.ShapeDtypeStruct((s, LANES), F32))
        out_specs.append(pl.BlockSpec((tm, LANES), lambda m: (m, 0)))
    return pl.pallas_call(
        functools.partial(_ln_kernel, has_next=nxt is not None, has_router=router is not None),
        grid=(s // tm,),
        in_specs=in_specs,
        out_specs=out_specs,
        out_shape=out_shape,
        compiler_params=_cparams(("arbitrary",)),
        name="deepnorm_ln",
    )(*args)


def _matmul_kernel(a_ref, w_ref, o_ref):
    o_ref[...] = jnp.dot(a_ref[...], w_ref[...], preferred_element_type=F32).astype(o_ref.dtype)


def _matmul(a, w, tm, tn, out_dtype=BF16, name="matmul"):
    m, k = a.shape
    n = w.shape[1]
    return pl.pallas_call(
        _matmul_kernel,
        grid=(n // tn, m // tm),
        in_specs=[pl.BlockSpec((tm, k), lambda j, i: (i, 0)),
                  pl.BlockSpec((k, tn), lambda j, i: (0, j))],
        out_specs=pl.BlockSpec((tm, tn), lambda j, i: (i, j)),
        out_shape=jax.ShapeDtypeStruct((m, n), out_dtype),
        compiler_params=_cparams(("arbitrary", "arbitrary")),
        name=name,
    )(a, w)


def _rel_bucket(dist):
    n = jnp.maximum(dist, 0)
    max_exact = REL_BUCKETS // 2
    nf = jnp.maximum(n, 1).astype(F32)
    large = max_exact + (jnp.log(nf / max_exact) / math.log(REL_MAX_DIST / max_exact)
                         * (REL_BUCKETS - max_exact)).astype(jnp.int32)
    large = jnp.minimum(large, REL_BUCKETS - 1)
    return jnp.where(n < max_exact, n, large)


def _moba_kernel(relb_ref, q_ref, k_ref, v_ref, o_ref, kmean_ref, bown_ref, bprev_ref, *, nb):
    h = pl.program_id(0)
    j = pl.program_id(1)
    blk = MOBA_BLOCK
    scale = ATT_HEAD_DIM ** -0.5
    nt = (((1,), (1,)), ((), ()))

    @pl.when(j == 0)
    def _():
        kf = k_ref[...].astype(F32).reshape(nb, blk, ATT_HEAD_DIM)
        kmean_ref[...] = jnp.mean(kf, axis=1)
        qi = lax.broadcasted_iota(jnp.int32, (blk, blk), 0)
        ki = lax.broadcasted_iota(jnp.int32, (blk, blk), 1)
        for ref, off in ((bown_ref, 0), (bprev_ref, blk)):
            bucket = _rel_bucket(qi - ki + off)
            bias = jnp.zeros((blk, blk), F32)
            for b in range(REL_BUCKETS):
                bias = jnp.where(bucket == b, relb_ref[h, b], bias)
            ref[...] = bias

    q = q_ref[...]
    score = lax.dot_general(q.astype(F32), kmean_ref[...], nt,
                            preferred_element_type=F32, precision=HIGHEST)
    lane = lax.broadcasted_iota(jnp.int32, (blk, nb), 1)
    sc = jnp.where(lane < j, score, NEG_INF)
    seladd = jnp.full((blk, nb), NEG_INF, F32)
    for _ in range(MOBA_TOPK):
        mx = jnp.max(sc, axis=-1, keepdims=True)
        cand = jnp.where(sc == mx, lane, nb)
        cand = jnp.where(mx > NEG_INF, cand, nb)
        idx = jnp.min(cand, axis=-1, keepdims=True)
        pick = lane == idx
        seladd = jnp.where(pick, 0.0, seladd)
        sc = jnp.where(pick, NEG_INF, sc)

    row0 = pl.multiple_of(j * blk, blk)
    k_own = k_ref[pl.ds(row0, blk), :]
    v_own = v_ref[pl.ds(row0, blk), :]
    s = lax.dot_general(q, k_own, nt, preferred_element_type=F32) * scale + bown_ref[...]
    qi = lax.broadcasted_iota(jnp.int32, (blk, blk), 0)
    ki = lax.broadcasted_iota(jnp.int32, (blk, blk), 1)
    s = jnp.where(qi >= ki, s, NEG_INF)
    m0 = jnp.max(s, axis=-1, keepdims=True)
    p = jnp.exp(s - m0)
    l0 = jnp.sum(p, axis=-1, keepdims=True)
    acc0 = jnp.dot(p.astype(BF16), v_own, preferred_element_type=F32)

    def update(kb, carry, bias):
        m, l, acc = carry
        r0 = pl.multiple_of(kb * blk, blk)
        kblk = k_ref[pl.ds(r0, blk), :]
        vblk = v_ref[pl.ds(r0, blk), :]
        colmask = jnp.max(jnp.where(lane == kb, seladd, NEG_INF), axis=-1, keepdims=True)
        sb = lax.dot_general(q, kblk, nt, preferred_element_type=F32) * scale + bias + colmask
        m_new = jnp.maximum(m, jnp.max(sb, axis=-1, keepdims=True))
        alpha = jnp.exp(m - m_new)
        pb = jnp.exp(sb - m_new)
        l_new = alpha * l + jnp.sum(pb, axis=-1, keepdims=True)
        acc_new = alpha * acc + jnp.dot(pb.astype(BF16), vblk, preferred_element_type=F32)
        return m_new, l_new, acc_new

    far_bias = relb_ref[h, REL_BUCKETS - 1]
    carry = lax.fori_loop(0, j - 1, lambda kb, c: update(kb, c, far_bias), (m0, l0, acc0))
    _, l, acc = update(jnp.maximum(j - 1, 0), carry, bprev_ref[...])
    o_ref[...] = (acc / l).astype(o_ref.dtype)


def _moba(proj, rel_bias):
    s = proj.shape[0]
    nb = s // MOBA_BLOCK
    dh = ATT_HEAD_DIM
    return pl.pallas_call(
        functools.partial(_moba_kernel, nb=nb),
        grid=(ATT_HEADS, nb),
        in_specs=[pl.BlockSpec(memory_space=pltpu.SMEM),
                  pl.BlockSpec((MOBA_BLOCK, dh), lambda h, j: (j, h)),
                  pl.BlockSpec((s, dh), lambda h, j: (0, ATT_HEADS + h)),
                  pl.BlockSpec((s, dh), lambda h, j: (0, 2 * ATT_HEADS + h))],
        out_specs=pl.BlockSpec((MOBA_BLOCK, dh), lambda h, j: (j, h)),
        out_shape=jax.ShapeDtypeStruct((s, BRANCH_WIDTH), BF16),
        scratch_shapes=[pltpu.VMEM((nb, dh), F32),
                        pltpu.VMEM((MOBA_BLOCK, MOBA_BLOCK), F32),
                        pltpu.VMEM((MOBA_BLOCK, MOBA_BLOCK), F32)],
        compiler_params=_cparams(("arbitrary", "arbitrary")),
        name="moba",
    )(rel_bias.T, proj, proj, proj)


def _s5_tables(lam_re, lam_im, log_dt, b_re, b_im, c_re, c_im):
    t_len = S5_CHUNK
    g_cnt, p_cnt, h_cnt = S5_GROUPS, S5_STATE, S5_GROUP
    tg = S5_TILE_GROUPS
    nt = g_cnt // tg
    dt = jnp.exp(log_dt)[:, None]
    ar, ai = lam_re * dt, lam_im * dt

    def lam_pow(steps):
        st = steps.astype(F32)[:, None, None]
        mag = jnp.exp(st * ar)
        return mag * jnp.cos(st * ai), mag * jnp.sin(st * ai)

    pr, pi = lam_pow(jnp.arange(t_len + 1))
    qr, qi = lam_pow((t_len - 1) - jnp.arange(t_len))
    nr, ni = pr[1] - 1.0, pi[1]
    den = lam_re * lam_re + lam_im * lam_im
    rr, ri = (nr * lam_re + ni * lam_im) / den, (ni * lam_re - nr * lam_im) / den
    bbr = rr[..., None] * b_re - ri[..., None] * b_im
    bbi = rr[..., None] * b_im + ri[..., None] * b_re

    def c_times(xr, xi):
        return (c_re[None] * xr[:, :, None, :] - c_im[None] * xi[:, :, None, :],
                c_re[None] * xi[:, :, None, :] + c_im[None] * xr[:, :, None, :])

    cpr, cpi = c_times(pr, pi)
    cqr, cqi = c_times(qr, qi)
    kc = (jnp.einsum('tghp,gpk->tgkh', cqr, bbr, precision=HIGHEST)
          - jnp.einsum('tghp,gpk->tgkh', cqi, bbi, precision=HIGHEST))
    eye = jnp.eye(tg, dtype=F32)
    kd = jnp.einsum('tcgkh,gf->ctgkfh', kc.reshape(t_len, nt, tg, h_cnt, h_cnt), eye)
    kd = kd.reshape(nt, t_len * LANES, LANES)
    qr4, qi4 = qr[:, :, :, None], qi[:, :, :, None]
    bzr = (qr4 * bbr[None] - qi4 * bbi[None]).transpose(0, 1, 3, 2)
    bzi = (qr4 * bbi[None] + qi4 * bbr[None]).transpose(0, 1, 3, 2)
    bz = jnp.stack([bzr, bzi], axis=3).reshape(t_len, nt, tg, h_cnt, 2, p_cnt)
    bz = jnp.einsum('tcgkxp,gf->ctgkxfp', bz, eye).reshape(nt, t_len * LANES, 2 * tg * p_cnt)
    cm = jnp.stack([cpr[1:], -cpi[1:]], axis=2).reshape(t_len, nt, tg, 2, h_cnt, p_cnt)
    cm = jnp.einsum('tcgxhp,gf->ctxfpgh', cm, eye).reshape(nt, t_len, 2 * tg * p_cnt, LANES)
    a_re = pr[t_len].reshape(nt, 1, tg * p_cnt)
    a_im = pi[t_len].reshape(nt, 1, tg * p_cnt)
    return kd.astype(BF16), bz.astype(BF16), cm.astype(BF16), a_re, a_im


def _s5_kernel(u_ref, kd_ref, bz_ref, cm_ref, are_ref, aim_ref, d_ref, y_ref,
               uf_ref, ucat_ref, z_ref, hc_ref, *, nc):
    t_len = S5_CHUNK
    ns = S5_TILE_STATE
    uf_ref[...] = u_ref[...].astype(F32)
    for s in range(t_len):
        ucat_ref[:, s * LANES:(s + 1) * LANES] = uf_ref[pl.ds(s, nc, stride=t_len), :].astype(BF16)
    z_ref[...] = jnp.dot(ucat_ref[...], bz_ref[0], preferred_element_type=F32)
    a_re = are_ref[0]
    a_im = aim_ref[0]

    def step(c, carry):
        h_re, h_im = carry
        hc_ref[pl.ds(c, 1), 0:ns] = h_re
        hc_ref[pl.ds(c, 1), ns:2 * ns] = h_im
        z_re = z_ref[pl.ds(c, 1), 0:ns]
        z_im = z_ref[pl.ds(c, 1), ns:2 * ns]
        return (a_re * h_re - a_im * h_im + z_re, a_re * h_im + a_im * h_re + z_im)

    zero = jnp.zeros((1, ns), F32)
    lax.fori_loop(0, nc, step, (zero, zero))
    hc = hc_ref[...].astype(BF16)
    d_skip = d_ref[...]
    for t in range(t_len):
        acc = jnp.dot(hc, cm_ref[0, t], preferred_element_type=F32)
        acc += jnp.dot(ucat_ref[:, 0:(t + 1) * LANES],
                       kd_ref[0, (t_len - 1 - t) * LANES:t_len * LANES, :],
                       preferred_element_type=F32)
        acc += d_skip * uf_ref[pl.ds(t, nc, stride=t_len), :]
        y_ref[pl.ds(t, nc, stride=t_len), :] = jax.nn.gelu(acc)


def _s5_glu_kernel(y_ref, yn_ref, w_ref, b_ref, o_ref):
    z = jnp.dot(y_ref[...].astype(BF16), w_ref[...], preferred_element_type=F32) + b_ref[...]
    o_ref[...] = (yn_ref[...] * jax.nn.sigmoid(z)).astype(o_ref.dtype)


def _s5(proj, lam_re, lam_im, log_dt, b_re, b_im, c_re, c_im, d_skip, w_glu, b_glu):
    s = proj.shape[0]
    nc = s // S5_CHUNK
    nt = S5_GROUPS // S5_TILE_GROUPS
    kd, bz, cm, a_re, a_im = _s5_tables(lam_re, lam_im, log_dt, b_re, b_im, c_re, c_im)
    u_col = 3 * BRANCH_WIDTH // LANES
    ns2 = 2 * S5_TILE_STATE
    y = pl.pallas_call(
        functools.partial(_s5_kernel, nc=nc),
        grid=(nt,),
        in_specs=[pl.BlockSpec((s, LANES), lambda c: (0, u_col + c)),
                  pl.BlockSpec((1, S5_CHUNK * LANES, LANES), lambda c: (c, 0, 0)),
                  pl.BlockSpec((1, S5_CHUNK * LANES, ns2), lambda c: (c, 0, 0)),
                  pl.BlockSpec((1, S5_CHUNK, ns2, LANES), lambda c: (c, 0, 0, 0)),
                  pl.BlockSpec((1, 1, S5_TILE_STATE), lambda c: (c, 0, 0)),
                  pl.BlockSpec((1, 1, S5_TILE_STATE), lambda c: (c, 0, 0)),
                  pl.BlockSpec((1, LANES), lambda c: (0, c))],
        out_specs=pl.BlockSpec((s, LANES), lambda c: (0, c)),
        out_shape=jax.ShapeDtypeStruct((s, BRANCH_WIDTH), F32),
        scratch_shapes=[pltpu.VMEM((s, LANES), F32),
                        pltpu.VMEM((nc, S5_CHUNK * LANES), BF16),
                        pltpu.VMEM((nc, ns2), F32),
                        pltpu.VMEM((nc, ns2), F32)],
        compiler_params=_cparams(("arbitrary",)),
        name="s5_scan",
    )(proj, kd, bz, cm, a_re, a_im, d_skip.reshape(1, BRANCH_WIDTH))
    tm, tn = 512, 512
    return pl.pallas_call(
        _s5_glu_kernel,
        grid=(BRANCH_WIDTH // tn, s // tm),
        in_specs=[pl.BlockSpec((tm, BRANCH_WIDTH), lambda j, i: (i, 0)),
                  pl.BlockSpec((tm, tn), lambda j, i: (i, j)),
                  pl.BlockSpec((BRANCH_WIDTH, tn), lambda j, i: (0, j)),
                  pl.BlockSpec((1, tn), lambda j, i: (0, j))],
        out_specs=pl.BlockSpec((tm, tn), lambda j, i: (i, j)),
        out_shape=jax.ShapeDtypeStruct((s, BRANCH_WIDTH), BF16),
        compiler_params=_cparams(("arbitrary", "arbitrary")),
        name="s5_glu",
    )(y, y, w_glu.astype(BF16), b_glu.reshape(1, BRANCH_WIDTH))


def _gla_kernel(q_ref, k_ref, v_ref, r_ref, tail_ref, wg_ref, bg_ref, ng_ref, o_ref, st_ref):
    dk = GLA_KEY // GLA_HEADS
    dv = GLA_VAL // GLA_HEADS
    cs = GLA_CHUNK
    sub = GLA_SUB
    nt = (((1,), (1,)), ((), ()))
    tn = (((0,), (0,)), ((), ()))

    @pl.when(pl.program_id(0) == 0)
    def _():
        st_ref[...] = jnp.zeros_like(st_ref)

    gate_in = jnp.dot(tail_ref[...], wg_ref[...], preferred_element_type=F32) + bg_ref[...]
    log_a = jax.nn.log_sigmoid(gate_in) / GLA_GATE_TAU
    ri = lax.broadcasted_iota(jnp.int32, (cs, cs), 0)
    ci = lax.broadcasted_iota(jnp.int32, (cs, cs), 1)
    tril = (ri >= ci).astype(F32)
    bcum_all = jnp.dot(tril, log_a, preferred_element_type=F32, precision=HIGHEST)
    sub_row = lax.broadcasted_iota(jnp.int32, (sub, 1), 0)

    for h in range(GLA_HEADS):
        q = q_ref[:, h * dk:(h + 1) * dk].astype(F32) * dk ** -0.5
        k = k_ref[:, h * dk:(h + 1) * dk].astype(F32)
        v_bf = v_ref[:, h * dv:(h + 1) * dv]
        v = v_bf.astype(F32)
        bc = bcum_all[:, h * dk:(h + 1) * dk]
        state = st_ref[h]
        o_inter = lax.dot_general((q * jnp.exp(bc)).astype(BF16), state.astype(BF16), nt,
                                  preferred_element_type=F32)
        parts = []
        for i in range(cs // sub):
            lo = i * sub
            b_i, q_i, k_i, v_i = bc[lo:lo + sub], q[lo:lo + sub], k[lo:lo + sub], v[lo:lo + sub]
            o_i = o_inter[lo:lo + sub]
            if i > 0:
                ref = bc[lo - 1:lo]
                q_h = (q_i * jnp.exp(b_i - ref)).astype(BF16)
                k_h = (k[:lo] * jnp.exp(ref - bc[:lo])).astype(BF16)
                attn = lax.dot_general(q_h, k_h, nt, preferred_element_type=F32)
                o_i = o_i + jnp.dot(attn.astype(BF16), v_bf[:lo], preferred_element_type=F32)
            for dlt in range(sub):
                if dlt == 0:
                    a = jnp.sum(q_i * k_i, axis=-1, keepdims=True)
                    o_i = o_i + a * v_i
                else:
                    b_s = pltpu.roll(b_i, dlt, 0)
                    k_s = pltpu.roll(k_i, dlt, 0)
                    v_s = pltpu.roll(v_i, dlt, 0)
                    e = jnp.exp(jnp.minimum(b_i - b_s, 0.0))
                    a = jnp.sum(q_i * k_s * e, axis=-1, keepdims=True)
                    a = jnp.where(sub_row >= dlt, a, 0.0)
                    o_i = o_i + a * v_s
            parts.append(o_i)
        o = jnp.concatenate(parts, axis=0)
        b_last = bc[cs - 1:cs]
        k_dec = (k * jnp.exp(b_last - bc)).astype(BF16)
        st_ref[h] = state * jnp.exp(b_last) + lax.dot_general(v_bf, k_dec, tn,
                                                             preferred_element_type=F32)
        o = o * lax.rsqrt(jnp.mean(o * o, axis=-1, keepdims=True) + NORM_EPS)
        o = o * ng_ref[:, h * dv:(h + 1) * dv]
        r = r_ref[:, h * dv:(h + 1) * dv].astype(F32)
        o_ref[:, h * dv:(h + 1) * dv] = (o * (r * jax.nn.sigmoid(r))).astype(o_ref.dtype)


def _gla(proj, w_gate, b_gate, norm_g):
    s = proj.shape[0]
    cs = GLA_CHUNK
    wg_pad = jnp.pad(w_gate, ((0, TAIL_W - GLA_GATE_RANK), (0, 0))).astype(BF16)
    return pl.pallas_call(
        _gla_kernel,
        grid=(s // cs,),
        in_specs=[pl.BlockSpec((cs, GLA_KEY), lambda c: (c, 4 * BRANCH_WIDTH // GLA_KEY)),
                  pl.BlockSpec((cs, GLA_KEY), lambda c: (c, 4 * BRANCH_WIDTH // GLA_KEY + 1)),
                  pl.BlockSpec((cs, GLA_VAL), lambda c: (c, 5)),
                  pl.BlockSpec((cs, GLA_VAL), lambda c: (c, 6)),
                  pl.BlockSpec((cs, TAIL_W), lambda c: (c, TAIL_COL // TAIL_W)),
                  pl.BlockSpec((TAIL_W, GLA_KEY), lambda c: (0, 0)),
                  pl.BlockSpec((1, GLA_KEY), lambda c: (0, 0)),
                  pl.BlockSpec((1, GLA_VAL), lambda c: (0, 0))],
        out_specs=pl.BlockSpec((cs, GLA_VAL), lambda c: (c, 0)),
        out_shape=jax.ShapeDtypeStruct((s, GLA_VAL), BF16),
        scratch_shapes=[pltpu.VMEM((GLA_HEADS, GLA_VAL // GLA_HEADS, GLA_KEY // GLA_HEADS), F32)],
        compiler_params=_cparams(("arbitrary",)),
        name="gla",
    )(proj, proj, proj, proj, proj, wg_pad, b_gate.reshape(1, GLA_KEY), norm_g.reshape(1, GLA_VAL))


def _merge_kernel(ya_ref, ys_ref, yg_ref, tail_ref, wb_ref, wg_ref, bg_ref, o_ref):
    tail = tail_ref[...]
    acc = None
    for n, y_ref in enumerate((ya_ref, ys_ref, yg_ref)):
        up = jnp.dot(y_ref[...], wb_ref[n], preferred_element_type=F32)
        gate = jax.nn.sigmoid(jnp.dot(tail, wg_ref[n], preferred_element_type=F32) + bg_ref[n])
        acc = gate * up if acc is None else acc + gate * up
    o_ref[...] = acc.astype(o_ref.dtype)


def _merge(y_att, y_s5, y_gla, proj, w_branch, w_merge_gate, b_merge_gate, tm=512, tn=512):
    s = proj.shape[0]
    wb = w_branch.astype(BF16)
    wg = w_merge_gate.reshape(MERGE_RANK, N_BRANCH, D_MODEL).transpose(1, 0, 2)
    wg = jnp.pad(wg, ((0, 0), (GLA_GATE_RANK, TAIL_W - GLA_GATE_RANK - MERGE_RANK), (0, 0))).astype(BF16)
    bg = b_merge_gate.reshape(N_BRANCH, 1, D_MODEL)
    ybs = pl.BlockSpec((tm, BRANCH_WIDTH), lambda j, i: (i, 0))
    return pl.pallas_call(
        _merge_kernel,
        grid=(D_MODEL // tn, s // tm),
        in_specs=[ybs, ybs, ybs,
                  pl.BlockSpec((tm, TAIL_W), lambda j, i: (i, TAIL_COL // TAIL_W)),
                  pl.BlockSpec((N_BRANCH, BRANCH_WIDTH, tn), lambda j, i: (0, 0, j)),
                  pl.BlockSpec((N_BRANCH, TAIL_W, tn), lambda j, i: (0, 0, j)),
                  pl.BlockSpec((N_BRANCH, 1, tn), lambda j, i: (0, 0, j))],
        out_specs=pl.BlockSpec((tm, tn), lambda j, i: (i, j)),
        out_shape=jax.ShapeDtypeStruct((s, D_MODEL), BF16),
        compiler_params=_cparams(("arbitrary", "arbitrary")),
        name="merge",
    )(y_att, y_s5, y_gla, proj, wb, wg, bg)


def _swiglu_kernel(a_ref, w1_ref, w3_ref, o_ref):
    a = a_ref[...]
    g = jnp.dot(a, w1_ref[...], preferred_element_type=F32)
    u = jnp.dot(a, w3_ref[...], preferred_element_type=F32)
    o_ref[...] = (g * jax.nn.sigmoid(g) * u).astype(o_ref.dtype)


def _swiglu_hidden(a, w1, w3, tm=512, tn=512):
    m, k = a.shape
    n = w1.shape[1]
    wspec = pl.BlockSpec((k, tn), lambda j, i: (0, j))
    return pl.pallas_call(
        _swiglu_kernel,
        grid=(n // tn, m // tm),
        in_specs=[pl.BlockSpec((tm, k), lambda j, i: (i, 0)), wspec, wspec],
        out_specs=pl.BlockSpec((tm, tn), lambda j, i: (i, j)),
        out_shape=jax.ShapeDtypeStruct((m, n), BF16),
        compiler_params=_cparams(("arbitrary", "arbitrary")),
        name="swiglu_hidden",
    )(a, w1, w3)


def _moe_hidden_kernel(a_ref, comb_ref, w1_ref, w3_ref, o_ref, *, tiles_per_expert):
    e = pl.program_id(0) // tiles_per_expert
    a = a_ref[...]
    g = jnp.dot(a, w1_ref[0], preferred_element_type=F32)
    u = jnp.dot(a, w3_ref[0], preferred_element_type=F32)
    comb = comb_ref[...]
    lane = lax.broadcasted_iota(jnp.int32, comb.shape, 1)
    w = jnp.sum(jnp.where(lane == e, comb, 0.0), axis=-1, keepdims=True)
    o_ref[...] = (g * jax.nn.sigmoid(g) * u * w).astype(o_ref.dtype)


def _moe_hidden(a, comb, w1, w3, tm=512, tn=256):
    m, k = a.shape
    tpe = D_FF_EXPERT // tn
    wspec = pl.BlockSpec((1, k, tn), lambda j, i: (j // tpe, 0, j % tpe))
    return pl.pallas_call(
        functools.partial(_moe_hidden_kernel, tiles_per_expert=tpe),
        grid=(N_EXPERTS * tpe, m // tm),
        in_specs=[pl.BlockSpec((tm, k), lambda j, i: (i, 0)),
                  pl.BlockSpec((tm, LANES), lambda j, i: (i, 0)), wspec, wspec],
        out_specs=pl.BlockSpec((tm, tn), lambda j, i: (i, j)),
        out_shape=jax.ShapeDtypeStruct((m, N_EXPERTS * D_FF_EXPERT), BF16),
        compiler_params=_cparams(("arbitrary", "arbitrary")),
        name="moe_hidden",
    )(a, comb, w1, w3)


def kernel(x, c, w_cond, b_cond, rel_bias, w_mod, b_mod, w_in, s5_lambda_re, s5_lambda_im, s5_log_dt, s5_b_re, s5_b_im, s5_c_re, s5_c_im, s5_d, s5_w_glu, s5_b_glu, gla_w_gate, gla_b_gate, gla_norm_g, w_branch, w_merge_gate, b_merge_gate, w_out, ln1_g, ln1_b, ffn_w1, ffn_w3, ffn_w2, router_w, router_b, exp_w1, exp_w3, exp_w2, ln2_g, ln2_b):
    bsz, seq, _ = x.shape
    assert bsz == 1
    mod = _conditioning(c, w_cond, b_cond, w_mod, b_mod)
    xs = x.reshape(seq, D_MODEL)
    hm = _modulate(xs, mod[0, 1], mod[0, 0])
    for l in range(DEPTH):
        shift_f, scale_f, gate_m, gate_f = mod[l, 3], mod[l, 4], mod[l, 2], mod[l, 5]
        w_in_b = jnp.pad(w_in[l].astype(BF16), ((0, 0), (0, D_IN_PAD - D_IN)))
        proj = _matmul(hm, w_in_b, tm=512, tn=512, name="in_proj")
        y_att = _moba(proj, rel_bias)
        y_s5 = _s5(proj, s5_lambda_re[l], s5_lambda_im[l], s5_log_dt[l], s5_b_re[l], s5_b_im[l],
                   s5_c_re[l], s5_c_im[l], s5_d[l], s5_w_glu[l], s5_b_glu[l])
        y_gla = _gla(proj, gla_w_gate[l], gla_b_gate[l], gla_norm_g[l])
        merged = _merge(y_att, y_s5, y_gla, proj, w_branch[l], w_merge_gate[l], b_merge_gate[l])
        y = _matmul(merged, w_out[l].astype(BF16), tm=512, tn=1024, name="out_proj")
        dense = l % 2 == 0
        router = None if dense else (router_w[l // 2], router_b[l // 2])
        outs = _deepnorm_ln(xs, y, gate_m, ln1_g[l], ln1_b[l], nxt=(scale_f, shift_f), router=router)
        xs, hf = outs[0], outs[1]
        if dense:
            hid = _swiglu_hidden(hf, ffn_w1[l // 2].astype(BF16), ffn_w3[l // 2].astype(BF16))
            f = _matmul(hid, ffn_w2[l // 2].astype(BF16), tm=256, tn=1024, name="ffn_down")
        else:
            hid = _moe_hidden(hf, outs[2], exp_w1[l // 2].astype(BF16), exp_w3[l // 2].astype(BF16))
            w2 = exp_w2[l // 2].astype(BF16).reshape(N_EXPERTS * D_FF_EXPERT, D_MODEL)
            f = _matmul(hid, w2, tm=256, tn=512, name="moe_down")
        if l + 1 < DEPTH:
            xs, hm = _deepnorm_ln(xs, f, gate_f, ln2_g[l], ln2_b[l],
                                  nxt=(mod[l + 1, 1], mod[l + 1, 0]))
        else:
            (xs,) = _deepnorm_ln(xs, f, gate_f, ln2_g[l], ln2_b[l])
    return xs.reshape(bsz, seq, D_MODEL)
```

```python
import functools
import math

import jax
import jax.numpy as jnp
from jax import lax
from jax.experimental import pallas as pl
from jax.experimental.pallas import tpu as pltpu

F32 = jnp.float32
BF16 = jnp.bfloat16
HIGHEST = lax.Precision.HIGHEST

D_MODEL = 4096
DEPTH = 4
BRANCH_WIDTH = 1024
N_BRANCH = 3
ATT_HEADS = 8
ATT_HEAD_DIM = 128
MOBA_BLOCK = 256
MOBA_TOPK = 3
MOBA_HEAD_GROUP = 4
MOBA_ONES_ROWS = 16
REL_BUCKETS = 32
REL_MAX_DIST = 128
S5_GROUP = 16
S5_GROUPS = 64
S5_STATE = 64
S5_CHUNK = 16
S5_TILE_GROUPS = 8
S5_TILE_STATE = S5_TILE_GROUPS * S5_STATE
GLA_HEADS = 4
GLA_KEY = 512
GLA_VAL = 1024
GLA_GATE_RANK = 16
GLA_GATE_TAU = 16.0
GLA_CHUNK = 64
GLA_SUB = 16
MERGE_RANK = 256
COND_RANK = 512
D_FF = 8192
N_EXPERTS = 8
TOP_K = 2
D_FF_EXPERT = 1792
DN_ALPHA = (2 * DEPTH) ** 0.25
LN_EPS = 1e-5
NORM_EPS = 1e-6

D_IN = 7440
D_IN_PAD = 7680
TAIL_COL = 7168
TAIL_W = 512
LANES = 128
VMEM_LIMIT = 56 * 1024 * 1024
NEG_INF = float("-inf")


def _cparams(sem):
    return pltpu.CompilerParams(dimension_semantics=sem, vmem_limit_bytes=VMEM_LIMIT)


def _cond_kernel(c_ref, w_ref, b_ref, o_ref):
    z = jnp.dot(c_ref[...], w_ref[...], preferred_element_type=F32, precision=HIGHEST) + b_ref[...]
    o_ref[...] = z * jax.nn.sigmoid(z)


def _mod_kernel(cond_ref, w_ref, b_ref, o_ref):
    o_ref[0] = jnp.dot(cond_ref[...], w_ref[0], preferred_element_type=F32,
                       precision=HIGHEST) + b_ref[0]


def _conditioning(c, w_cond, b_cond, w_mod, b_mod):
    c8 = jnp.broadcast_to(c, (8, D_MODEL))
    cond = pl.pallas_call(
        _cond_kernel,
        out_shape=jax.ShapeDtypeStruct((8, COND_RANK), F32),
        compiler_params=_cparams(None),
        name="cond",
    )(c8, w_cond, b_cond.reshape(1, COND_RANK))
    n_mod = 6 * D_MODEL
    tn = 3072
    mod = pl.pallas_call(
        _mod_kernel,
        grid=(DEPTH, n_mod // tn),
        in_specs=[pl.BlockSpec((8, COND_RANK), lambda l, n: (0, 0)),
                  pl.BlockSpec((1, COND_RANK, tn), lambda l, n: (l, 0, n)),
                  pl.BlockSpec((1, 1, tn), lambda l, n: (l, 0, n))],
        out_specs=pl.BlockSpec((1, 8, tn), lambda l, n: (l, 0, n)),
        out_shape=jax.ShapeDtypeStruct((DEPTH, 8, n_mod), F32),
        compiler_params=_cparams(("arbitrary", "arbitrary")),
        name="mod",
    )(cond, w_mod, b_mod.reshape(DEPTH, 1, n_mod))
    return mod[:, 0, :].reshape(DEPTH, 6, 1, D_MODEL)


def _modulate_kernel(x_ref, scale_ref, shift_ref, o_ref):
    o_ref[...] = (x_ref[...] * (1.0 + scale_ref[...]) + shift_ref[...]).astype(o_ref.dtype)


def _modulate(x, scale, shift, tm=512):
    s = x.shape[0]
    vec = pl.BlockSpec((1, D_MODEL), lambda m: (0, 0))
    return pl.pallas_call(
        _modulate_kernel,
        grid=(s // tm,),
        in_specs=[pl.BlockSpec((tm, D_MODEL), lambda m: (m, 0)), vec, vec],
        out_specs=pl.BlockSpec((tm, D_MODEL), lambda m: (m, 0)),
        out_shape=jax.ShapeDtypeStruct((s, D_MODEL), BF16),
        compiler_params=_cparams(("arbitrary",)),
        name="modulate",
    )(x, scale, shift)


def _route_top2(logits):
    lane = lax.broadcasted_iota(jnp.int32, logits.shape, 1)
    m1 = jnp.max(logits, axis=-1, keepdims=True)
    i1 = jnp.min(jnp.where(logits == m1, lane, LANES), axis=-1, keepdims=True)
    rest = jnp.where(lane == i1, NEG_INF, logits)
    m2 = jnp.max(rest, axis=-1, keepdims=True)
    i2 = jnp.min(jnp.where(rest == m2, lane, LANES), axis=-1, keepdims=True)
    e2 = jnp.exp(m2 - m1)
    denom = 1.0 + e2
    return jnp.where(lane == i1, 1.0 / denom, 0.0) + jnp.where(lane == i2, e2 / denom, 0.0)


def _ln_kernel(*refs, has_next, has_router):
    x_ref, y_ref, gate_ref, g_ref, b_ref = refs[:5]
    pos = 5
    if has_next:
        scale_ref, shift_ref = refs[pos:pos + 2]
        pos += 2
    if has_router:
        rw_ref, rb_ref = refs[pos:pos + 2]
        pos += 2
    xo_ref = refs[pos]
    pos += 1
    z = DN_ALPHA * x_ref[...] + (1.0 + gate_ref[...]) * y_ref[...].astype(F32)
    mu = jnp.mean(z, axis=-1, keepdims=True)
    zc = z - mu
    var = jnp.mean(zc * zc, axis=-1, keepdims=True)
    xn = zc * lax.rsqrt(var + LN_EPS) * g_ref[...] + b_ref[...]
    xo_ref[...] = xn
    if has_next:
        ho_ref = refs[pos]
        pos += 1
        h = xn * (1.0 + scale_ref[...]) + shift_ref[...]
        ho_ref[...] = h.astype(ho_ref.dtype)
        if has_router:
            co_ref = refs[pos]
            logits = jnp.dot(h, rw_ref[...], preferred_element_type=F32, precision=HIGHEST)
            lane = lax.broadcasted_iota(jnp.int32, logits.shape, 1)
            logits = jnp.where(lane < N_EXPERTS, logits + rb_ref[...], NEG_INF)
            co_ref[...] = _route_top2(logits)


def _deepnorm_ln(x, y, gate, g, b, nxt=None, router=None, tm=256):
    s = x.shape[0]
    row = pl.BlockSpec((tm, D_MODEL), lambda m: (m, 0))
    vec = pl.BlockSpec((1, D_MODEL), lambda m: (0, 0))
    args = [x, y, gate, g.reshape(1, D_MODEL), b.reshape(1, D_MODEL)]
    in_specs = [row, row, vec, vec, vec]
    out_shape = [jax.ShapeDtypeStruct((s, D_MODEL), F32)]
    out_specs = [row]
    if nxt is not None:
        args += [nxt[0], nxt[1]]
        in_specs += [vec, vec]
        out_shape.append(jax.ShapeDtypeStruct((s, D_MODEL), BF16))
        out_specs.append(row)
    if router is not None:
        rw, rb = router
        rw_pad = jnp.pad(rw, ((0, 0), (0, LANES - N_EXPERTS)))
        rb_pad = jnp.pad(rb, (0, LANES - N_EXPERTS)).reshape(1, LANES)
        args += [rw_pad, rb_pad]
        in_specs += [pl.BlockSpec((D_MODEL, LANES), lambda m: (0, 0)),
                     pl.BlockSpec((1, LANES), lambda m: (0, 0))]
        out_shape.append(jax.ShapeDtypeStruct((s, LANES), F32))
        out_specs.append(pl.BlockSpec((tm, LANES), lambda m: (m, 0)))
    return pl.pallas_call(
        functools.partial(_ln_kernel, has_next=nxt is not None, has_router=router is not None),
        grid=(s // tm,),
        in_specs=in_specs,
        out_specs=out_specs,
        out_shape=out_shape,
        compiler_params=_cparams(("arbitrary",)),
        name="deepnorm_ln",
    )(*args)


def _matmul_kernel(a_ref, w_ref, o_ref):
    o_ref[...] = jnp.dot(a_ref[...], w_ref[...], preferred_element_type=F32).astype(o_ref.dtype)


def _matmul(a, w, tm, tn, out_dtype=BF16, name="matmul"):
    m, k = a.shape
    n = w.shape[1]
    return pl.pallas_call(
        _matmul_kernel,
        grid=(n // tn, m // tm),
        in_specs=[pl.BlockSpec((tm, k), lambda j, i: (i, 0)),
                  pl.BlockSpec((k, tn), lambda j, i: (0, j))],
        out_specs=pl.BlockSpec((tm, tn), lambda j, i: (i, j)),
        out_shape=jax.ShapeDtypeStruct((m, n), out_dtype),
        compiler_params=_cparams(("arbitrary", "arbitrary")),
        name=name,
    )(a, w)


def _rel_bucket(dist):
    n = jnp.maximum(dist, 0)
    max_exact = REL_BUCKETS // 2
    nf = jnp.maximum(n, 1).astype(F32)
    large = max_exact + (jnp.log(nf / max_exact) / math.log(REL_MAX_DIST / max_exact)
                         * (REL_BUCKETS - max_exact)).astype(jnp.int32)
    large = jnp.minimum(large, REL_BUCKETS - 1)
    return jnp.where(n < max_exact, n, large)


def _moba_kernel(relb_ref, q_ref, k_ref, vt_ref, o_ref,
                 kmean_ref, bown_ref, bprev_ref, sel_ref, m_ref, acc_ref, *, nb):
    g = pl.program_id(0)
    j = pl.program_id(1)
    blk = MOBA_BLOCK
    dh = ATT_HEAD_DIM
    scale = dh ** -0.5
    ln2 = math.log(2.0)
    nt = (((1,), (1,)), ((), ()))
    heads = range(MOBA_HEAD_GROUP)
    key_i = lax.broadcasted_iota(jnp.int32, (blk, blk), 0)
    qry_i = lax.broadcasted_iota(jnp.int32, (blk, blk), 1)

    @pl.when(j == 0)
    def _():
        for hh in heads:
            head = g * MOBA_HEAD_GROUP + hh
            kf = k_ref[:, hh * dh:(hh + 1) * dh].astype(F32).reshape(nb, blk, dh)
            kmean_ref[hh] = jnp.mean(kf, axis=1)
            for ref, off in ((bown_ref, 0), (bprev_ref, blk)):
                bucket = _rel_bucket(qry_i - key_i + off)
                bias = jnp.zeros((blk, blk), F32)
                for b in range(REL_BUCKETS):
                    bias = jnp.where(bucket == b, relb_ref[head, b], bias)
                ref[hh] = bias

    row0 = pl.multiple_of(j * blk, blk)
    blk_i = lax.broadcasted_iota(jnp.int32, (nb, blk), 0)
    q2 = []
    for hh in heads:
        q = q_ref[:, hh * dh:(hh + 1) * dh]
        q2.append((q.astype(F32) * (scale / ln2)).astype(BF16))
        score = lax.dot_general(kmean_ref[hh], q.astype(F32), nt,
                                preferred_element_type=F32, precision=HIGHEST)
        sc = jnp.where(blk_i < j, score, NEG_INF)
        seladd = jnp.full((nb, blk), NEG_INF, F32)
        for _ in range(MOBA_TOPK):
            mx = jnp.max(sc, axis=0, keepdims=True)
            cand = jnp.where(sc == mx, blk_i, nb)
            cand = jnp.where(mx > NEG_INF, cand, nb)
            idx = jnp.min(cand, axis=0, keepdims=True)
            pick = blk_i == idx
            seladd = jnp.where(pick, 0.0, seladd)
            sc = jnp.where(pick, NEG_INF, sc)
        sel_ref[hh] = seladd

    def scores(kb_row0, hh):
        return lax.dot_general(k_ref[pl.ds(kb_row0, blk), hh * dh:(hh + 1) * dh], q2[hh], nt,
                               preferred_element_type=F32)

    def weighted_values(kb, hh, p):
        return jnp.dot(vt_ref[kb, hh], p.astype(BF16), preferred_element_type=F32)

    for hh in heads:
        s = scores(row0, hh) * ln2 + bown_ref[hh]
        s = jnp.where(key_i <= qry_i, s, NEG_INF)
        m0 = jnp.max(s, axis=0, keepdims=True)
        m_ref[hh] = m0
        acc_ref[hh] = weighted_values(j, hh, jnp.exp(s - m0))

    def merge(hh, kb, m_blk, acc_blk):
        m_blk = m_blk + sel_ref[hh, pl.ds(kb, 1), :]
        m_old = m_ref[hh]
        m_new = jnp.maximum(m_old, m_blk)
        m_ref[hh] = m_new
        acc_ref[hh] = jnp.exp(m_old - m_new) * acc_ref[hh] + jnp.exp(m_blk - m_new) * acc_blk

    def far_block(kb, carry):
        r0 = pl.multiple_of(kb * blk, blk)
        s2 = [scores(r0, hh) for hh in heads]
        m2 = [jnp.max(s2[hh], axis=0, keepdims=True) for hh in heads]
        acc_blk = [weighted_values(kb, hh, jnp.exp2(s2[hh] - m2[hh])) for hh in heads]
        for hh in heads:
            far_bias = relb_ref[g * MOBA_HEAD_GROUP + hh, REL_BUCKETS - 1]
            merge(hh, kb, m2[hh] * ln2 + far_bias, acc_blk[hh])
        return carry

    lax.fori_loop(0, j - 1, far_block, 0)
    kb = jnp.maximum(j - 1, 0)
    r0 = pl.multiple_of(kb * blk, blk)
    for hh in heads:
        s = scores(r0, hh) * ln2 + bprev_ref[hh]
        m_blk = jnp.max(s, axis=0, keepdims=True)
        merge(hh, kb, m_blk, weighted_values(kb, hh, jnp.exp(s - m_blk)))
        acc = acc_ref[hh]
        o_ref[:, hh * dh:(hh + 1) * dh] = (acc[0:dh] / acc[dh:dh + 1]).T.astype(o_ref.dtype)


def _moba(proj, rel_bias):
    s = proj.shape[0]
    nb = s // MOBA_BLOCK
    hg = MOBA_HEAD_GROUP
    gw = hg * ATT_HEAD_DIM
    n_groups = ATT_HEADS // hg
    dhp = ATT_HEAD_DIM + MOBA_ONES_ROWS
    v_t = proj[:, 2 * BRANCH_WIDTH:3 * BRANCH_WIDTH].reshape(nb, MOBA_BLOCK, ATT_HEADS, ATT_HEAD_DIM)
    v_t = jnp.concatenate([v_t.transpose(0, 2, 3, 1),
                           jnp.ones((nb, ATT_HEADS, MOBA_ONES_ROWS, MOBA_BLOCK), BF16)], axis=2)
    return pl.pallas_call(
        functools.partial(_moba_kernel, nb=nb),
        grid=(n_groups, nb),
        in_specs=[pl.BlockSpec(memory_space=pltpu.SMEM),
                  pl.BlockSpec((MOBA_BLOCK, gw), lambda g, j: (j, g)),
                  pl.BlockSpec((s, gw), lambda g, j: (0, n_groups + g)),
                  pl.BlockSpec((nb, hg, dhp, MOBA_BLOCK), lambda g, j: (0, g, 0, 0))],
        out_specs=pl.BlockSpec((MOBA_BLOCK, gw), lambda g, j: (j, g)),
        out_shape=jax.ShapeDtypeStruct((s, BRANCH_WIDTH), BF16),
        scratch_shapes=[pltpu.VMEM((hg, nb, ATT_HEAD_DIM), F32),
                        pltpu.VMEM((hg, MOBA_BLOCK, MOBA_BLOCK), F32),
                        pltpu.VMEM((hg, MOBA_BLOCK, MOBA_BLOCK), F32),
                        pltpu.VMEM((hg, nb, MOBA_BLOCK), F32),
                        pltpu.VMEM((hg, 1, MOBA_BLOCK), F32),
                        pltpu.VMEM((hg, dhp, MOBA_BLOCK), F32)],
        compiler_params=_cparams(("arbitrary", "arbitrary")),
        name="moba",
    )(rel_bias.T, proj, proj, v_t)


def _s5_tables(lam_re, lam_im, log_dt, b_re, b_im, c_re, c_im):
    t_len = S5_CHUNK
    g_cnt, p_cnt, h_cnt = S5_GROUPS, S5_STATE, S5_GROUP
    tg = S5_TILE_GROUPS
    nt = g_cnt // tg
    dt = jnp.exp(log_dt)[:, None]
    ar, ai = lam_re * dt, lam_im * dt

    def lam_pow(steps):
        st = steps.astype(F32)[:, None, None]
        mag = jnp.exp(st * ar)
        return mag * jnp.cos(st * ai), mag * jnp.sin(st * ai)

    pr, pi = lam_pow(jnp.arange(t_len + 1))
    qr, qi = lam_pow((t_len - 1) - jnp.arange(t_len))
    nr, ni = pr[1] - 1.0, pi[1]
    den = lam_re * lam_re + lam_im * lam_im
    rr, ri = (nr * lam_re + ni * lam_im) / den, (ni * lam_re - nr * lam_im) / den
    bbr = rr[..., None] * b_re - ri[..., None] * b_im
    bbi = rr[..., None] * b_im + ri[..., None] * b_re

    def c_times(xr, xi):
        return (c_re[None] * xr[:, :, None, :] - c_im[None] * xi[:, :, None, :],
                c_re[None] * xi[:, :, None, :] + c_im[None] * xr[:, :, None, :])

    cpr, cpi = c_times(pr, pi)
    cqr, cqi = c_times(qr, qi)
    kc = (jnp.einsum('tghp,gpk->tgkh', cqr, bbr, precision=HIGHEST)
          - jnp.einsum('tghp,gpk->tgkh', cqi, bbi, precision=HIGHEST))
    eye = jnp.eye(tg, dtype=F32)
    kd = jnp.einsum('tcgkh,gf->ctgkfh', kc.reshape(t_len, nt, tg, h_cnt, h_cnt), eye)
    kd = kd.reshape(nt, t_len * LANES, LANES)
    qr4, qi4 = qr[:, :, :, None], qi[:, :, :, None]
    bzr = (qr4 * bbr[None] - qi4 * bbi[None]).transpose(0, 1, 3, 2)
    bzi = (qr4 * bbi[None] + qi4 * bbr[None]).transpose(0, 1, 3, 2)
    bz = jnp.stack([bzr, bzi], axis=3).reshape(t_len, nt, tg, h_cnt, 2, p_cnt)
    bz = jnp.einsum('tcgkxp,gf->ctgkxfp', bz, eye).reshape(nt, t_len * LANES, 2 * tg * p_cnt)
    cm = jnp.stack([cpr[1:], -cpi[1:]], axis=2).reshape(t_len, nt, tg, 2, h_cnt, p_cnt)
    cm = jnp.einsum('tcgxhp,gf->ctxfpgh', cm, eye).reshape(nt, t_len, 2 * tg * p_cnt, LANES)
    a_re = pr[t_len].reshape(nt, 1, tg * p_cnt)
    a_im = pi[t_len].reshape(nt, 1, tg * p_cnt)
    return kd.astype(BF16), bz.astype(BF16), cm.astype(BF16), a_re, a_im


def _s5_kernel(u_ref, kd_ref, bz_ref, cm_ref, are_ref, aim_ref, d_ref, y_ref,
               uf_ref, ucat_ref, z_ref, hc_ref, *, nc):
    t_len = S5_CHUNK
    ns = S5_TILE_STATE
    uf_ref[...] = u_ref[...].astype(F32)
    for s in range(t_len):
        ucat_ref[:, s * LANES:(s + 1) * LANES] = uf_ref[pl.ds(s, nc, stride=t_len), :].astype(BF16)
    z_ref[...] = jnp.dot(ucat_ref[...], bz_ref[0], preferred_element_type=F32)
    a_re = are_ref[0]
    a_im = aim_ref[0]

    def step(c, carry):
        h_re, h_im = carry
        hc_ref[pl.ds(c, 1), 0:ns] = h_re
        hc_ref[pl.ds(c, 1), ns:2 * ns] = h_im
        z_re = z_ref[pl.ds(c, 1), 0:ns]
        z_im = z_ref[pl.ds(c, 1), ns:2 * ns]
        return (a_re * h_re - a_im * h_im + z_re, a_re * h_im + a_im * h_re + z_im)

    zero = jnp.zeros((1, ns), F32)
    lax.fori_loop(0, nc, step, (zero, zero))
    hc = hc_ref[...].astype(BF16)
    d_skip = d_ref[...]
    for t in range(t_len):
        acc = jnp.dot(hc, cm_ref[0, t], preferred_element_type=F32)
        acc += jnp.dot(ucat_ref[:, 0:(t + 1) * LANES],
                       kd_ref[0, (t_len - 1 - t) * LANES:t_len * LANES, :],
                       preferred_element_type=F32)
        acc += d_skip * uf_ref[pl.ds(t, nc, stride=t_len), :]
        y_ref[pl.ds(t, nc, stride=t_len), :] = jax.nn.gelu(acc)


def _s5_glu_kernel(y_ref, yn_ref, w_ref, b_ref, o_ref):
    z = jnp.dot(y_ref[...].astype(BF16), w_ref[...], preferred_element_type=F32) + b_ref[...]
    o_ref[...] = (yn_ref[...] * jax.nn.sigmoid(z)).astype(o_ref.dtype)


def _s5(proj, lam_re, lam_im, log_dt, b_re, b_im, c_re, c_im, d_skip, w_glu, b_glu):
    s = proj.shape[0]
    nc = s // S5_CHUNK
    nt = S5_GROUPS // S5_TILE_GROUPS
    kd, bz, cm, a_re, a_im = _s5_tables(lam_re, lam_im, log_dt, b_re, b_im, c_re, c_im)
    u_col = 3 * BRANCH_WIDTH // LANES
    ns2 = 2 * S5_TILE_STATE
    y = pl.pallas_call(
        functools.partial(_s5_kernel, nc=nc),
        grid=(nt,),
        in_specs=[pl.BlockSpec((s, LANES), lambda c: (0, u_col + c)),
                  pl.BlockSpec((1, S5_CHUNK * LANES, LANES), lambda c: (c, 0, 0)),
                  pl.BlockSpec((1, S5_CHUNK * LANES, ns2), lambda c: (c, 0, 0)),
                  pl.BlockSpec((1, S5_CHUNK, ns2, LANES), lambda c: (c, 0, 0, 0)),
                  pl.BlockSpec((1, 1, S5_TILE_STATE), lambda c: (c, 0, 0)),
                  pl.BlockSpec((1, 1, S5_TILE_STATE), lambda c: (c, 0, 0)),
                  pl.BlockSpec((1, LANES), lambda c: (0, c))],
        out_specs=pl.BlockSpec((s, LANES), lambda c: (0, c)),
        out_shape=jax.ShapeDtypeStruct((s, BRANCH_WIDTH), F32),
        scratch_shapes=[pltpu.VMEM((s, LANES), F32),
                        pltpu.VMEM((nc, S5_CHUNK * LANES), BF16),
                        pltpu.VMEM((nc, ns2), F32),
                        pltpu.VMEM((nc, ns2), F32)],
        compiler_params=_cparams(("arbitrary",)),
        name="s5_scan",
    )(proj, kd, bz, cm, a_re, a_im, d_skip.reshape(1, BRANCH_WIDTH))
    tm, tn = 512, 512
    return pl.pallas_call(
        _s5_glu_kernel,
        grid=(BRANCH_WIDTH // tn, s // tm),
        in_specs=[pl.BlockSpec((tm, BRANCH_WIDTH), lambda j, i: (i, 0)),
                  pl.BlockSpec((tm, tn), lambda j, i: (i, j)),
                  pl.BlockSpec((BRANCH_WIDTH, tn), lambda j, i: (0, j)),
                  pl.BlockSpec((1, tn), lambda j, i: (0, j))],
        out_specs=pl.BlockSpec((tm, tn), lambda j, i: (i, j)),
        out_shape=jax.ShapeDtypeStruct((s, BRANCH_WIDTH), BF16),
        compiler_params=_cparams(("arbitrary", "arbitrary")),
        name="s5_glu",
    )(y, y, w_glu.astype(BF16), b_glu.reshape(1, BRANCH_WIDTH))


def _gla_kernel(q_ref, k_ref, v_ref, r_ref, tail_ref, wg_ref, bg_ref, ng_ref, o_ref, st_ref):
    dk = GLA_KEY // GLA_HEADS
    dv = GLA_VAL // GLA_HEADS
    cs = GLA_CHUNK
    sub = GLA_SUB
    nt = (((1,), (1,)), ((), ()))
    tn = (((0,), (0,)), ((), ()))

    @pl.when(pl.program_id(0) == 0)
    def _():
        st_ref[...] = jnp.zeros_like(st_ref)

    gate_in = jnp.dot(tail_ref[...], wg_ref[...], preferred_element_type=F32) + bg_ref[...]
    log_a = jax.nn.log_sigmoid(gate_in) / GLA_GATE_TAU
    ri = lax.broadcasted_iota(jnp.int32, (cs, cs), 0)
    ci = lax.broadcasted_iota(jnp.int32, (cs, cs), 1)
    tril = (ri >= ci).astype(F32)
    bcum_all = jnp.dot(tril, log_a, preferred_element_type=F32, precision=HIGHEST)
    sub_row = lax.broadcasted_iota(jnp.int32, (sub, 1), 0)

    for h in range(GLA_HEADS):
        q = q_ref[:, h * dk:(h + 1) * dk].astype(F32) * dk ** -0.5
        k = k_ref[:, h * dk:(h + 1) * dk].astype(F32)
        v_bf = v_ref[:, h * dv:(h + 1) * dv]
        v = v_bf.astype(F32)
        bc = bcum_all[:, h * dk:(h + 1) * dk]
        state = st_ref[h]
        o_inter = lax.dot_general((q * jnp.exp(bc)).astype(BF16), state.astype(BF16), nt,
                                  preferred_element_type=F32)
        parts = []
        for i in range(cs // sub):
            lo = i * sub
            b_i, q_i, k_i, v_i = bc[lo:lo + sub], q[lo:lo + sub], k[lo:lo + sub], v[lo:lo + sub]
            o_i = o_inter[lo:lo + sub]
            if i > 0:
                ref = bc[lo - 1:lo]
                q_h = (q_i * jnp.exp(b_i - ref)).astype(BF16)
                k_h = (k[:lo] * jnp.exp(ref - bc[:lo])).astype(BF16)
                attn = lax.dot_general(q_h, k_h, nt, preferred_element_type=F32)
                o_i = o_i + jnp.dot(attn.astype(BF16), v_bf[:lo], preferred_element_type=F32)
            for dlt in range(sub):
                if dlt == 0:
                    a = jnp.sum(q_i * k_i, axis=-1, keepdims=True)
                    o_i = o_i + a * v_i
                else:
                    b_s = pltpu.roll(b_i, dlt, 0)
                    k_s = pltpu.roll(k_i, dlt, 0)
                    v_s = pltpu.roll(v_i, dlt, 0)
                    e = jnp.exp(jnp.minimum(b_i - b_s, 0.0))
                    a = jnp.sum(q_i * k_s * e, axis=-1, keepdims=True)
                    a = jnp.where(sub_row >= dlt, a, 0.0)
                    o_i = o_i + a * v_s
            parts.append(o_i)
        o = jnp.concatenate(parts, axis=0)
        b_last = bc[cs - 1:cs]
        k_dec = (k * jnp.exp(b_last - bc)).astype(BF16)
        st_ref[h] = state * jnp.exp(b_last) + lax.dot_general(v_bf, k_dec, tn,
                                                             preferred_element_type=F32)
        o = o * lax.rsqrt(jnp.mean(o * o, axis=-1, keepdims=True) + NORM_EPS)
        o = o * ng_ref[:, h * dv:(h + 1) * dv]
        r = r_ref[:, h * dv:(h + 1) * dv].astype(F32)
        o_ref[:, h * dv:(h + 1) * dv] = (o * (r * jax.nn.sigmoid(r))).astype(o_ref.dtype)


def _gla(proj, w_gate, b_gate, norm_g):
    s = proj.shape[0]
    cs = GLA_CHUNK
    wg_pad = jnp.pad(w_gate, ((0, TAIL_W - GLA_GATE_RANK), (0, 0))).astype(BF16)
    return pl.pallas_call(
        _gla_kernel,
        grid=(s // cs,),
        in_specs=[pl.BlockSpec((cs, GLA_KEY), lambda c: (c, 4 * BRANCH_WIDTH // GLA_KEY)),
                  pl.BlockSpec((cs, GLA_KEY), lambda c: (c, 4 * BRANCH_WIDTH // GLA_KEY + 1)),
                  pl.BlockSpec((cs, GLA_VAL), lambda c: (c, 5)),
                  pl.BlockSpec((cs, GLA_VAL), lambda c: (c, 6)),
                  pl.BlockSpec((cs, TAIL_W), lambda c: (c, TAIL_COL // TAIL_W)),
                  pl.BlockSpec((TAIL_W, GLA_KEY), lambda c: (0, 0)),
                  pl.BlockSpec((1, GLA_KEY), lambda c: (0, 0)),
                  pl.BlockSpec((1, GLA_VAL), lambda c: (0, 0))],
        out_specs=pl.BlockSpec((cs, GLA_VAL), lambda c: (c, 0)),
        out_shape=jax.ShapeDtypeStruct((s, GLA_VAL), BF16),
        scratch_shapes=[pltpu.VMEM((GLA_HEADS, GLA_VAL // GLA_HEADS, GLA_KEY // GLA_HEADS), F32)],
        compiler_params=_cparams(("arbitrary",)),
        name="gla",
    )(proj, proj, proj, proj, proj, wg_pad, b_gate.reshape(1, GLA_KEY), norm_g.reshape(1, GLA_VAL))


def _merge_kernel(ya_ref, ys_ref, yg_ref, tail_ref, wb_ref, wg_ref, bg_ref, o_ref):
    tail = tail_ref[...]
    acc = None
    for n, y_ref in enumerate((ya_ref, ys_ref, yg_ref)):
        up = jnp.dot(y_ref[...], wb_ref[n], preferred_element_type=F32)
        gate = jax.nn.sigmoid(jnp.dot(tail, wg_ref[n], preferred_element_type=F32) + bg_ref[n])
        acc = gate * up if acc is None else acc + gate * up
    o_ref[...] = acc.astype(o_ref.dtype)


def _merge(y_att, y_s5, y_gla, proj, w_branch, w_merge_gate, b_merge_gate, tm=512, tn=512):
    s = proj.shape[0]
    wb = w_branch.astype(BF16)
    wg = w_merge_gate.reshape(MERGE_RANK, N_BRANCH, D_MODEL).transpose(1, 0, 2)
    wg = jnp.pad(wg, ((0, 0), (GLA_GATE_RANK, TAIL_W - GLA_GATE_RANK - MERGE_RANK), (0, 0))).astype(BF16)
    bg = b_merge_gate.reshape(N_BRANCH, 1, D_MODEL)
    ybs = pl.BlockSpec((tm, BRANCH_WIDTH), lambda j, i: (i, 0))
    return pl.pallas_call(
        _merge_kernel,
        grid=(D_MODEL // tn, s // tm),
        in_specs=[ybs, ybs, ybs,
                  pl.BlockSpec((tm, TAIL_W), lambda j, i: (i, TAIL_COL // TAIL_W)),
                  pl.BlockSpec((N_BRANCH, BRANCH_WIDTH, tn), lambda j, i: (0, 0, j)),
                  pl.BlockSpec((N_BRANCH, TAIL_W, tn), lambda j, i: (0, 0, j)),
                  pl.BlockSpec((N_BRANCH, 1, tn), lambda j, i: (0, 0, j))],
        out_specs=pl.BlockSpec((tm, tn), lambda j, i: (i, j)),
        out_shape=jax.ShapeDtypeStruct((s, D_MODEL), BF16),
        compiler_params=_cparams(("arbitrary", "arbitrary")),
        name="merge",
    )(y_att, y_s5, y_gla, proj, wb, wg, bg)


def _swiglu_kernel(a_ref, w1_ref, w3_ref, o_ref):
    a = a_ref[...]
    g = jnp.dot(a, w1_ref[...], preferred_element_type=F32)
    u = jnp.dot(a, w3_ref[...], preferred_element_type=F32)
    o_ref[...] = (g * jax.nn.sigmoid(g) * u).astype(o_ref.dtype)


def _swiglu_hidden(a, w1, w3, tm=512, tn=512):
    m, k = a.shape
    n = w1.shape[1]
    wspec = pl.BlockSpec((k, tn), lambda j, i: (0, j))
    return pl.pallas_call(
        _swiglu_kernel,
        grid=(n // tn, m // tm),
        in_specs=[pl.BlockSpec((tm, k), lambda j, i: (i, 0)), wspec, wspec],
        out_specs=pl.BlockSpec((tm, tn), lambda j, i: (i, j)),
        out_shape=jax.ShapeDtypeStruct((m, n), BF16),
        compiler_params=_cparams(("arbitrary", "arbitrary")),
        name="swiglu_hidden",
    )(a, w1, w3)


def _moe_hidden_kernel(a_ref, comb_ref, w1_ref, w3_ref, o_ref, *, tiles_per_expert):
    e = pl.program_id(0) // tiles_per_expert
    a = a_ref[...]
    g = jnp.dot(a, w1_ref[0], preferred_element_type=F32)
    u = jnp.dot(a, w3_ref[0], preferred_element_type=F32)
    comb = comb_ref[...]
    lane = lax.broadcasted_iota(jnp.int32, comb.shape, 1)
    w = jnp.sum(jnp.where(lane == e, comb, 0.0), axis=-1, keepdims=True)
    o_ref[...] = (g * jax.nn.sigmoid(g) * u * w).astype(o_ref.dtype)


def _moe_hidden(a, comb, w1, w3, tm=512, tn=256):
    m, k = a.shape
    tpe = D_FF_EXPERT // tn
    wspec = pl.BlockSpec((1, k, tn), lambda j, i: (j // tpe, 0, j % tpe))
    return pl.pallas_call(
        functools.partial(_moe_hidden_kernel, tiles_per_expert=tpe),
        grid=(N_EXPERTS * tpe, m // tm),
        in_specs=[pl.BlockSpec((tm, k), lambda j, i: (i, 0)),
                  pl.BlockSpec((tm, LANES), lambda j, i: (i, 0)), wspec, wspec],
        out_specs=pl.BlockSpec((tm, tn), lambda j, i: (i, j)),
        out_shape=jax.ShapeDtypeStruct((m, N_EXPERTS * D_FF_EXPERT), BF16),
        compiler_params=_cparams(("arbitrary", "arbitrary")),
        name="moe_hidden",
    )(a, comb, w1, w3)


def kernel(x, c, w_cond, b_cond, rel_bias, w_mod, b_mod, w_in, s5_lambda_re, s5_lambda_im, s5_log_dt, s5_b_re, s5_b_im, s5_c_re, s5_c_im, s5_d, s5_w_glu, s5_b_glu, gla_w_gate, gla_b_gate, gla_norm_g, w_branch, w_merge_gate, b_merge_gate, w_out, ln1_g, ln1_b, ffn_w1, ffn_w3, ffn_w2, router_w, router_b, exp_w1, exp_w3, exp_w2, ln2_g, ln2_b):
    bsz, seq, _ = x.shape
    assert bsz == 1
    mod = _conditioning(c, w_cond, b_cond, w_mod, b_mod)
    xs = x.reshape(seq, D_MODEL)
    hm = _modulate(xs, mod[0, 1], mod[0, 0])
    for l in range(DEPTH):
        shift_f, scale_f, gate_m, gate_f = mod[l, 3], mod[l, 4], mod[l, 2], mod[l, 5]
        w_in_b = jnp.pad(w_in[l].astype(BF16), ((0, 0), (0, D_IN_PAD - D_IN)))
        proj = _matmul(hm, w_in_b, tm=512, tn=512, name="in_proj")
        y_att = _moba(proj, rel_bias)
        y_s5 = _s5(proj, s5_lambda_re[l], s5_lambda_im[l], s5_log_dt[l], s5_b_re[l], s5_b_im[l],
                   s5_c_re[l], s5_c_im[l], s5_d[l], s5_w_glu[l], s5_b_glu[l])
        y_gla = _gla(proj, gla_w_gate[l], gla_b_gate[l], gla_norm_g[l])
        merged = _merge(y_att, y_s5, y_gla, proj, w_branch[l], w_merge_gate[l], b_merge_gate[l])
        y = _matmul(merged, w_out[l].astype(BF16), tm=512, tn=1024, name="out_proj")
        dense = l % 2 == 0
        router = None if dense else (router_w[l // 2], router_b[l // 2])
        outs = _deepnorm_ln(xs, y, gate_m, ln1_g[l], ln1_b[l], nxt=(scale_f, shift_f), router=router)
        xs, hf = outs[0], outs[1]
        if dense:
            hid = _swiglu_hidden(hf, ffn_w1[l // 2].astype(BF16), ffn_w3[l // 2].astype(BF16))
            f = _matmul(hid, ffn_w2[l // 2].astype(BF16), tm=256, tn=1024, name="ffn_down")
        else:
            hid = _moe_hidden(hf, outs[2], exp_w1[l // 2].astype(BF16), exp_w3[l // 2].astype(BF16))
            w2 = exp_w2[l // 2].astype(BF16).reshape(N_EXPERTS * D_FF_EXPERT, D_MODEL)
            f = _matmul(hid, w2, tm=256, tn=512, name="moe_down")
        if l + 1 < DEPTH:
            xs, hm = _deepnorm_ln(xs, f, gate_f, ln2_g[l], ln2_b[l],
                                  nxt=(mod[l + 1, 1], mod[l + 1, 0]))
        else:
            (xs,) = _deepnorm_ln(xs, f, gate_f, ln2_g[l], ln2_b[l])
    return xs.reshape(bsz, seq, D_MODEL)
```

```python
import functools
import math

import jax
import jax.numpy as jnp
from jax import lax
from jax.experimental import pallas as pl
from jax.experimental.pallas import tpu as pltpu

F32 = jnp.float32
BF16 = jnp.bfloat16
HIGHEST = lax.Precision.HIGHEST

D_MODEL = 4096
DEPTH = 4
BRANCH_WIDTH = 1024
N_BRANCH = 3
ATT_HEADS = 8
ATT_HEAD_DIM = 128
MOBA_BLOCK = 256
MOBA_TOPK = 3
MOBA_HEAD_GROUP = 4
MOBA_ONES_ROWS = 16
REL_BUCKETS = 32
REL_MAX_DIST = 128
S5_GROUP = 16
S5_GROUPS = 64
S5_STATE = 64
S5_CHUNK = 16
S5_TILE_GROUPS = 8
S5_TILE_STATE = S5_TILE_GROUPS * S5_STATE
GLA_HEADS = 4
GLA_KEY = 512
GLA_VAL = 1024
GLA_GATE_RANK = 16
GLA_GATE_TAU = 16.0
GLA_CHUNK = 64
GLA_SUB = 16
MERGE_RANK = 256
COND_RANK = 512
D_FF = 8192
N_EXPERTS = 8
TOP_K = 2
ROUTE_ID = 8
ROUTE_W = 10
MOE_TILE = 256
MOE_DMA_IN_FLIGHT = 32
D_FF_EXPERT = 1792
DN_ALPHA = (2 * DEPTH) ** 0.25
LN_EPS = 1e-5
NORM_EPS = 1e-6

D_IN = 7440
D_IN_PAD = 7680
TAIL_COL = 7168
TAIL_W = 512
LANES = 128
VMEM_LIMIT = 56 * 1024 * 1024
NEG_INF = float("-inf")


def _cparams(sem):
    return pltpu.CompilerParams(dimension_semantics=sem, vmem_limit_bytes=VMEM_LIMIT)


def _cond_kernel(c_ref, w_ref, b_ref, o_ref):
    z = jnp.dot(c_ref[...], w_ref[...], preferred_element_type=F32, precision=HIGHEST) + b_ref[...]
    o_ref[...] = z * jax.nn.sigmoid(z)


def _mod_kernel(cond_ref, w_ref, b_ref, o_ref):
    o_ref[0] = jnp.dot(cond_ref[...], w_ref[0], preferred_element_type=F32,
                       precision=HIGHEST) + b_ref[0]


def _conditioning(c, w_cond, b_cond, w_mod, b_mod):
    c8 = jnp.broadcast_to(c, (8, D_MODEL))
    cond = pl.pallas_call(
        _cond_kernel,
        out_shape=jax.ShapeDtypeStruct((8, COND_RANK), F32),
        compiler_params=_cparams(None),
        name="cond",
    )(c8, w_cond, b_cond.reshape(1, COND_RANK))
    n_mod = 6 * D_MODEL
    tn = 3072
    mod = pl.pallas_call(
        _mod_kernel,
        grid=(DEPTH, n_mod // tn),
        in_specs=[pl.BlockSpec((8, COND_RANK), lambda l, n: (0, 0)),
                  pl.BlockSpec((1, COND_RANK, tn), lambda l, n: (l, 0, n)),
                  pl.BlockSpec((1, 1, tn), lambda l, n: (l, 0, n))],
        out_specs=pl.BlockSpec((1, 8, tn), lambda l, n: (l, 0, n)),
        out_shape=jax.ShapeDtypeStruct((DEPTH, 8, n_mod), F32),
        compiler_params=_cparams(("arbitrary", "arbitrary")),
        name="mod",
    )(cond, w_mod, b_mod.reshape(DEPTH, 1, n_mod))
    return mod[:, 0, :].reshape(DEPTH, 6, 1, D_MODEL)


def _modulate_kernel(x_ref, scale_ref, shift_ref, o_ref):
    o_ref[...] = (x_ref[...] * (1.0 + scale_ref[...]) + shift_ref[...]).astype(o_ref.dtype)


def _modulate(x, scale, shift, tm=512):
    s = x.shape[0]
    vec = pl.BlockSpec((1, D_MODEL), lambda m: (0, 0))
    return pl.pallas_call(
        _modulate_kernel,
        grid=(s // tm,),
        in_specs=[pl.BlockSpec((tm, D_MODEL), lambda m: (m, 0)), vec, vec],
        out_specs=pl.BlockSpec((tm, D_MODEL), lambda m: (m, 0)),
        out_shape=jax.ShapeDtypeStruct((s, D_MODEL), BF16),
        compiler_params=_cparams(("arbitrary",)),
        name="modulate",
    )(x, scale, shift)


def _route_top2(logits):
    lane = lax.broadcasted_iota(jnp.int32, logits.shape, 1)
    m1 = jnp.max(logits, axis=-1, keepdims=True)
    i1 = jnp.min(jnp.where(logits == m1, lane, LANES), axis=-1, keepdims=True)
    rest = jnp.where(lane == i1, NEG_INF, logits)
    m2 = jnp.max(rest, axis=-1, keepdims=True)
    i2 = jnp.min(jnp.where(rest == m2, lane, LANES), axis=-1, keepdims=True)
    e2 = jnp.exp(m2 - m1)
    denom = 1.0 + e2
    rec = jnp.where(lane == ROUTE_ID, i1.astype(F32), 0.0)
    rec = jnp.where(lane == ROUTE_ID + 1, i2.astype(F32), rec)
    rec = jnp.where(lane == ROUTE_W, 1.0 / denom, rec)
    return jnp.where(lane == ROUTE_W + 1, e2 / denom, rec)


def _row_copy(src_ref, dst_ref, sem, src_row, dst_row):
    return pltpu.make_async_copy(src_ref.at[pl.ds(src_row, 1)], dst_ref.at[pl.ds(dst_row, 1)], sem)


def _smem_at(ref, i):
    return ref[i // LANES, i % LANES]


def _ln_kernel(*refs, has_next, has_router, moe_combine, tm):
    pos = 0
    if moe_combine:
        pos1_ref, pos2_ref, x_ref, route_ref, ys_ref = refs[:5]
        pos = 5
    else:
        x_ref, y_ref = refs[:2]
        pos = 2
    gate_ref, g_ref, b_ref = refs[pos:pos + 3]
    pos += 3
    if has_next:
        scale_ref, shift_ref = refs[pos:pos + 2]
        pos += 2
    if has_router:
        rw_ref, rb_ref = refs[pos:pos + 2]
        pos += 2
    xo_ref = refs[pos]
    pos += 1
    if moe_combine:
        buf1_ref, buf2_ref, sem = refs[-3:]
        base = pl.program_id(0) * tm

        def copies(r):
            return (_row_copy(ys_ref, buf1_ref, sem, _smem_at(pos1_ref, base + r), r),
                    _row_copy(ys_ref, buf2_ref, sem, _smem_at(pos2_ref, base + r), r))

        def start(r, carry):
            for cp in copies(r):
                cp.start()
            return carry

        def wait(r, carry):
            for cp in copies(r):
                cp.wait()
            return carry

        lax.fori_loop(0, tm, start, 0)
        lax.fori_loop(0, tm, wait, 0)
        route = route_ref[...]
        lane = lax.broadcasted_iota(jnp.int32, route.shape, 1)
        w1 = jnp.sum(jnp.where(lane == ROUTE_W, route, 0.0), axis=-1, keepdims=True)
        w2 = jnp.sum(jnp.where(lane == ROUTE_W + 1, route, 0.0), axis=-1, keepdims=True)
        y = w1 * buf1_ref[...] + w2 * buf2_ref[...]
    else:
        y = y_ref[...].astype(F32)
    z = DN_ALPHA * x_ref[...] + (1.0 + gate_ref[...]) * y
    mu = jnp.mean(z, axis=-1, keepdims=True)
    zc = z - mu
    var = jnp.mean(zc * zc, axis=-1, keepdims=True)
    xn = zc * lax.rsqrt(var + LN_EPS) * g_ref[...] + b_ref[...]
    xo_ref[...] = xn
    if has_next:
        ho_ref = refs[pos]
        pos += 1
        h = xn * (1.0 + scale_ref[...]) + shift_ref[...]
        ho_ref[...] = h.astype(ho_ref.dtype)
        if has_router:
            co_ref = refs[pos]
            logits = jnp.dot(h, rw_ref[...], preferred_element_type=F32, precision=HIGHEST)
            lane = lax.broadcasted_iota(jnp.int32, logits.shape, 1)
            logits = jnp.where(lane < N_EXPERTS, logits + rb_ref[...], NEG_INF)
            co_ref[...] = _route_top2(logits)


def _deepnorm_ln(x, y, gate, g, b, nxt=None, router=None, moe=None, tm=256):
    s = x.shape[0]
    n_pre = 0 if moe is None else 2
    imap = (lambda m: (m, 0)) if moe is None else (lambda m, p1, p2: (m, 0))
    vmap = (lambda m: (0, 0)) if moe is None else (lambda m, p1, p2: (0, 0))
    row = pl.BlockSpec((tm, D_MODEL), imap)
    vec = pl.BlockSpec((1, D_MODEL), vmap)
    scratch = []
    if moe is None:
        args = [x, y]
        in_specs = [row, row]
    else:
        ys, pos1, pos2, route = moe
        args = [pos1, pos2, x, route, ys]
        in_specs = [row, pl.BlockSpec((tm, LANES), imap), pl.BlockSpec(memory_space=pl.ANY)]
        scratch = [pltpu.VMEM((tm, D_MODEL), F32), pltpu.VMEM((tm, D_MODEL), F32),
                   pltpu.SemaphoreType.DMA(())]
    args += [gate, g.reshape(1, D_MODEL), b.reshape(1, D_MODEL)]
    in_specs += [vec, vec, vec]
    out_shape = [jax.ShapeDtypeStruct((s, D_MODEL), F32)]
    out_specs = [row]
    if nxt is not None:
        args += [nxt[0], nxt[1]]
        in_specs += [vec, vec]
        out_shape.append(jax.ShapeDtypeStruct((s, D_MODEL), BF16 if router is None else F32))
        out_specs.append(row)
    if router is not None:
        rw, rb = router
        rw_pad = jnp.pad(rw, ((0, 0), (0, LANES - N_EXPERTS)))
        rb_pad = jnp.pad(rb, (0, LANES - N_EXPERTS)).reshape(1, LANES)
        args += [rw_pad, rb_pad]
        in_specs += [pl.BlockSpec((D_MODEL, LANES), vmap), pl.BlockSpec((1, LANES), vmap)]
        out_shape.append(jax.ShapeDtypeStruct((s, LANES), F32))
        out_specs.append(pl.BlockSpec((tm, LANES), imap))
    return pl.pallas_call(
        functools.partial(_ln_kernel, has_next=nxt is not None, has_router=router is not None,
                          moe_combine=moe is not None, tm=tm),
        grid_spec=pltpu.PrefetchScalarGridSpec(
            num_scalar_prefetch=n_pre, grid=(s // tm,), in_specs=in_specs, out_specs=out_specs,
            scratch_shapes=scratch),
        out_shape=out_shape,
        compiler_params=_cparams(("arbitrary",)),
        name="deepnorm_ln",
    )(*args)


def _matmul_kernel(a_ref, w_ref, o_ref):
    o_ref[...] = jnp.dot(a_ref[...], w_ref[...], preferred_element_type=F32).astype(o_ref.dtype)


def _matmul(a, w, tm, tn, out_dtype=BF16, name="matmul"):
    m, k = a.shape
    n = w.shape[1]
    return pl.pallas_call(
        _matmul_kernel,
        grid=(n // tn, m // tm),
        in_specs=[pl.BlockSpec((tm, k), lambda j, i: (i, 0)),
                  pl.BlockSpec((k, tn), lambda j, i: (0, j))],
        out_specs=pl.BlockSpec((tm, tn), lambda j, i: (i, j)),
        out_shape=jax.ShapeDtypeStruct((m, n), out_dtype),
        compiler_params=_cparams(("arbitrary", "arbitrary")),
        name=name,
    )(a, w)


def _rel_bucket(dist):
    n = jnp.maximum(dist, 0)
    max_exact = REL_BUCKETS // 2
    nf = jnp.maximum(n, 1).astype(F32)
    large = max_exact + (jnp.log(nf / max_exact) / math.log(REL_MAX_DIST / max_exact)
                         * (REL_BUCKETS - max_exact)).astype(jnp.int32)
    large = jnp.minimum(large, REL_BUCKETS - 1)
    return jnp.where(n < max_exact, n, large)


def _moba_kernel(relb_ref, q_ref, k_ref, vt_ref, o_ref,
                 kmean_ref, bown_ref, bprev_ref, sel_ref, m_ref, acc_ref, *, nb):
    g = pl.program_id(0)
    j = pl.program_id(1)
    blk = MOBA_BLOCK
    dh = ATT_HEAD_DIM
    scale = dh ** -0.5
    ln2 = math.log(2.0)
    nt = (((1,), (1,)), ((), ()))
    heads = range(MOBA_HEAD_GROUP)
    key_i = lax.broadcasted_iota(jnp.int32, (blk, blk), 0)
    qry_i = lax.broadcasted_iota(jnp.int32, (blk, blk), 1)

    @pl.when(j == 0)
    def _():
        for hh in heads:
            head = g * MOBA_HEAD_GROUP + hh
            kf = k_ref[:, hh * dh:(hh + 1) * dh].astype(F32).reshape(nb, blk, dh)
            kmean_ref[hh] = jnp.mean(kf, axis=1)
            for ref, off in ((bown_ref, 0), (bprev_ref, blk)):
                bucket = _rel_bucket(qry_i - key_i + off)
                bias = jnp.zeros((blk, blk), F32)
                for b in range(REL_BUCKETS):
                    bias = jnp.where(bucket == b, relb_ref[head, b], bias)
                ref[hh] = bias

    row0 = pl.multiple_of(j * blk, blk)
    blk_i = lax.broadcasted_iota(jnp.int32, (nb, blk), 0)
    q2 = []
    for hh in heads:
        q = q_ref[:, hh * dh:(hh + 1) * dh]
        q2.append((q.astype(F32) * (scale / ln2)).astype(BF16))
        score = lax.dot_general(kmean_ref[hh], q.astype(F32), nt,
                                preferred_element_type=F32, precision=HIGHEST)
        sc = jnp.where(blk_i < j, score, NEG_INF)
        seladd = jnp.full((nb, blk), NEG_INF, F32)
        for _ in range(MOBA_TOPK):
            mx = jnp.max(sc, axis=0, keepdims=True)
            cand = jnp.where(sc == mx, blk_i, nb)
            cand = jnp.where(mx > NEG_INF, cand, nb)
            idx = jnp.min(cand, axis=0, keepdims=True)
            pick = blk_i == idx
            seladd = jnp.where(pick, 0.0, seladd)
            sc = jnp.where(pick, NEG_INF, sc)
        sel_ref[hh] = seladd

    def scores(kb_row0, hh):
        return lax.dot_general(k_ref[pl.ds(kb_row0, blk), hh * dh:(hh + 1) * dh], q2[hh], nt,
                               preferred_element_type=F32)

    def weighted_values(kb, hh, p):
        return jnp.dot(vt_ref[kb, hh], p.astype(BF16), preferred_element_type=F32)

    for hh in heads:
        s = scores(row0, hh) * ln2 + bown_ref[hh]
        s = jnp.where(key_i <= qry_i, s, NEG_INF)
        m0 = jnp.max(s, axis=0, keepdims=True)
        m_ref[hh] = m0
        acc_ref[hh] = weighted_values(j, hh, jnp.exp(s - m0))

    def merge(hh, kb, m_blk, acc_blk):
        m_blk = m_blk + sel_ref[hh, pl.ds(kb, 1), :]
        m_old = m_ref[hh]
        m_new = jnp.maximum(m_old, m_blk)
        m_ref[hh] = m_new
        acc_ref[hh] = jnp.exp(m_old - m_new) * acc_ref[hh] + jnp.exp(m_blk - m_new) * acc_blk

    def far_block(kb, carry):
        r0 = pl.multiple_of(kb * blk, blk)
        s2 = [scores(r0, hh) for hh in heads]
        m2 = [jnp.max(s2[hh], axis=0, keepdims=True) for hh in heads]
        acc_blk = [weighted_values(kb, hh, jnp.exp2(s2[hh] - m2[hh])) for hh in heads]
        for hh in heads:
            far_bias = relb_ref[g * MOBA_HEAD_GROUP + hh, REL_BUCKETS - 1]
            merge(hh, kb, m2[hh] * ln2 + far_bias, acc_blk[hh])
        return carry

    lax.fori_loop(0, j - 1, far_block, 0)
    kb = jnp.maximum(j - 1, 0)
    r0 = pl.multiple_of(kb * blk, blk)
    for hh in heads:
        s = scores(r0, hh) * ln2 + bprev_ref[hh]
        m_blk = jnp.max(s, axis=0, keepdims=True)
        merge(hh, kb, m_blk, weighted_values(kb, hh, jnp.exp(s - m_blk)))
        acc = acc_ref[hh]
        o_ref[:, hh * dh:(hh + 1) * dh] = (acc[0:dh] / acc[dh:dh + 1]).T.astype(o_ref.dtype)


def _moba(proj, rel_bias):
    s = proj.shape[0]
    nb = s // MOBA_BLOCK
    hg = MOBA_HEAD_GROUP
    gw = hg * ATT_HEAD_DIM
    n_groups = ATT_HEADS // hg
    dhp = ATT_HEAD_DIM + MOBA_ONES_ROWS
    v_t = proj[:, 2 * BRANCH_WIDTH:3 * BRANCH_WIDTH].reshape(nb, MOBA_BLOCK, ATT_HEADS, ATT_HEAD_DIM)
    v_t = jnp.concatenate([v_t.transpose(0, 2, 3, 1),
                           jnp.ones((nb, ATT_HEADS, MOBA_ONES_ROWS, MOBA_BLOCK), BF16)], axis=2)
    return pl.pallas_call(
        functools.partial(_moba_kernel, nb=nb),
        grid=(n_groups, nb),
        in_specs=[pl.BlockSpec(memory_space=pltpu.SMEM),
                  pl.BlockSpec((MOBA_BLOCK, gw), lambda g, j: (j, g)),
                  pl.BlockSpec((s, gw), lambda g, j: (0, n_groups + g)),
                  pl.BlockSpec((nb, hg, dhp, MOBA_BLOCK), lambda g, j: (0, g, 0, 0))],
        out_specs=pl.BlockSpec((MOBA_BLOCK, gw), lambda g, j: (j, g)),
        out_shape=jax.ShapeDtypeStruct((s, BRANCH_WIDTH), BF16),
        scratch_shapes=[pltpu.VMEM((hg, nb, ATT_HEAD_DIM), F32),
                        pltpu.VMEM((hg, MOBA_BLOCK, MOBA_BLOCK), F32),
                        pltpu.VMEM((hg, MOBA_BLOCK, MOBA_BLOCK), F32),
                        pltpu.VMEM((hg, nb, MOBA_BLOCK), F32),
                        pltpu.VMEM((hg, 1, MOBA_BLOCK), F32),
                        pltpu.VMEM((hg, dhp, MOBA_BLOCK), F32)],
        compiler_params=_cparams(("arbitrary", "arbitrary")),
        name="moba",
    )(rel_bias.T, proj, proj, v_t)


def _s5_tables(lam_re, lam_im, log_dt, b_re, b_im, c_re, c_im):
    t_len = S5_CHUNK
    g_cnt, p_cnt, h_cnt = S5_GROUPS, S5_STATE, S5_GROUP
    tg = S5_TILE_GROUPS
    nt = g_cnt // tg
    dt = jnp.exp(log_dt)[:, None]
    ar, ai = lam_re * dt, lam_im * dt

    def lam_pow(steps):
        st = steps.astype(F32)[:, None, None]
        mag = jnp.exp(st * ar)
        return mag * jnp.cos(st * ai), mag * jnp.sin(st * ai)

    pr, pi = lam_pow(jnp.arange(t_len + 1))
    qr, qi = lam_pow((t_len - 1) - jnp.arange(t_len))
    nr, ni = pr[1] - 1.0, pi[1]
    den = lam_re * lam_re + lam_im * lam_im
    rr, ri = (nr * lam_re + ni * lam_im) / den, (ni * lam_re - nr * lam_im) / den
    bbr = rr[..., None] * b_re - ri[..., None] * b_im
    bbi = rr[..., None] * b_im + ri[..., None] * b_re

    def c_times(xr, xi):
        return (c_re[None] * xr[:, :, None, :] - c_im[None] * xi[:, :, None, :],
                c_re[None] * xi[:, :, None, :] + c_im[None] * xr[:, :, None, :])

    cpr, cpi = c_times(pr, pi)
    cqr, cqi = c_times(qr, qi)
    kc = (jnp.einsum('tghp,gpk->tgkh', cqr, bbr, precision=HIGHEST)
          - jnp.einsum('tghp,gpk->tgkh', cqi, bbi, precision=HIGHEST))
    kd = kc.reshape(t_len, nt, tg, h_cnt, h_cnt).transpose(1, 0, 3, 2, 4).reshape(nt, t_len, h_cnt, LANES)
    qr4, qi4 = qr[:, :, :, None], qi[:, :, :, None]
    bzr = qr4 * bbr[None] - qi4 * bbi[None]
    bzi = qr4 * bbi[None] + qi4 * bbr[None]
    bz = jnp.stack([bzr, bzi], axis=0).reshape(2, t_len, nt, tg, p_cnt, h_cnt)
    bz = bz.transpose(2, 1, 5, 0, 3, 4).reshape(nt, t_len, h_cnt, 2 * tg * p_cnt)
    cm = jnp.stack([cpr[1:], -cpi[1:]], axis=0).reshape(2, t_len, nt, tg, h_cnt, p_cnt)
    cm = cm.transpose(2, 1, 0, 5, 3, 4).reshape(nt, t_len, 2 * p_cnt, LANES)
    a_re = pr[t_len].reshape(nt, 1, tg * p_cnt)
    a_im = pi[t_len].reshape(nt, 1, tg * p_cnt)
    return kd.astype(BF16), bz.astype(BF16), cm.astype(BF16), a_re, a_im


def _s5_kernel(u_ref, kd_ref, bz_ref, cm_ref, are_ref, aim_ref, d_ref, y_ref,
               uf_ref, ucat_ref, z_ref, hc_ref, kdf_ref, bzf_ref, cmf_ref, *, nc):
    t_len = S5_CHUNK
    ns = S5_TILE_STATE
    tg, hs, ps = S5_TILE_GROUPS, S5_GROUP, S5_STATE
    chan_grp = lax.broadcasted_iota(jnp.int32, (LANES, LANES), 0) // hs
    lane_grp = lax.broadcasted_iota(jnp.int32, (LANES, LANES), 1) // hs
    state_grp = (lax.broadcasted_iota(jnp.int32, (LANES, 2 * ns), 1) % ns) // ps
    chan_grp_w = lax.broadcasted_iota(jnp.int32, (LANES, 2 * ns), 0) // hs
    lane_grp_p = lax.broadcasted_iota(jnp.int32, (ps, LANES), 1) // hs
    zero = jnp.zeros((), BF16)
    for s in range(t_len):
        rows = slice(s * LANES, (s + 1) * LANES)
        kdf_ref[rows, :] = jnp.where(chan_grp == lane_grp, jnp.concatenate([kd_ref[0, s]] * tg, axis=0), zero)
        bzf_ref[rows, :] = jnp.where(chan_grp_w == state_grp, jnp.concatenate([bz_ref[0, s]] * tg, axis=0), zero)
        for x in range(2):
            piece = cm_ref[0, s, x * ps:(x + 1) * ps, :]
            for g in range(tg):
                cmf_ref[s, x * ns + g * ps:x * ns + (g + 1) * ps, :] = jnp.where(lane_grp_p == g, piece, zero)
    uf_ref[...] = u_ref[...].astype(F32)
    for s in range(t_len):
        ucat_ref[:, s * LANES:(s + 1) * LANES] = uf_ref[pl.ds(s, nc, stride=t_len), :].astype(BF16)
    z_ref[...] = jnp.dot(ucat_ref[...], bzf_ref[...], preferred_element_type=F32)
    a_re = are_ref[0]
    a_im = aim_ref[0]

    def step(c, carry):
        h_re, h_im = carry
        hc_ref[pl.ds(c, 1), 0:ns] = h_re
        hc_ref[pl.ds(c, 1), ns:2 * ns] = h_im
        z_re = z_ref[pl.ds(c, 1), 0:ns]
        z_im = z_ref[pl.ds(c, 1), ns:2 * ns]
        return (a_re * h_re - a_im * h_im + z_re, a_re * h_im + a_im * h_re + z_im)

    zero_row = jnp.zeros((1, ns), F32)
    lax.fori_loop(0, nc, step, (zero_row, zero_row))
    hc = hc_ref[...].astype(BF16)
    d_skip = d_ref[...]
    for t in range(t_len):
        acc = jnp.dot(hc, cmf_ref[t], preferred_element_type=F32)
        acc += jnp.dot(ucat_ref[:, 0:(t + 1) * LANES],
                       kdf_ref[(t_len - 1 - t) * LANES:t_len * LANES, :],
                       preferred_element_type=F32)
        acc += d_skip * uf_ref[pl.ds(t, nc, stride=t_len), :]
        y_ref[pl.ds(t, nc, stride=t_len), :] = jax.nn.gelu(acc)


def _s5_glu_kernel(y_ref, yn_ref, w_ref, b_ref, o_ref):
    z = jnp.dot(y_ref[...].astype(BF16), w_ref[...], preferred_element_type=F32) + b_ref[...]
    o_ref[...] = (yn_ref[...] * jax.nn.sigmoid(z)).astype(o_ref.dtype)


def _s5(proj, lam_re, lam_im, log_dt, b_re, b_im, c_re, c_im, d_skip, w_glu, b_glu):
    s = proj.shape[0]
    nc = s // S5_CHUNK
    nt = S5_GROUPS // S5_TILE_GROUPS
    kd, bz, cm, a_re, a_im = _s5_tables(lam_re, lam_im, log_dt, b_re, b_im, c_re, c_im)
    u_col = 3 * BRANCH_WIDTH // LANES
    ns2 = 2 * S5_TILE_STATE
    y = pl.pallas_call(
        functools.partial(_s5_kernel, nc=nc),
        grid=(nt,),
        in_specs=[pl.BlockSpec((s, LANES), lambda c: (0, u_col + c)),
                  pl.BlockSpec((1, S5_CHUNK, S5_GROUP, LANES), lambda c: (c, 0, 0, 0)),
                  pl.BlockSpec((1, S5_CHUNK, S5_GROUP, ns2), lambda c: (c, 0, 0, 0)),
                  pl.BlockSpec((1, S5_CHUNK, 2 * S5_STATE, LANES), lambda c: (c, 0, 0, 0)),
                  pl.BlockSpec((1, 1, S5_TILE_STATE), lambda c: (c, 0, 0)),
                  pl.BlockSpec((1, 1, S5_TILE_STATE), lambda c: (c, 0, 0)),
                  pl.BlockSpec((1, LANES), lambda c: (0, c))],
        out_specs=pl.BlockSpec((s, LANES), lambda c: (0, c)),
        out_shape=jax.ShapeDtypeStruct((s, BRANCH_WIDTH), F32),
        scratch_shapes=[pltpu.VMEM((s, LANES), F32),
                        pltpu.VMEM((nc, S5_CHUNK * LANES), BF16),
                        pltpu.VMEM((nc, ns2), F32),
                        pltpu.VMEM((nc, ns2), F32),
                        pltpu.VMEM((S5_CHUNK * LANES, LANES), BF16),
                        pltpu.VMEM((S5_CHUNK * LANES, ns2), BF16),
                        pltpu.VMEM((S5_CHUNK, ns2, LANES), BF16)],
        compiler_params=_cparams(("arbitrary",)),
        name="s5_scan",
    )(proj, kd, bz, cm, a_re, a_im, d_skip.reshape(1, BRANCH_WIDTH))
    tm, tn = 512, 512
    return pl.pallas_call(
        _s5_glu_kernel,
        grid=(BRANCH_WIDTH // tn, s // tm),
        in_specs=[pl.BlockSpec((tm, BRANCH_WIDTH), lambda j, i: (i, 0)),
                  pl.BlockSpec((tm, tn), lambda j, i: (i, j)),
                  pl.BlockSpec((BRANCH_WIDTH, tn), lambda j, i: (0, j)),
                  pl.BlockSpec((1, tn), lambda j, i: (0, j))],
        out_specs=pl.BlockSpec((tm, tn), lambda j, i: (i, j)),
        out_shape=jax.ShapeDtypeStruct((s, BRANCH_WIDTH), BF16),
        compiler_params=_cparams(("arbitrary", "arbitrary")),
        name="s5_glu",
    )(y, y, w_glu.astype(BF16), b_glu.reshape(1, BRANCH_WIDTH))


def _gla_kernel(q_ref, k_ref, v_ref, r_ref, tail_ref, wg_ref, bg_ref, ng_ref, o_ref, st_ref):
    dk = GLA_KEY // GLA_HEADS
    dv = GLA_VAL // GLA_HEADS
    cs = GLA_CHUNK
    sub = GLA_SUB
    nt = (((1,), (1,)), ((), ()))
    tn = (((0,), (0,)), ((), ()))

    @pl.when(pl.program_id(0) == 0)
    def _():
        st_ref[...] = jnp.zeros_like(st_ref)

    gate_in = jnp.dot(tail_ref[...], wg_ref[...], preferred_element_type=F32) + bg_ref[...]
    log_a = jax.nn.log_sigmoid(gate_in) / GLA_GATE_TAU
    ri = lax.broadcasted_iota(jnp.int32, (cs, cs), 0)
    ci = lax.broadcasted_iota(jnp.int32, (cs, cs), 1)
    tril = (ri >= ci).astype(F32)
    bcum_all = jnp.dot(tril, log_a, preferred_element_type=F32, precision=HIGHEST)
    sub_row = lax.broadcasted_iota(jnp.int32, (sub, 1), 0)

    for h in range(GLA_HEADS):
        q = q_ref[:, h * dk:(h + 1) * dk].astype(F32) * dk ** -0.5
        k = k_ref[:, h * dk:(h + 1) * dk].astype(F32)
        v_bf = v_ref[:, h * dv:(h + 1) * dv]
        v = v_bf.astype(F32)
        bc = bcum_all[:, h * dk:(h + 1) * dk]
        state = st_ref[h]
        o_inter = lax.dot_general((q * jnp.exp(bc)).astype(BF16), state.astype(BF16), nt,
                                  preferred_element_type=F32)
        parts = []
        for i in range(cs // sub):
            lo = i * sub
            b_i, q_i, k_i, v_i = bc[lo:lo + sub], q[lo:lo + sub], k[lo:lo + sub], v[lo:lo + sub]
            o_i = o_inter[lo:lo + sub]
            if i > 0:
                ref = bc[lo - 1:lo]
                q_h = (q_i * jnp.exp(b_i - ref)).astype(BF16)
                k_h = (k[:lo] * jnp.exp(ref - bc[:lo])).astype(BF16)
                attn = lax.dot_general(q_h, k_h, nt, preferred_element_type=F32)
                o_i = o_i + jnp.dot(attn.astype(BF16), v_bf[:lo], preferred_element_type=F32)
            for dlt in range(sub):
                if dlt == 0:
                    a = jnp.sum(q_i * k_i, axis=-1, keepdims=True)
                    o_i = o_i + a * v_i
                else:
                    b_s = pltpu.roll(b_i, dlt, 0)
                    k_s = pltpu.roll(k_i, dlt, 0)
                    v_s = pltpu.roll(v_i, dlt, 0)
                    e = jnp.exp(jnp.minimum(b_i - b_s, 0.0))
                    a = jnp.sum(q_i * k_s * e, axis=-1, keepdims=True)
                    a = jnp.where(sub_row >= dlt, a, 0.0)
                    o_i = o_i + a * v_s
            parts.append(o_i)
        o = jnp.concatenate(parts, axis=0)
        b_last = bc[cs - 1:cs]
        k_dec = (k * jnp.exp(b_last - bc)).astype(BF16)
        st_ref[h] = state * jnp.exp(b_last) + lax.dot_general(v_bf, k_dec, tn,
                                                             preferred_element_type=F32)
        o = o * lax.rsqrt(jnp.mean(o * o, axis=-1, keepdims=True) + NORM_EPS)
        o = o * ng_ref[:, h * dv:(h + 1) * dv]
        r = r_ref[:, h * dv:(h + 1) * dv].astype(F32)
        o_ref[:, h * dv:(h + 1) * dv] = (o * (r * jax.nn.sigmoid(r))).astype(o_ref.dtype)


def _gla(proj, w_gate, b_gate, norm_g):
    s = proj.shape[0]
    cs = GLA_CHUNK
    wg_pad = jnp.pad(w_gate, ((0, TAIL_W - GLA_GATE_RANK), (0, 0))).astype(BF16)
    return pl.pallas_call(
        _gla_kernel,
        grid=(s // cs,),
        in_specs=[pl.BlockSpec((cs, GLA_KEY), lambda c: (c, 4 * BRANCH_WIDTH // GLA_KEY)),
                  pl.BlockSpec((cs, GLA_KEY), lambda c: (c, 4 * BRANCH_WIDTH // GLA_KEY + 1)),
                  pl.BlockSpec((cs, GLA_VAL), lambda c: (c, 5)),
                  pl.BlockSpec((cs, GLA_VAL), lambda c: (c, 6)),
                  pl.BlockSpec((cs, TAIL_W), lambda c: (c, TAIL_COL // TAIL_W)),
                  pl.BlockSpec((TAIL_W, GLA_KEY), lambda c: (0, 0)),
                  pl.BlockSpec((1, GLA_KEY), lambda c: (0, 0)),
                  pl.BlockSpec((1, GLA_VAL), lambda c: (0, 0))],
        out_specs=pl.BlockSpec((cs, GLA_VAL), lambda c: (c, 0)),
        out_shape=jax.ShapeDtypeStruct((s, GLA_VAL), BF16),
        scratch_shapes=[pltpu.VMEM((GLA_HEADS, GLA_VAL // GLA_HEADS, GLA_KEY // GLA_HEADS), F32)],
        compiler_params=_cparams(("arbitrary",)),
        name="gla",
    )(proj, proj, proj, proj, proj, wg_pad, b_gate.reshape(1, GLA_KEY), norm_g.reshape(1, GLA_VAL))


def _merge_kernel(ya_ref, ys_ref, yg_ref, tail_ref, wb_ref, wg_ref, bg_ref, o_ref):
    tail = tail_ref[...]
    acc = None
    for n, y_ref in enumerate((ya_ref, ys_ref, yg_ref)):
        up = jnp.dot(y_ref[...], wb_ref[n], preferred_element_type=F32)
        gate = jax.nn.sigmoid(jnp.dot(tail, wg_ref[n], preferred_element_type=F32) + bg_ref[n])
        acc = gate * up if acc is None else acc + gate * up
    o_ref[...] = acc.astype(o_ref.dtype)


def _merge(y_att, y_s5, y_gla, proj, w_branch, w_merge_gate, b_merge_gate, tm=512, tn=512):
    s = proj.shape[0]
    wb = w_branch.astype(BF16)
    wg = w_merge_gate.reshape(MERGE_RANK, N_BRANCH, D_MODEL).transpose(1, 0, 2)
    wg = jnp.pad(wg, ((0, 0), (GLA_GATE_RANK, TAIL_W - GLA_GATE_RANK - MERGE_RANK), (0, 0))).astype(BF16)
    bg = b_merge_gate.reshape(N_BRANCH, 1, D_MODEL)
    ybs = pl.BlockSpec((tm, BRANCH_WIDTH), lambda j, i: (i, 0))
    return pl.pallas_call(
        _merge_kernel,
        grid=(D_MODEL // tn, s // tm),
        in_specs=[ybs, ybs, ybs,
                  pl.BlockSpec((tm, TAIL_W), lambda j, i: (i, TAIL_COL // TAIL_W)),
                  pl.BlockSpec((N_BRANCH, BRANCH_WIDTH, tn), lambda j, i: (0, 0, j)),
                  pl.BlockSpec((N_BRANCH, TAIL_W, tn), lambda j, i: (0, 0, j)),
                  pl.BlockSpec((N_BRANCH, 1, tn), lambda j, i: (0, 0, j))],
        out_specs=pl.BlockSpec((tm, tn), lambda j, i: (i, j)),
        out_shape=jax.ShapeDtypeStruct((s, D_MODEL), BF16),
        compiler_params=_cparams(("arbitrary", "arbitrary")),
        name="merge",
    )(y_att, y_s5, y_gla, proj, wb, wg, bg)


def _swiglu_kernel(a_ref, w1_ref, w3_ref, o_ref):
    a = a_ref[...]
    g = jnp.dot(a, w1_ref[...], preferred_element_type=F32)
    u = jnp.dot(a, w3_ref[...], preferred_element_type=F32)
    o_ref[...] = (g * jax.nn.sigmoid(g) * u).astype(o_ref.dtype)


def _swiglu_hidden(a, w1, w3, tm=512, tn=512):
    m, k = a.shape
    n = w1.shape[1]
    wspec = pl.BlockSpec((k, tn), lambda j, i: (0, j))
    return pl.pallas_call(
        _swiglu_kernel,
        grid=(n // tn, m // tm),
        in_specs=[pl.BlockSpec((tm, k), lambda j, i: (i, 0)), wspec, wspec],
        out_specs=pl.BlockSpec((tm, tn), lambda j, i: (i, j)),
        out_shape=jax.ShapeDtypeStruct((m, n), BF16),
        compiler_params=_cparams(("arbitrary", "arbitrary")),
        name="swiglu_hidden",
    )(a, w1, w3)


def _moe_plan(route):
    s = route.shape[0]
    tile = MOE_TILE
    n_tiles = (TOP_K * s) // tile + N_EXPERTS
    ids = route[:, ROUTE_ID:ROUTE_ID + TOP_K].astype(jnp.int32)
    onehot = jnp.sum(jax.nn.one_hot(ids, N_EXPERTS, dtype=jnp.int32), axis=1)
    before = jnp.cumsum(onehot, axis=0) - onehot
    counts = jnp.sum(onehot, axis=0)
    padded = (counts + tile - 1) // tile * tile
    ends = jnp.cumsum(padded)
    offsets = ends - padded
    pos = offsets[ids] + jnp.take_along_axis(before, ids, axis=1)
    tile_start = jnp.arange(n_tiles, dtype=jnp.int32) * tile
    tile_expert = jnp.minimum(jnp.sum(tile_start[:, None] >= ends[None, :], axis=1), N_EXPERTS - 1)
    n_active = (ends[-1] // tile).reshape(1)
    pos1 = pos[:, 0].reshape(s // LANES, LANES)
    pos2 = pos[:, 1].reshape(s // LANES, LANES)
    return pos1, pos2, tile_expert.astype(jnp.int32), n_active.astype(jnp.int32), n_tiles


def _moe_dispatch_kernel(pos1_ref, pos2_ref, h_ref, xs_in_ref, xs_ref, sem, *, n_tokens):
    del xs_in_ref
    lag = MOE_DMA_IN_FLIGHT

    def copies(t):
        return (_row_copy(h_ref, xs_ref, sem, t, _smem_at(pos1_ref, t)),
                _row_copy(h_ref, xs_ref, sem, t, _smem_at(pos2_ref, t)))

    def step(t, carry):
        for cp in copies(t):
            cp.start()

        @pl.when(t >= lag)
        def _():
            for cp in copies(t - lag):
                cp.wait()
        return carry

    def drain(t, carry):
        for cp in copies(t):
            cp.wait()
        return carry

    lax.fori_loop(0, n_tokens, step, 0)
    lax.fori_loop(n_tokens - lag, n_tokens, drain, 0)


def _moe_dispatch(h, pos1, pos2, n_tiles):
    s = h.shape[0]
    rows = n_tiles * MOE_TILE
    smem = pl.BlockSpec(memory_space=pltpu.SMEM)
    hbm = pl.BlockSpec(memory_space=pl.ANY)
    return pl.pallas_call(
        functools.partial(_moe_dispatch_kernel, n_tokens=s),
        in_specs=[smem, smem, hbm, hbm],
        out_specs=hbm,
        out_shape=jax.ShapeDtypeStruct((rows, D_MODEL), F32),
        scratch_shapes=[pltpu.SemaphoreType.DMA(())],
        input_output_aliases={3: 0},
        compiler_params=pltpu.CompilerParams(vmem_limit_bytes=VMEM_LIMIT),
        name="moe_dispatch",
    )(pos1, pos2, h, jnp.zeros((rows, D_MODEL), F32))


def _moe_hidden_kernel(te_ref, na_ref, a_ref, w1_ref, w3_ref, o_ref):
    del te_ref

    @pl.when(pl.program_id(1) < na_ref[0])
    def _():
        a = a_ref[...].astype(BF16)
        g = jnp.dot(a, w1_ref[0], preferred_element_type=F32)
        u = jnp.dot(a, w3_ref[0], preferred_element_type=F32)
        o_ref[...] = (g * jax.nn.sigmoid(g) * u).astype(o_ref.dtype)

    @pl.when(pl.program_id(1) >= na_ref[0])
    def _():
        o_ref[...] = jnp.zeros_like(o_ref)


def _moe_down_kernel(te_ref, na_ref, a_ref, w_ref, o_ref):
    del te_ref

    @pl.when(pl.program_id(1) < na_ref[0])
    def _():
        o_ref[...] = jnp.dot(a_ref[...], w_ref[0], preferred_element_type=F32).astype(o_ref.dtype)

    @pl.when(pl.program_id(1) >= na_ref[0])
    def _():
        o_ref[...] = jnp.zeros_like(o_ref)


def _moe_experts(xs, tile_expert, n_active, w1, w3, w2, n_tiles, tn_hidden=896, tn_down=1024):
    tile = MOE_TILE
    rows = n_tiles * tile
    hid = pl.pallas_call(
        _moe_hidden_kernel,
        grid_spec=pltpu.PrefetchScalarGridSpec(
            num_scalar_prefetch=2, grid=(D_FF_EXPERT // tn_hidden, n_tiles),
            in_specs=[pl.BlockSpec((tile, D_MODEL), lambda n, i, te, na: (i, 0)),
                      pl.BlockSpec((1, D_MODEL, tn_hidden), lambda n, i, te, na: (te[i], 0, n)),
                      pl.BlockSpec((1, D_MODEL, tn_hidden), lambda n, i, te, na: (te[i], 0, n))],
            out_specs=pl.BlockSpec((tile, tn_hidden), lambda n, i, te, na: (i, n))),
        out_shape=jax.ShapeDtypeStruct((rows, D_FF_EXPERT), BF16),
        compiler_params=_cparams(("arbitrary", "arbitrary")),
        name="moe_hidden",
    )(tile_expert, n_active, xs, w1, w3)
    return pl.pallas_call(
        _moe_down_kernel,
        grid_spec=pltpu.PrefetchScalarGridSpec(
            num_scalar_prefetch=2, grid=(D_MODEL // tn_down, n_tiles),
            in_specs=[pl.BlockSpec((tile, D_FF_EXPERT), lambda n, i, te, na: (i, 0)),
                      pl.BlockSpec((1, D_FF_EXPERT, tn_down), lambda n, i, te, na: (te[i], 0, n))],
            out_specs=pl.BlockSpec((tile, tn_down), lambda n, i, te, na: (i, n))),
        out_shape=jax.ShapeDtypeStruct((rows, D_MODEL), F32),
        compiler_params=_cparams(("arbitrary", "arbitrary")),
        name="moe_down",
    )(tile_expert, n_active, hid, w2)


def kernel(x, c, w_cond, b_cond, rel_bias, w_mod, b_mod, w_in, s5_lambda_re, s5_lambda_im, s5_log_dt, s5_b_re, s5_b_im, s5_c_re, s5_c_im, s5_d, s5_w_glu, s5_b_glu, gla_w_gate, gla_b_gate, gla_norm_g, w_branch, w_merge_gate, b_merge_gate, w_out, ln1_g, ln1_b, ffn_w1, ffn_w3, ffn_w2, router_w, router_b, exp_w1, exp_w3, exp_w2, ln2_g, ln2_b):
    bsz, seq, _ = x.shape
    assert bsz == 1
    mod = _conditioning(c, w_cond, b_cond, w_mod, b_mod)
    xs = x.reshape(seq, D_MODEL)
    hm = _modulate(xs, mod[0, 1], mod[0, 0])
    for l in range(DEPTH):
        shift_f, scale_f, gate_m, gate_f = mod[l, 3], mod[l, 4], mod[l, 2], mod[l, 5]
        w_in_b = jnp.pad(w_in[l].astype(BF16), ((0, 0), (0, D_IN_PAD - D_IN)))
        proj = _matmul(hm, w_in_b, tm=512, tn=512, name="in_proj")
        y_att = _moba(proj, rel_bias)
        y_s5 = _s5(proj, s5_lambda_re[l], s5_lambda_im[l], s5_log_dt[l], s5_b_re[l], s5_b_im[l],
                   s5_c_re[l], s5_c_im[l], s5_d[l], s5_w_glu[l], s5_b_glu[l])
        y_gla = _gla(proj, gla_w_gate[l], gla_b_gate[l], gla_norm_g[l])
        merged = _merge(y_att, y_s5, y_gla, proj, w_branch[l], w_merge_gate[l], b_merge_gate[l])
        y = _matmul(merged, w_out[l].astype(BF16), tm=512, tn=1024, name="out_proj")
        dense = l % 2 == 0
        router = None if dense else (router_w[l // 2], router_b[l // 2])
        outs = _deepnorm_ln(xs, y, gate_m, ln1_g[l], ln1_b[l], nxt=(scale_f, shift_f), router=router)
        xs, hf = outs[0], outs[1]
        if dense:
            hid = _swiglu_hidden(hf, ffn_w1[l // 2].astype(BF16), ffn_w3[l // 2].astype(BF16))
            f = _matmul(hid, ffn_w2[l // 2].astype(BF16), tm=256, tn=1024, name="ffn_down")
            moe = None
        else:
            route = outs[2]
            pos1, pos2, tile_expert, n_active, n_tiles = _moe_plan(route)
            xsorted = _moe_dispatch(hf, pos1, pos2, n_tiles)
            ys = _moe_experts(xsorted, tile_expert, n_active, exp_w1[l // 2].astype(BF16),
                              exp_w3[l // 2].astype(BF16), exp_w2[l // 2].astype(BF16), n_tiles)
            f = None
            moe = (ys, pos1, pos2, route)
        nxt = (mod[l + 1, 1], mod[l + 1, 0]) if l + 1 < DEPTH else None
        outs = _deepnorm_ln(xs, f, gate_f, ln2_g[l], ln2_b[l], nxt=nxt, moe=moe)
        xs = outs[0]
        if nxt is not None:
            hm = outs[1]
    return xs.reshape(bsz, seq, D_MODEL)
```

```python
import functools
import math

import jax
import jax.numpy as jnp
from jax import lax
from jax.experimental import pallas as pl
from jax.experimental.pallas import tpu as pltpu

F32 = jnp.float32
BF16 = jnp.bfloat16
HIGHEST = lax.Precision.HIGHEST

D_MODEL = 4096
DEPTH = 4
BRANCH_WIDTH = 1024
N_BRANCH = 3
ATT_HEADS = 8
ATT_HEAD_DIM = 128
MOBA_BLOCK = 256
MOBA_TOPK = 3
MOBA_HEAD_GROUP = 4
MOBA_ONES_ROWS = 16
REL_BUCKETS = 32
REL_MAX_DIST = 128
S5_GROUP = 16
S5_GROUPS = 64
S5_STATE = 64
S5_CHUNK = 16
S5_TILE_GROUPS = 8
S5_TILE_STATE = S5_TILE_GROUPS * S5_STATE
GLA_HEADS = 4
GLA_KEY = 512
GLA_VAL = 1024
GLA_GATE_RANK = 16
GLA_GATE_TAU = 16.0
GLA_CHUNK = 64
GLA_SUB = 16
MERGE_RANK = 256
COND_RANK = 512
D_FF = 8192
N_EXPERTS = 8
TOP_K = 2
ROUTE_ID = 8
ROUTE_W = 10
MOE_TILE = 256
D_FF_EXPERT = 1792
DN_ALPHA = (2 * DEPTH) ** 0.25
LN_EPS = 1e-5
NORM_EPS = 1e-6

D_IN = 7440
D_IN_PAD = 7680
TAIL_COL = 7168
TAIL_W = 512
LANES = 128
VMEM_LIMIT = 56 * 1024 * 1024
NEG_INF = float("-inf")


def _cparams(sem):
    return pltpu.CompilerParams(dimension_semantics=sem, vmem_limit_bytes=VMEM_LIMIT)


def _cond_kernel(c_ref, w_ref, b_ref, o_ref):
    z = jnp.dot(c_ref[...], w_ref[...], preferred_element_type=F32, precision=HIGHEST) + b_ref[...]
    o_ref[...] = z * jax.nn.sigmoid(z)


def _mod_kernel(cond_ref, w_ref, b_ref, o_ref):
    o_ref[0] = jnp.dot(cond_ref[...], w_ref[0], preferred_element_type=F32,
                       precision=HIGHEST) + b_ref[0]


def _conditioning(c, w_cond, b_cond, w_mod, b_mod):
    c8 = jnp.broadcast_to(c, (8, D_MODEL))
    cond = pl.pallas_call(
        _cond_kernel,
        out_shape=jax.ShapeDtypeStruct((8, COND_RANK), F32),
        compiler_params=_cparams(None),
        name="cond",
    )(c8, w_cond, b_cond.reshape(1, COND_RANK))
    n_mod = 6 * D_MODEL
    tn = 3072
    mod = pl.pallas_call(
        _mod_kernel,
        grid=(DEPTH, n_mod // tn),
        in_specs=[pl.BlockSpec((8, COND_RANK), lambda l, n: (0, 0)),
                  pl.BlockSpec((1, COND_RANK, tn), lambda l, n: (l, 0, n)),
                  pl.BlockSpec((1, 1, tn), lambda l, n: (l, 0, n))],
        out_specs=pl.BlockSpec((1, 8, tn), lambda l, n: (l, 0, n)),
        out_shape=jax.ShapeDtypeStruct((DEPTH, 8, n_mod), F32),
        compiler_params=_cparams(("arbitrary", "arbitrary")),
        name="mod",
    )(cond, w_mod, b_mod.reshape(DEPTH, 1, n_mod))
    return mod[:, 0, :].reshape(DEPTH, 6, 1, D_MODEL)


def _modulate_kernel(x_ref, scale_ref, shift_ref, o_ref):
    o_ref[...] = (x_ref[...] * (1.0 + scale_ref[...]) + shift_ref[...]).astype(o_ref.dtype)


def _modulate(x, scale, shift, tm=512):
    s = x.shape[0]
    vec = pl.BlockSpec((1, D_MODEL), lambda m: (0, 0))
    return pl.pallas_call(
        _modulate_kernel,
        grid=(s // tm,),
        in_specs=[pl.BlockSpec((tm, D_MODEL), lambda m: (m, 0)), vec, vec],
        out_specs=pl.BlockSpec((tm, D_MODEL), lambda m: (m, 0)),
        out_shape=jax.ShapeDtypeStruct((s, D_MODEL), BF16),
        compiler_params=_cparams(("arbitrary",)),
        name="modulate",
    )(x, scale, shift)


def _route_top2(logits):
    lane = lax.broadcasted_iota(jnp.int32, logits.shape, 1)
    m1 = jnp.max(logits, axis=-1, keepdims=True)
    i1 = jnp.min(jnp.where(logits == m1, lane, LANES), axis=-1, keepdims=True)
    rest = jnp.where(lane == i1, NEG_INF, logits)
    m2 = jnp.max(rest, axis=-1, keepdims=True)
    i2 = jnp.min(jnp.where(rest == m2, lane, LANES), axis=-1, keepdims=True)
    e2 = jnp.exp(m2 - m1)
    denom = 1.0 + e2
    rec = jnp.where(lane == ROUTE_ID, i1.astype(F32), 0.0)
    rec = jnp.where(lane == ROUTE_ID + 1, i2.astype(F32), rec)
    rec = jnp.where(lane == ROUTE_W, 1.0 / denom, rec)
    return jnp.where(lane == ROUTE_W + 1, e2 / denom, rec)


def _row_copy(src_ref, dst_ref, sem, src_row, dst_row):
    return pltpu.make_async_copy(src_ref.at[pl.ds(src_row, 1)], dst_ref.at[pl.ds(dst_row, 1)], sem)


def _smem_at(ref, i):
    return ref[i // LANES, i % LANES]


def _gather_rows(copies_of, n_rows):
    def start(r, carry):
        for cp in copies_of(r):
            cp.start()
        return carry

    def wait(r, carry):
        for cp in copies_of(r):
            cp.wait()
        return carry

    lax.fori_loop(0, n_rows, start, 0)
    lax.fori_loop(0, n_rows, wait, 0)


def _ln_kernel(*refs, has_next, has_router, moe_combine, tm):
    pos = 0
    if moe_combine:
        pos1_ref, pos2_ref, x_ref, route_ref, ys_ref = refs[:5]
        pos = 5
    else:
        x_ref, y_ref = refs[:2]
        pos = 2
    gate_ref, g_ref, b_ref = refs[pos:pos + 3]
    pos += 3
    if has_next:
        scale_ref, shift_ref = refs[pos:pos + 2]
        pos += 2
    if has_router:
        rw_ref, rb_ref = refs[pos:pos + 2]
        pos += 2
    xo_ref = refs[pos]
    pos += 1
    if moe_combine:
        buf1_ref, buf2_ref, sem = refs[-3:]
        base = pl.program_id(0) * tm

        _gather_rows(lambda r: (_row_copy(ys_ref, buf1_ref, sem, _smem_at(pos1_ref, base + r), r),
                                _row_copy(ys_ref, buf2_ref, sem, _smem_at(pos2_ref, base + r), r)), tm)
        route = route_ref[...]
        lane = lax.broadcasted_iota(jnp.int32, route.shape, 1)
        w1 = jnp.sum(jnp.where(lane == ROUTE_W, route, 0.0), axis=-1, keepdims=True)
        w2 = jnp.sum(jnp.where(lane == ROUTE_W + 1, route, 0.0), axis=-1, keepdims=True)
        y = w1 * buf1_ref[...] + w2 * buf2_ref[...]
    else:
        y = y_ref[...].astype(F32)
    z = DN_ALPHA * x_ref[...] + (1.0 + gate_ref[...]) * y
    mu = jnp.mean(z, axis=-1, keepdims=True)
    zc = z - mu
    var = jnp.mean(zc * zc, axis=-1, keepdims=True)
    xn = zc * lax.rsqrt(var + LN_EPS) * g_ref[...] + b_ref[...]
    xo_ref[...] = xn
    if has_next:
        ho_ref = refs[pos]
        pos += 1
        h = xn * (1.0 + scale_ref[...]) + shift_ref[...]
        ho_ref[...] = h.astype(ho_ref.dtype)
        if has_router:
            co_ref = refs[pos]
            logits = jnp.dot(h, rw_ref[...], preferred_element_type=F32, precision=HIGHEST)
            lane = lax.broadcasted_iota(jnp.int32, logits.shape, 1)
            logits = jnp.where(lane < N_EXPERTS, logits + rb_ref[...], NEG_INF)
            co_ref[...] = _route_top2(logits)


def _deepnorm_ln(x, y, gate, g, b, nxt=None, router=None, moe=None, tm=256):
    s = x.shape[0]
    n_pre = 0 if moe is None else 2
    imap = (lambda m: (m, 0)) if moe is None else (lambda m, p1, p2: (m, 0))
    vmap = (lambda m: (0, 0)) if moe is None else (lambda m, p1, p2: (0, 0))
    row = pl.BlockSpec((tm, D_MODEL), imap)
    vec = pl.BlockSpec((1, D_MODEL), vmap)
    scratch = []
    if moe is None:
        args = [x, y]
        in_specs = [row, row]
    else:
        ys, pos1, pos2, route = moe
        args = [pos1, pos2, x, route, ys]
        in_specs = [row, pl.BlockSpec((tm, LANES), imap), pl.BlockSpec(memory_space=pl.ANY)]
        scratch = [pltpu.VMEM((tm, D_MODEL), F32), pltpu.VMEM((tm, D_MODEL), F32),
                   pltpu.SemaphoreType.DMA(())]
    args += [gate, g.reshape(1, D_MODEL), b.reshape(1, D_MODEL)]
    in_specs += [vec, vec, vec]
    out_shape = [jax.ShapeDtypeStruct((s, D_MODEL), F32)]
    out_specs = [row]
    if nxt is not None:
        args += [nxt[0], nxt[1]]
        in_specs += [vec, vec]
        out_shape.append(jax.ShapeDtypeStruct((s, D_MODEL), BF16 if router is None else F32))
        out_specs.append(row)
    if router is not None:
        rw, rb = router
        rw_pad = jnp.pad(rw, ((0, 0), (0, LANES - N_EXPERTS)))
        rb_pad = jnp.pad(rb, (0, LANES - N_EXPERTS)).reshape(1, LANES)
        args += [rw_pad, rb_pad]
        in_specs += [pl.BlockSpec((D_MODEL, LANES), vmap), pl.BlockSpec((1, LANES), vmap)]
        out_shape.append(jax.ShapeDtypeStruct((s, LANES), F32))
        out_specs.append(pl.BlockSpec((tm, LANES), imap))
    return pl.pallas_call(
        functools.partial(_ln_kernel, has_next=nxt is not None, has_router=router is not None,
                          moe_combine=moe is not None, tm=tm),
        grid_spec=pltpu.PrefetchScalarGridSpec(
            num_scalar_prefetch=n_pre, grid=(s // tm,), in_specs=in_specs, out_specs=out_specs,
            scratch_shapes=scratch),
        out_shape=out_shape,
        compiler_params=_cparams(("arbitrary",)),
        name="deepnorm_ln",
    )(*args)


def _matmul_kernel(a_ref, w_ref, o_ref, wb_ref, *, n_valid, tn):
    @pl.when(pl.program_id(1) == 0)
    def _():
        wb_ref[...] = w_ref[0].astype(BF16)

    acc = jnp.dot(a_ref[...], wb_ref[...], preferred_element_type=F32)
    if n_valid is not None:
        col = pl.program_id(0) * tn + lax.broadcasted_iota(jnp.int32, acc.shape, 1)
        acc = jnp.where(col < n_valid, acc, 0.0)
    o_ref[...] = acc.astype(o_ref.dtype)


def _matmul(a, w_stack, layer, tm, tn, out_dtype=BF16, n_out=None, name="matmul"):
    m, k = a.shape
    n = w_stack.shape[2]
    n_out = n if n_out is None else n_out
    return pl.pallas_call(
        functools.partial(_matmul_kernel, n_valid=None if n_out == n else n, tn=tn),
        grid=(n_out // tn, m // tm),
        in_specs=[pl.BlockSpec((tm, k), lambda j, i: (i, 0)),
                  pl.BlockSpec((1, k, tn), lambda j, i: (layer, 0, j))],
        out_specs=pl.BlockSpec((tm, tn), lambda j, i: (i, j)),
        out_shape=jax.ShapeDtypeStruct((m, n_out), out_dtype),
        scratch_shapes=[pltpu.VMEM((k, tn), BF16)],
        compiler_params=_cparams(("arbitrary", "arbitrary")),
        name=name,
    )(a, w_stack)


def _rel_bucket(dist):
    n = jnp.maximum(dist, 0)
    max_exact = REL_BUCKETS // 2
    nf = jnp.maximum(n, 1).astype(F32)
    large = max_exact + (jnp.log(nf / max_exact) / math.log(REL_MAX_DIST / max_exact)
                         * (REL_BUCKETS - max_exact)).astype(jnp.int32)
    large = jnp.minimum(large, REL_BUCKETS - 1)
    return jnp.where(n < max_exact, n, large)


def _moba_kernel(relb_ref, q_ref, k_ref, vt_ref, o_ref,
                 kmean_ref, bown_ref, bprev_ref, sel_ref, m_ref, acc_ref, *, nb):
    g = pl.program_id(0)
    j = pl.program_id(1)
    blk = MOBA_BLOCK
    dh = ATT_HEAD_DIM
    scale = dh ** -0.5
    ln2 = math.log(2.0)
    nt = (((1,), (1,)), ((), ()))
    heads = range(MOBA_HEAD_GROUP)
    key_i = lax.broadcasted_iota(jnp.int32, (blk, blk), 0)
    qry_i = lax.broadcasted_iota(jnp.int32, (blk, blk), 1)

    @pl.when(j == 0)
    def _():
        for hh in heads:
            head = g * MOBA_HEAD_GROUP + hh
            kf = k_ref[:, hh * dh:(hh + 1) * dh].astype(F32).reshape(nb, blk, dh)
            kmean_ref[hh] = jnp.mean(kf, axis=1)
            for ref, off in ((bown_ref, 0), (bprev_ref, blk)):
                bucket = _rel_bucket(qry_i - key_i + off)
                bias = jnp.zeros((blk, blk), F32)
                for b in range(REL_BUCKETS):
                    bias = jnp.where(bucket == b, relb_ref[head, b], bias)
                ref[hh] = bias

    row0 = pl.multiple_of(j * blk, blk)
    blk_i = lax.broadcasted_iota(jnp.int32, (nb, blk), 0)
    q2 = []
    for hh in heads:
        q = q_ref[:, hh * dh:(hh + 1) * dh]
        q2.append((q.astype(F32) * (scale / ln2)).astype(BF16))
        score = lax.dot_general(kmean_ref[hh], q.astype(F32), nt,
                                preferred_element_type=F32, precision=HIGHEST)
        sc = jnp.where(blk_i < j, score, NEG_INF)
        seladd = jnp.full((nb, blk), NEG_INF, F32)
        for _ in range(MOBA_TOPK):
            mx = jnp.max(sc, axis=0, keepdims=True)
            cand = jnp.where(sc == mx, blk_i, nb)
            cand = jnp.where(mx > NEG_INF, cand, nb)
            idx = jnp.min(cand, axis=0, keepdims=True)
            pick = blk_i == idx
            seladd = jnp.where(pick, 0.0, seladd)
            sc = jnp.where(pick, NEG_INF, sc)
        sel_ref[hh] = seladd

    def scores(kb_row0, hh):
        return lax.dot_general(k_ref[pl.ds(kb_row0, blk), hh * dh:(hh + 1) * dh], q2[hh], nt,
                               preferred_element_type=F32)

    def weighted_values(kb, hh, p):
        return jnp.dot(vt_ref[kb, hh], p.astype(BF16), preferred_element_type=F32)

    for hh in heads:
        s = scores(row0, hh) * ln2 + bown_ref[hh]
        s = jnp.where(key_i <= qry_i, s, NEG_INF)
        m0 = jnp.max(s, axis=0, keepdims=True)
        m_ref[hh] = m0
        acc_ref[hh] = weighted_values(j, hh, jnp.exp(s - m0))

    def merge(hh, kb, m_blk, acc_blk):
        m_blk = m_blk + sel_ref[hh, pl.ds(kb, 1), :]
        m_old = m_ref[hh]
        m_new = jnp.maximum(m_old, m_blk)
        m_ref[hh] = m_new
        acc_ref[hh] = jnp.exp(m_old - m_new) * acc_ref[hh] + jnp.exp(m_blk - m_new) * acc_blk

    def far_block(kb, carry):
        r0 = pl.multiple_of(kb * blk, blk)
        s2 = [scores(r0, hh) for hh in heads]
        m2 = [jnp.max(s2[hh], axis=0, keepdims=True) for hh in heads]
        acc_blk = [weighted_values(kb, hh, jnp.exp2(s2[hh] - m2[hh])) for hh in heads]
        for hh in heads:
            far_bias = relb_ref[g * MOBA_HEAD_GROUP + hh, REL_BUCKETS - 1]
            merge(hh, kb, m2[hh] * ln2 + far_bias, acc_blk[hh])
        return carry

    lax.fori_loop(0, j - 1, far_block, 0)
    kb = jnp.maximum(j - 1, 0)
    r0 = pl.multiple_of(kb * blk, blk)
    for hh in heads:
        s = scores(r0, hh) * ln2 + bprev_ref[hh]
        m_blk = jnp.max(s, axis=0, keepdims=True)
        merge(hh, kb, m_blk, weighted_values(kb, hh, jnp.exp(s - m_blk)))
        acc = acc_ref[hh]
        o_ref[:, hh * dh:(hh + 1) * dh] = (acc[0:dh] / acc[dh:dh + 1]).T.astype(o_ref.dtype)


def _moba(proj, rel_bias):
    s = proj.shape[0]
    nb = s // MOBA_BLOCK
    hg = MOBA_HEAD_GROUP
    gw = hg * ATT_HEAD_DIM
    n_groups = ATT_HEADS // hg
    dhp = ATT_HEAD_DIM + MOBA_ONES_ROWS
    v_t = proj[:, 2 * BRANCH_WIDTH:3 * BRANCH_WIDTH].reshape(nb, MOBA_BLOCK, ATT_HEADS, ATT_HEAD_DIM)
    v_t = jnp.concatenate([v_t.transpose(0, 2, 3, 1),
                           jnp.ones((nb, ATT_HEADS, MOBA_ONES_ROWS, MOBA_BLOCK), BF16)], axis=2)
    return pl.pallas_call(
        functools.partial(_moba_kernel, nb=nb),
        grid=(n_groups, nb),
        in_specs=[pl.BlockSpec(memory_space=pltpu.SMEM),
                  pl.BlockSpec((MOBA_BLOCK, gw), lambda g, j: (j, g)),
                  pl.BlockSpec((s, gw), lambda g, j: (0, n_groups + g)),
                  pl.BlockSpec((nb, hg, dhp, MOBA_BLOCK), lambda g, j: (0, g, 0, 0))],
        out_specs=pl.BlockSpec((MOBA_BLOCK, gw), lambda g, j: (j, g)),
        out_shape=jax.ShapeDtypeStruct((s, BRANCH_WIDTH), BF16),
        scratch_shapes=[pltpu.VMEM((hg, nb, ATT_HEAD_DIM), F32),
                        pltpu.VMEM((hg, MOBA_BLOCK, MOBA_BLOCK), F32),
                        pltpu.VMEM((hg, MOBA_BLOCK, MOBA_BLOCK), F32),
                        pltpu.VMEM((hg, nb, MOBA_BLOCK), F32),
                        pltpu.VMEM((hg, 1, MOBA_BLOCK), F32),
                        pltpu.VMEM((hg, dhp, MOBA_BLOCK), F32)],
        compiler_params=_cparams(("arbitrary", "arbitrary")),
        name="moba",
    )(rel_bias.T, proj, proj, v_t)


def _s5_tables(lam_re, lam_im, log_dt, b_re, b_im, c_re, c_im):
    t_len = S5_CHUNK
    g_cnt, p_cnt, h_cnt = S5_GROUPS, S5_STATE, S5_GROUP
    tg = S5_TILE_GROUPS
    nt = g_cnt // tg
    dt = jnp.exp(log_dt)[:, None]
    ar, ai = lam_re * dt, lam_im * dt

    def lam_pow(steps):
        st = steps.astype(F32)[:, None, None]
        mag = jnp.exp(st * ar)
        return mag * jnp.cos(st * ai), mag * jnp.sin(st * ai)

    pr, pi = lam_pow(jnp.arange(t_len + 1))
    qr, qi = lam_pow((t_len - 1) - jnp.arange(t_len))
    nr, ni = pr[1] - 1.0, pi[1]
    den = lam_re * lam_re + lam_im * lam_im
    rr, ri = (nr * lam_re + ni * lam_im) / den, (ni * lam_re - nr * lam_im) / den
    bbr = rr[..., None] * b_re - ri[..., None] * b_im
    bbi = rr[..., None] * b_im + ri[..., None] * b_re

    def c_times(xr, xi):
        return (c_re[None] * xr[:, :, None, :] - c_im[None] * xi[:, :, None, :],
                c_re[None] * xi[:, :, None, :] + c_im[None] * xr[:, :, None, :])

    cpr, cpi = c_times(pr, pi)
    cqr, cqi = c_times(qr, qi)
    kc = (jnp.einsum('tghp,gpk->tgkh', cqr, bbr, precision=HIGHEST)
          - jnp.einsum('tghp,gpk->tgkh', cqi, bbi, precision=HIGHEST))
    kd = kc.reshape(t_len, nt, tg, h_cnt, h_cnt).transpose(1, 0, 3, 2, 4).reshape(nt, t_len, h_cnt, LANES)
    qr4, qi4 = qr[:, :, :, None], qi[:, :, :, None]
    bzr = qr4 * bbr[None] - qi4 * bbi[None]
    bzi = qr4 * bbi[None] + qi4 * bbr[None]
    bz = jnp.stack([bzr, bzi], axis=0).reshape(2, t_len, nt, tg, p_cnt, h_cnt)
    bz = bz.transpose(2, 1, 5, 0, 3, 4).reshape(nt, t_len, h_cnt, 2 * tg * p_cnt)
    cm = jnp.stack([cpr[1:], -cpi[1:]], axis=0).reshape(2, t_len, nt, tg, h_cnt, p_cnt)
    cm = cm.transpose(2, 1, 0, 5, 3, 4).reshape(nt, t_len, 2 * p_cnt, LANES)
    a_re = pr[t_len].reshape(nt, 1, tg * p_cnt)
    a_im = pi[t_len].reshape(nt, 1, tg * p_cnt)
    return kd.astype(BF16), bz.astype(BF16), cm.astype(BF16), a_re, a_im


def _s5_kernel(u_ref, kd_ref, bz_ref, cm_ref, are_ref, aim_ref, d_ref, y_ref,
               uf_ref, ucat_ref, z_ref, hc_ref, kdf_ref, bzf_ref, cmf_ref, *, nc):
    t_len = S5_CHUNK
    ns = S5_TILE_STATE
    tg, hs, ps = S5_TILE_GROUPS, S5_GROUP, S5_STATE
    chan_grp = lax.broadcasted_iota(jnp.int32, (LANES, LANES), 0) // hs
    lane_grp = lax.broadcasted_iota(jnp.int32, (LANES, LANES), 1) // hs
    state_grp = (lax.broadcasted_iota(jnp.int32, (LANES, 2 * ns), 1) % ns) // ps
    chan_grp_w = lax.broadcasted_iota(jnp.int32, (LANES, 2 * ns), 0) // hs
    lane_grp_p = lax.broadcasted_iota(jnp.int32, (ps, LANES), 1) // hs
    zero = jnp.zeros((), BF16)
    for s in range(t_len):
        rows = slice(s * LANES, (s + 1) * LANES)
        kdf_ref[rows, :] = jnp.where(chan_grp == lane_grp, jnp.concatenate([kd_ref[0, s]] * tg, axis=0), zero)
        bzf_ref[rows, :] = jnp.where(chan_grp_w == state_grp, jnp.concatenate([bz_ref[0, s]] * tg, axis=0), zero)
        for x in range(2):
            piece = cm_ref[0, s, x * ps:(x + 1) * ps, :]
            for g in range(tg):
                cmf_ref[s, x * ns + g * ps:x * ns + (g + 1) * ps, :] = jnp.where(lane_grp_p == g, piece, zero)
    uf_ref[...] = u_ref[...].astype(F32)
    for s in range(t_len):
        ucat_ref[:, s * LANES:(s + 1) * LANES] = uf_ref[pl.ds(s, nc, stride=t_len), :].astype(BF16)
    z_ref[...] = jnp.dot(ucat_ref[...], bzf_ref[...], preferred_element_type=F32)
    a_re = are_ref[0]
    a_im = aim_ref[0]

    def step(c, carry):
        h_re, h_im = carry
        hc_ref[pl.ds(c, 1), 0:ns] = h_re
        hc_ref[pl.ds(c, 1), ns:2 * ns] = h_im
        z_re = z_ref[pl.ds(c, 1), 0:ns]
        z_im = z_ref[pl.ds(c, 1), ns:2 * ns]
        return (a_re * h_re - a_im * h_im + z_re, a_re * h_im + a_im * h_re + z_im)

    zero_row = jnp.zeros((1, ns), F32)
    lax.fori_loop(0, nc, step, (zero_row, zero_row))
    hc = hc_ref[...].astype(BF16)
    d_skip = d_ref[...]
    for t in range(t_len):
        acc = jnp.dot(hc, cmf_ref[t], preferred_element_type=F32)
        acc += jnp.dot(ucat_ref[:, 0:(t + 1) * LANES],
                       kdf_ref[(t_len - 1 - t) * LANES:t_len * LANES, :],
                       preferred_element_type=F32)
        acc += d_skip * uf_ref[pl.ds(t, nc, stride=t_len), :]
        y_ref[pl.ds(t, nc, stride=t_len), :] = jax.nn.gelu(acc)


def _s5_glu_kernel(y_ref, yn_ref, w_ref, b_ref, o_ref):
    z = jnp.dot(y_ref[...].astype(BF16), w_ref[...], preferred_element_type=F32) + b_ref[...]
    o_ref[...] = (yn_ref[...] * jax.nn.sigmoid(z)).astype(o_ref.dtype)


def _s5(proj, lam_re, lam_im, log_dt, b_re, b_im, c_re, c_im, d_skip, w_glu, b_glu):
    s = proj.shape[0]
    nc = s // S5_CHUNK
    nt = S5_GROUPS // S5_TILE_GROUPS
    kd, bz, cm, a_re, a_im = _s5_tables(lam_re, lam_im, log_dt, b_re, b_im, c_re, c_im)
    u_col = 3 * BRANCH_WIDTH // LANES
    ns2 = 2 * S5_TILE_STATE
    y = pl.pallas_call(
        functools.partial(_s5_kernel, nc=nc),
        grid=(nt,),
        in_specs=[pl.BlockSpec((s, LANES), lambda c: (0, u_col + c)),
                  pl.BlockSpec((1, S5_CHUNK, S5_GROUP, LANES), lambda c: (c, 0, 0, 0)),
                  pl.BlockSpec((1, S5_CHUNK, S5_GROUP, ns2), lambda c: (c, 0, 0, 0)),
                  pl.BlockSpec((1, S5_CHUNK, 2 * S5_STATE, LANES), lambda c: (c, 0, 0, 0)),
                  pl.BlockSpec((1, 1, S5_TILE_STATE), lambda c: (c, 0, 0)),
                  pl.BlockSpec((1, 1, S5_TILE_STATE), lambda c: (c, 0, 0)),
                  pl.BlockSpec((1, LANES), lambda c: (0, c))],
        out_specs=pl.BlockSpec((s, LANES), lambda c: (0, c)),
        out_shape=jax.ShapeDtypeStruct((s, BRANCH_WIDTH), F32),
        scratch_shapes=[pltpu.VMEM((s, LANES), F32),
                        pltpu.VMEM((nc, S5_CHUNK * LANES), BF16),
                        pltpu.VMEM((nc, ns2), F32),
                        pltpu.VMEM((nc, ns2), F32),
                        pltpu.VMEM((S5_CHUNK * LANES, LANES), BF16),
                        pltpu.VMEM((S5_CHUNK * LANES, ns2), BF16),
                        pltpu.VMEM((S5_CHUNK, ns2, LANES), BF16)],
        compiler_params=_cparams(("arbitrary",)),
        name="s5_scan",
    )(proj, kd, bz, cm, a_re, a_im, d_skip.reshape(1, BRANCH_WIDTH))
    tm, tn = 512, 512
    return pl.pallas_call(
        _s5_glu_kernel,
        grid=(BRANCH_WIDTH // tn, s // tm),
        in_specs=[pl.BlockSpec((tm, BRANCH_WIDTH), lambda j, i: (i, 0)),
                  pl.BlockSpec((tm, tn), lambda j, i: (i, j)),
                  pl.BlockSpec((BRANCH_WIDTH, tn), lambda j, i: (0, j)),
                  pl.BlockSpec((1, tn), lambda j, i: (0, j))],
        out_specs=pl.BlockSpec((tm, tn), lambda j, i: (i, j)),
        out_shape=jax.ShapeDtypeStruct((s, BRANCH_WIDTH), BF16),
        compiler_params=_cparams(("arbitrary", "arbitrary")),
        name="s5_glu",
    )(y, y, w_glu.astype(BF16), b_glu.reshape(1, BRANCH_WIDTH))


def _gla_kernel(q_ref, k_ref, v_ref, r_ref, tail_ref, wg_ref, bg_ref, ng_ref, o_ref, st_ref):
    dk = GLA_KEY // GLA_HEADS
    dv = GLA_VAL // GLA_HEADS
    cs = GLA_CHUNK
    sub = GLA_SUB
    nt = (((1,), (1,)), ((), ()))
    tn = (((0,), (0,)), ((), ()))

    @pl.when(pl.program_id(0) == 0)
    def _():
        st_ref[...] = jnp.zeros_like(st_ref)

    gate_in = jnp.dot(tail_ref[...], wg_ref[...], preferred_element_type=F32) + bg_ref[...]
    log_a = jax.nn.log_sigmoid(gate_in) / GLA_GATE_TAU
    ri = lax.broadcasted_iota(jnp.int32, (cs, cs), 0)
    ci = lax.broadcasted_iota(jnp.int32, (cs, cs), 1)
    tril = (ri >= ci).astype(F32)
    bcum_all = jnp.dot(tril, log_a, preferred_element_type=F32, precision=HIGHEST)
    sub_row = lax.broadcasted_iota(jnp.int32, (sub, 1), 0)

    for h in range(GLA_HEADS):
        q = q_ref[:, h * dk:(h + 1) * dk].astype(F32) * dk ** -0.5
        k = k_ref[:, h * dk:(h + 1) * dk].astype(F32)
        v_bf = v_ref[:, h * dv:(h + 1) * dv]
        v = v_bf.astype(F32)
        bc = bcum_all[:, h * dk:(h + 1) * dk]
        state = st_ref[h]
        o_inter = lax.dot_general((q * jnp.exp(bc)).astype(BF16), state.astype(BF16), nt,
                                  preferred_element_type=F32)
        parts = []
        for i in range(cs // sub):
            lo = i * sub
            b_i, q_i, k_i, v_i = bc[lo:lo + sub], q[lo:lo + sub], k[lo:lo + sub], v[lo:lo + sub]
            o_i = o_inter[lo:lo + sub]
            if i > 0:
                ref = bc[lo - 1:lo]
                q_h = (q_i * jnp.exp(b_i - ref)).astype(BF16)
                k_h = (k[:lo] * jnp.exp(ref - bc[:lo])).astype(BF16)
                attn = lax.dot_general(q_h, k_h, nt, preferred_element_type=F32)
                o_i = o_i + jnp.dot(attn.astype(BF16), v_bf[:lo], preferred_element_type=F32)
            for dlt in range(sub):
                if dlt == 0:
                    a = jnp.sum(q_i * k_i, axis=-1, keepdims=True)
                    o_i = o_i + a * v_i
                else:
                    b_s = pltpu.roll(b_i, dlt, 0)
                    k_s = pltpu.roll(k_i, dlt, 0)
                    v_s = pltpu.roll(v_i, dlt, 0)
                    e = jnp.exp(jnp.minimum(b_i - b_s, 0.0))
                    a = jnp.sum(q_i * k_s * e, axis=-1, keepdims=True)
                    a = jnp.where(sub_row >= dlt, a, 0.0)
                    o_i = o_i + a * v_s
            parts.append(o_i)
        o = jnp.concatenate(parts, axis=0)
        b_last = bc[cs - 1:cs]
        k_dec = (k * jnp.exp(b_last - bc)).astype(BF16)
        st_ref[h] = state * jnp.exp(b_last) + lax.dot_general(v_bf, k_dec, tn,
                                                             preferred_element_type=F32)
        o = o * lax.rsqrt(jnp.mean(o * o, axis=-1, keepdims=True) + NORM_EPS)
        o = o * ng_ref[:, h * dv:(h + 1) * dv]
        r = r_ref[:, h * dv:(h + 1) * dv].astype(F32)
        o_ref[:, h * dv:(h + 1) * dv] = (o * (r * jax.nn.sigmoid(r))).astype(o_ref.dtype)


def _gla(proj, w_gate, b_gate, norm_g):
    s = proj.shape[0]
    cs = GLA_CHUNK
    wg_pad = jnp.pad(w_gate, ((0, TAIL_W - GLA_GATE_RANK), (0, 0))).astype(BF16)
    return pl.pallas_call(
        _gla_kernel,
        grid=(s // cs,),
        in_specs=[pl.BlockSpec((cs, GLA_KEY), lambda c: (c, 4 * BRANCH_WIDTH // GLA_KEY)),
                  pl.BlockSpec((cs, GLA_KEY), lambda c: (c, 4 * BRANCH_WIDTH // GLA_KEY + 1)),
                  pl.BlockSpec((cs, GLA_VAL), lambda c: (c, 5)),
                  pl.BlockSpec((cs, GLA_VAL), lambda c: (c, 6)),
                  pl.BlockSpec((cs, TAIL_W), lambda c: (c, TAIL_COL // TAIL_W)),
                  pl.BlockSpec((TAIL_W, GLA_KEY), lambda c: (0, 0)),
                  pl.BlockSpec((1, GLA_KEY), lambda c: (0, 0)),
                  pl.BlockSpec((1, GLA_VAL), lambda c: (0, 0))],
        out_specs=pl.BlockSpec((cs, GLA_VAL), lambda c: (c, 0)),
        out_shape=jax.ShapeDtypeStruct((s, GLA_VAL), BF16),
        scratch_shapes=[pltpu.VMEM((GLA_HEADS, GLA_VAL // GLA_HEADS, GLA_KEY // GLA_HEADS), F32)],
        compiler_params=_cparams(("arbitrary",)),
        name="gla",
    )(proj, proj, proj, proj, proj, wg_pad, b_gate.reshape(1, GLA_KEY), norm_g.reshape(1, GLA_VAL))


def _merge_kernel(ya_ref, ys_ref, yg_ref, tail_ref, wb_ref, wg_ref, bg_ref, o_ref):
    tail = tail_ref[...]
    acc = None
    for n, y_ref in enumerate((ya_ref, ys_ref, yg_ref)):
        up = jnp.dot(y_ref[...], wb_ref[n], preferred_element_type=F32)
        gate = jax.nn.sigmoid(jnp.dot(tail, wg_ref[n], preferred_element_type=F32) + bg_ref[n])
        acc = gate * up if acc is None else acc + gate * up
    o_ref[...] = acc.astype(o_ref.dtype)


def _merge(y_att, y_s5, y_gla, proj, w_branch, w_merge_gate, b_merge_gate, tm=512, tn=512):
    s = proj.shape[0]
    wb = w_branch.astype(BF16)
    wg = w_merge_gate.reshape(MERGE_RANK, N_BRANCH, D_MODEL).transpose(1, 0, 2)
    wg = jnp.pad(wg, ((0, 0), (GLA_GATE_RANK, TAIL_W - GLA_GATE_RANK - MERGE_RANK), (0, 0))).astype(BF16)
    bg = b_merge_gate.reshape(N_BRANCH, 1, D_MODEL)
    ybs = pl.BlockSpec((tm, BRANCH_WIDTH), lambda j, i: (i, 0))
    return pl.pallas_call(
        _merge_kernel,
        grid=(D_MODEL // tn, s // tm),
        in_specs=[ybs, ybs, ybs,
                  pl.BlockSpec((tm, TAIL_W), lambda j, i: (i, TAIL_COL // TAIL_W)),
                  pl.BlockSpec((N_BRANCH, BRANCH_WIDTH, tn), lambda j, i: (0, 0, j)),
                  pl.BlockSpec((N_BRANCH, TAIL_W, tn), lambda j, i: (0, 0, j)),
                  pl.BlockSpec((N_BRANCH, 1, tn), lambda j, i: (0, 0, j))],
        out_specs=pl.BlockSpec((tm, tn), lambda j, i: (i, j)),
        out_shape=jax.ShapeDtypeStruct((s, D_MODEL), BF16),
        compiler_params=_cparams(("arbitrary", "arbitrary")),
        name="merge",
    )(y_att, y_s5, y_gla, proj, wb, wg, bg)


def _swiglu_kernel(a_ref, w1_ref, w3_ref, o_ref, w1b_ref, w3b_ref):
    @pl.when(pl.program_id(1) == 0)
    def _():
        w1b_ref[...] = w1_ref[0].astype(BF16)
        w3b_ref[...] = w3_ref[0].astype(BF16)

    a = a_ref[...]
    g = jnp.dot(a, w1b_ref[...], preferred_element_type=F32)
    u = jnp.dot(a, w3b_ref[...], preferred_element_type=F32)
    o_ref[...] = (g * jax.nn.sigmoid(g) * u).astype(o_ref.dtype)


def _swiglu_hidden(a, w1_stack, w3_stack, layer, tm=1024, tn=256):
    m, k = a.shape
    n = w1_stack.shape[2]
    wspec = pl.BlockSpec((1, k, tn), lambda j, i: (layer, 0, j))
    return pl.pallas_call(
        _swiglu_kernel,
        grid=(n // tn, m // tm),
        in_specs=[pl.BlockSpec((tm, k), lambda j, i: (i, 0)), wspec, wspec],
        out_specs=pl.BlockSpec((tm, tn), lambda j, i: (i, j)),
        out_shape=jax.ShapeDtypeStruct((m, n), BF16),
        scratch_shapes=[pltpu.VMEM((k, tn), BF16), pltpu.VMEM((k, tn), BF16)],
        compiler_params=_cparams(("arbitrary", "arbitrary")),
        name="swiglu_hidden",
    )(a, w1_stack, w3_stack)


def _moe_plan(route):
    s = route.shape[0]
    tile = MOE_TILE
    n_tiles = (TOP_K * s) // tile + N_EXPERTS
    ids = route[:, ROUTE_ID:ROUTE_ID + TOP_K].astype(jnp.int32)
    onehot = jnp.sum(jax.nn.one_hot(ids, N_EXPERTS, dtype=jnp.int32), axis=1)
    before = jnp.cumsum(onehot, axis=0) - onehot
    counts = jnp.sum(onehot, axis=0)
    padded = (counts + tile - 1) // tile * tile
    ends = jnp.cumsum(padded)
    offsets = ends - padded
    pos = offsets[ids] + jnp.take_along_axis(before, ids, axis=1)
    tile_start = jnp.arange(n_tiles, dtype=jnp.int32) * tile
    tile_expert = jnp.minimum(jnp.sum(tile_start[:, None] >= ends[None, :], axis=1), N_EXPERTS - 1)
    n_active = (ends[-1] // tile).reshape(1)
    tokens = jnp.broadcast_to(jnp.arange(s, dtype=jnp.int32)[:, None], pos.shape)
    row_token = jnp.zeros((n_tiles * tile,), jnp.int32).at[pos.reshape(-1)].set(tokens.reshape(-1))
    pos1 = pos[:, 0].reshape(s // LANES, LANES)
    pos2 = pos[:, 1].reshape(s // LANES, LANES)
    return (pos1, pos2, row_token.reshape(-1, LANES), tile_expert.astype(jnp.int32),
            n_active.astype(jnp.int32), n_tiles)


def _moe_dispatch_kernel(rt_ref, na_ref, h_ref, o_ref, buf_ref, sem):
    i = pl.program_id(0)

    @pl.when(i < na_ref[0])
    def _():
        _gather_rows(lambda r: (_row_copy(h_ref, buf_ref, sem, _smem_at(rt_ref, i * MOE_TILE + r), r),), MOE_TILE)
        o_ref[...] = buf_ref[...].astype(o_ref.dtype)

    @pl.when(i >= na_ref[0])
    def _():
        o_ref[...] = jnp.zeros_like(o_ref)


def _moe_dispatch(h, row_token, n_active, n_tiles):
    tile = MOE_TILE
    return pl.pallas_call(
        _moe_dispatch_kernel,
        grid_spec=pltpu.PrefetchScalarGridSpec(
            num_scalar_prefetch=2, grid=(n_tiles,),
            in_specs=[pl.BlockSpec(memory_space=pl.ANY)],
            out_specs=pl.BlockSpec((tile, D_MODEL), lambda i, rt, na: (i, 0)),
            scratch_shapes=[pltpu.VMEM((tile, D_MODEL), F32), pltpu.SemaphoreType.DMA(())]),
        out_shape=jax.ShapeDtypeStruct((n_tiles * tile, D_MODEL), BF16),
        compiler_params=_cparams(("arbitrary",)),
        name="moe_dispatch",
    )(row_token, n_active, h)


def _new_expert_panel(te_ref):
    i = pl.program_id(1)
    return jnp.logical_or(i == 0, te_ref[i] != te_ref[jnp.maximum(i - 1, 0)])


def _moe_hidden_kernel(te_ref, na_ref, a_ref, w1_ref, w3_ref, o_ref, w1b_ref, w3b_ref):
    @pl.when(_new_expert_panel(te_ref))
    def _():
        w1b_ref[...] = w1_ref[0, 0].astype(BF16)
        w3b_ref[...] = w3_ref[0, 0].astype(BF16)

    @pl.when(pl.program_id(1) < na_ref[0])
    def _():
        a = a_ref[...]
        g = jnp.dot(a, w1b_ref[...], preferred_element_type=F32)
        u = jnp.dot(a, w3b_ref[...], preferred_element_type=F32)
        o_ref[...] = (g * jax.nn.sigmoid(g) * u).astype(o_ref.dtype)

    @pl.when(pl.program_id(1) >= na_ref[0])
    def _():
        o_ref[...] = jnp.zeros_like(o_ref)


def _moe_down_kernel(te_ref, na_ref, a_ref, w_ref, o_ref, wb_ref):
    @pl.when(_new_expert_panel(te_ref))
    def _():
        wb_ref[...] = w_ref[0, 0].astype(BF16)

    @pl.when(pl.program_id(1) < na_ref[0])
    def _():
        o_ref[...] = jnp.dot(a_ref[...], wb_ref[...], preferred_element_type=F32).astype(o_ref.dtype)

    @pl.when(pl.program_id(1) >= na_ref[0])
    def _():
        o_ref[...] = jnp.zeros_like(o_ref)


def _moe_experts(xs, tile_expert, n_active, w1_stack, w3_stack, w2_stack, layer, n_tiles,
                 tn_hidden=256, tn_down=1024):
    tile = MOE_TILE
    rows = n_tiles * tile
    w_in_spec = pl.BlockSpec((1, 1, D_MODEL, tn_hidden), lambda n, i, te, na: (layer, te[i], 0, n))
    hid = pl.pallas_call(
        _moe_hidden_kernel,
        grid_spec=pltpu.PrefetchScalarGridSpec(
            num_scalar_prefetch=2, grid=(D_FF_EXPERT // tn_hidden, n_tiles),
            in_specs=[pl.BlockSpec((tile, D_MODEL), lambda n, i, te, na: (i, 0)), w_in_spec, w_in_spec],
            out_specs=pl.BlockSpec((tile, tn_hidden), lambda n, i, te, na: (i, n)),
            scratch_shapes=[pltpu.VMEM((D_MODEL, tn_hidden), BF16), pltpu.VMEM((D_MODEL, tn_hidden), BF16)]),
        out_shape=jax.ShapeDtypeStruct((rows, D_FF_EXPERT), BF16),
        compiler_params=_cparams(("arbitrary", "arbitrary")),
        name="moe_hidden",
    )(tile_expert, n_active, xs, w1_stack, w3_stack)
    return pl.pallas_call(
        _moe_down_kernel,
        grid_spec=pltpu.PrefetchScalarGridSpec(
            num_scalar_prefetch=2, grid=(D_MODEL // tn_down, n_tiles),
            in_specs=[pl.BlockSpec((tile, D_FF_EXPERT), lambda n, i, te, na: (i, 0)),
                      pl.BlockSpec((1, 1, D_FF_EXPERT, tn_down), lambda n, i, te, na: (layer, te[i], 0, n))],
            out_specs=pl.BlockSpec((tile, tn_down), lambda n, i, te, na: (i, n)),
            scratch_shapes=[pltpu.VMEM((D_FF_EXPERT, tn_down), BF16)]),
        out_shape=jax.ShapeDtypeStruct((rows, D_MODEL), F32),
        compiler_params=_cparams(("arbitrary", "arbitrary")),
        name="moe_down",
    )(tile_expert, n_active, hid, w2_stack)


def kernel(x, c, w_cond, b_cond, rel_bias, w_mod, b_mod, w_in, s5_lambda_re, s5_lambda_im, s5_log_dt, s5_b_re, s5_b_im, s5_c_re, s5_c_im, s5_d, s5_w_glu, s5_b_glu, gla_w_gate, gla_b_gate, gla_norm_g, w_branch, w_merge_gate, b_merge_gate, w_out, ln1_g, ln1_b, ffn_w1, ffn_w3, ffn_w2, router_w, router_b, exp_w1, exp_w3, exp_w2, ln2_g, ln2_b):
    bsz, seq, _ = x.shape
    assert bsz == 1
    mod = _conditioning(c, w_cond, b_cond, w_mod, b_mod)
    xs = x.reshape(seq, D_MODEL)
    hm = _modulate(xs, mod[0, 1], mod[0, 0])
    for l in range(DEPTH):
        shift_f, scale_f, gate_m, gate_f = mod[l, 3], mod[l, 4], mod[l, 2], mod[l, 5]
        proj = _matmul(hm, w_in, l, tm=512, tn=512, n_out=D_IN_PAD, name="in_proj")
        y_att = _moba(proj, rel_bias)
        y_s5 = _s5(proj, s5_lambda_re[l], s5_lambda_im[l], s5_log_dt[l], s5_b_re[l], s5_b_im[l],
                   s5_c_re[l], s5_c_im[l], s5_d[l], s5_w_glu[l], s5_b_glu[l])
        y_gla = _gla(proj, gla_w_gate[l], gla_b_gate[l], gla_norm_g[l])
        merged = _merge(y_att, y_s5, y_gla, proj, w_branch[l], w_merge_gate[l], b_merge_gate[l])
        y = _matmul(merged, w_out, l, tm=512, tn=512, name="out_proj")
        dense = l % 2 == 0
        router = None if dense else (router_w[l // 2], router_b[l // 2])
        outs = _deepnorm_ln(xs, y, gate_m, ln1_g[l], ln1_b[l], nxt=(scale_f, shift_f), router=router)
        xs, hf = outs[0], outs[1]
        if dense:
            hid = _swiglu_hidden(hf, ffn_w1, ffn_w3, l // 2)
            f = _matmul(hid, ffn_w2, l // 2, tm=256, tn=512, name="ffn_down")
            moe = None
        else:
            route = outs[2]
            pos1, pos2, row_token, tile_expert, n_active, n_tiles = _moe_plan(route)
            xsorted = _moe_dispatch(hf, row_token, n_active, n_tiles)
            ys = _moe_experts(xsorted, tile_expert, n_active, exp_w1, exp_w3, exp_w2, l // 2, n_tiles)
            f = None
            moe = (ys, pos1, pos2, route)
        nxt = (mod[l + 1, 1], mod[l + 1, 0]) if l + 1 < DEPTH else None
        outs = _deepnorm_ln(xs, f, gate_f, ln2_g[l], ln2_b[l], nxt=nxt, moe=moe)
        xs = outs[0]
        if nxt is not None:
            hm = outs[1]
    return xs.reshape(bsz, seq, D_MODEL)
```

```python
import functools
import math

import jax
import jax.numpy as jnp
from jax import lax
from jax.experimental import pallas as pl
from jax.experimental.pallas import tpu as pltpu

F32 = jnp.float32
BF16 = jnp.bfloat16
HIGHEST = lax.Precision.HIGHEST

D_MODEL = 4096
DEPTH = 4
BRANCH_WIDTH = 1024
N_BRANCH = 3
ATT_HEADS = 8
ATT_HEAD_DIM = 128
MOBA_BLOCK = 256
MOBA_TOPK = 3
MOBA_HEAD_GROUP = 4
MOBA_ONES_ROWS = 16
REL_BUCKETS = 32
REL_MAX_DIST = 128
S5_GROUP = 16
S5_GROUPS = 64
S5_STATE = 64
S5_CHUNK = 16
S5_TILE_GROUPS = 8
S5_TILE_STATE = S5_TILE_GROUPS * S5_STATE
GLA_HEADS = 4
GLA_KEY = 512
GLA_VAL = 1024
GLA_GATE_RANK = 16
GLA_GATE_TAU = 16.0
GLA_CHUNK = 64
GLA_SUB = 16
MERGE_RANK = 256
COND_RANK = 512
D_FF = 8192
N_EXPERTS = 8
TOP_K = 2
ROUTE_ID = 8
ROUTE_W = 10
MOE_TILE = 256
D_FF_EXPERT = 1792
DN_ALPHA = (2 * DEPTH) ** 0.25
LN_EPS = 1e-5
NORM_EPS = 1e-6

D_IN = 7440
D_IN_PAD = 7680
TAIL_COL = 7168
TAIL_W = 512
LANES = 128
VMEM_LIMIT = 56 * 1024 * 1024
NEG_INF = float("-inf")


def _cparams(sem):
    return pltpu.CompilerParams(dimension_semantics=sem, vmem_limit_bytes=VMEM_LIMIT)


def _cond_kernel(c_ref, w_ref, b_ref, o_ref):
    z = jnp.dot(c_ref[...], w_ref[...], preferred_element_type=F32, precision=HIGHEST) + b_ref[...]
    o_ref[...] = z * jax.nn.sigmoid(z)


def _mod_kernel(cond_ref, w_ref, b_ref, o_ref):
    o_ref[0] = jnp.dot(cond_ref[...], w_ref[0], preferred_element_type=F32,
                       precision=HIGHEST) + b_ref[0]


def _conditioning(c, w_cond, b_cond, w_mod, b_mod):
    c8 = jnp.broadcast_to(c, (8, D_MODEL))
    cond = pl.pallas_call(
        _cond_kernel,
        out_shape=jax.ShapeDtypeStruct((8, COND_RANK), F32),
        compiler_params=_cparams(None),
        name="cond",
    )(c8, w_cond, b_cond.reshape(1, COND_RANK))
    n_mod = 6 * D_MODEL
    tn = 3072
    mod = pl.pallas_call(
        _mod_kernel,
        grid=(DEPTH, n_mod // tn),
        in_specs=[pl.BlockSpec((8, COND_RANK), lambda l, n: (0, 0)),
                  pl.BlockSpec((1, COND_RANK, tn), lambda l, n: (l, 0, n)),
                  pl.BlockSpec((1, 1, tn), lambda l, n: (l, 0, n))],
        out_specs=pl.BlockSpec((1, 8, tn), lambda l, n: (l, 0, n)),
        out_shape=jax.ShapeDtypeStruct((DEPTH, 8, n_mod), F32),
        compiler_params=_cparams(("arbitrary", "arbitrary")),
        name="mod",
    )(cond, w_mod, b_mod.reshape(DEPTH, 1, n_mod))
    return mod[:, 0, :].reshape(DEPTH, 6, 1, D_MODEL)


def _modulate_kernel(x_ref, scale_ref, shift_ref, o_ref):
    o_ref[...] = (x_ref[...] * (1.0 + scale_ref[...]) + shift_ref[...]).astype(o_ref.dtype)


def _modulate(x, scale, shift, tm=512):
    s = x.shape[0]
    vec = pl.BlockSpec((1, D_MODEL), lambda m: (0, 0))
    return pl.pallas_call(
        _modulate_kernel,
        grid=(s // tm,),
        in_specs=[pl.BlockSpec((tm, D_MODEL), lambda m: (m, 0)), vec, vec],
        out_specs=pl.BlockSpec((tm, D_MODEL), lambda m: (m, 0)),
        out_shape=jax.ShapeDtypeStruct((s, D_MODEL), BF16),
        compiler_params=_cparams(("arbitrary",)),
        name="modulate",
    )(x, scale, shift)


def _route_top2(logits):
    lane = lax.broadcasted_iota(jnp.int32, logits.shape, 1)
    m1 = jnp.max(logits, axis=-1, keepdims=True)
    i1 = jnp.min(jnp.where(logits == m1, lane, LANES), axis=-1, keepdims=True)
    rest = jnp.where(lane == i1, NEG_INF, logits)
    m2 = jnp.max(rest, axis=-1, keepdims=True)
    i2 = jnp.min(jnp.where(rest == m2, lane, LANES), axis=-1, keepdims=True)
    e2 = jnp.exp(m2 - m1)
    denom = 1.0 + e2
    rec = jnp.where(lane == ROUTE_ID, i1.astype(F32), 0.0)
    rec = jnp.where(lane == ROUTE_ID + 1, i2.astype(F32), rec)
    rec = jnp.where(lane == ROUTE_W, 1.0 / denom, rec)
    return jnp.where(lane == ROUTE_W + 1, e2 / denom, rec)


def _row_copy(src_ref, dst_ref, sem, src_row, dst_row):
    return pltpu.make_async_copy(src_ref.at[pl.ds(src_row, 1)], dst_ref.at[pl.ds(dst_row, 1)], sem)


def _smem_at(ref, i):
    return ref[i // LANES, i % LANES]


def _gather_rows(copies_of, n_rows):
    def start(r, carry):
        for cp in copies_of(r):
            cp.start()
        return carry

    def wait(r, carry):
        for cp in copies_of(r):
            cp.wait()
        return carry

    lax.fori_loop(0, n_rows, start, 0)
    lax.fori_loop(0, n_rows, wait, 0)


def _ln_kernel(*refs, has_next, has_router, moe_combine, tm):
    pos = 0
    if moe_combine:
        pos1_ref, pos2_ref, x_ref, route_ref, ys_ref = refs[:5]
        pos = 5
    else:
        x_ref, y_ref = refs[:2]
        pos = 2
    gate_ref, g_ref, b_ref = refs[pos:pos + 3]
    pos += 3
    if has_next:
        scale_ref, shift_ref = refs[pos:pos + 2]
        pos += 2
    if has_router:
        rw_ref, rb_ref = refs[pos:pos + 2]
        pos += 2
    xo_ref = refs[pos]
    pos += 1
    if moe_combine:
        buf1_ref, buf2_ref, sem = refs[-3:]
        base = pl.program_id(0) * tm

        _gather_rows(lambda r: (_row_copy(ys_ref, buf1_ref, sem, _smem_at(pos1_ref, base + r), r),
                                _row_copy(ys_ref, buf2_ref, sem, _smem_at(pos2_ref, base + r), r)), tm)
        route = route_ref[...]
        lane = lax.broadcasted_iota(jnp.int32, route.shape, 1)
        w1 = jnp.sum(jnp.where(lane == ROUTE_W, route, 0.0), axis=-1, keepdims=True)
        w2 = jnp.sum(jnp.where(lane == ROUTE_W + 1, route, 0.0), axis=-1, keepdims=True)
        y = w1 * buf1_ref[...] + w2 * buf2_ref[...]
    else:
        y = y_ref[...].astype(F32)
    z = DN_ALPHA * x_ref[...] + (1.0 + gate_ref[...]) * y
    mu = jnp.mean(z, axis=-1, keepdims=True)
    zc = z - mu
    var = jnp.mean(zc * zc, axis=-1, keepdims=True)
    xn = zc * lax.rsqrt(var + LN_EPS) * g_ref[...] + b_ref[...]
    xo_ref[...] = xn
    if has_next:
        ho_ref = refs[pos]
        pos += 1
        h = xn * (1.0 + scale_ref[...]) + shift_ref[...]
        ho_ref[...] = h.astype(ho_ref.dtype)
        if has_router:
            co_ref = refs[pos]
            logits = jnp.dot(h, rw_ref[...], preferred_element_type=F32, precision=HIGHEST)
            lane = lax.broadcasted_iota(jnp.int32, logits.shape, 1)
            logits = jnp.where(lane < N_EXPERTS, logits + rb_ref[...], NEG_INF)
            co_ref[...] = _route_top2(logits)


def _deepnorm_ln(x, y, gate, g, b, nxt=None, router=None, moe=None, tm=256):
    s = x.shape[0]
    n_pre = 0 if moe is None else 2
    imap = (lambda m: (m, 0)) if moe is None else (lambda m, p1, p2: (m, 0))
    vmap = (lambda m: (0, 0)) if moe is None else (lambda m, p1, p2: (0, 0))
    row = pl.BlockSpec((tm, D_MODEL), imap)
    vec = pl.BlockSpec((1, D_MODEL), vmap)
    scratch = []
    if moe is None:
        args = [x, y]
        in_specs = [row, row]
    else:
        ys, pos1, pos2, route = moe
        args = [pos1, pos2, x, route, ys]
        in_specs = [row, pl.BlockSpec((tm, LANES), imap), pl.BlockSpec(memory_space=pl.ANY)]
        scratch = [pltpu.VMEM((tm, D_MODEL), F32), pltpu.VMEM((tm, D_MODEL), F32),
                   pltpu.SemaphoreType.DMA(())]
    args += [gate, g.reshape(1, D_MODEL), b.reshape(1, D_MODEL)]
    in_specs += [vec, vec, vec]
    out_shape = [jax.ShapeDtypeStruct((s, D_MODEL), F32)]
    out_specs = [row]
    if nxt is not None:
        args += [nxt[0], nxt[1]]
        in_specs += [vec, vec]
        out_shape.append(jax.ShapeDtypeStruct((s, D_MODEL), BF16 if router is None else F32))
        out_specs.append(row)
    if router is not None:
        rw, rb = router
        rw_pad = jnp.pad(rw, ((0, 0), (0, LANES - N_EXPERTS)))
        rb_pad = jnp.pad(rb, (0, LANES - N_EXPERTS)).reshape(1, LANES)
        args += [rw_pad, rb_pad]
        in_specs += [pl.BlockSpec((D_MODEL, LANES), vmap), pl.BlockSpec((1, LANES), vmap)]
        out_shape.append(jax.ShapeDtypeStruct((s, LANES), F32))
        out_specs.append(pl.BlockSpec((tm, LANES), imap))
    return pl.pallas_call(
        functools.partial(_ln_kernel, has_next=nxt is not None, has_router=router is not None,
                          moe_combine=moe is not None, tm=tm),
        grid_spec=pltpu.PrefetchScalarGridSpec(
            num_scalar_prefetch=n_pre, grid=(s // tm,), in_specs=in_specs, out_specs=out_specs,
            scratch_shapes=scratch),
        out_shape=out_shape,
        compiler_params=_cparams(("arbitrary",)),
        name="deepnorm_ln",
    )(*args)


def _matmul_kernel(a_ref, w_ref, o_ref, wb_ref, *, n_valid, tn, w_transposed):
    @pl.when(pl.program_id(1) == 0)
    def _():
        w = w_ref[0]
        wb_ref[...] = (w.T if w_transposed else w).astype(BF16)

    acc = jnp.dot(a_ref[...], wb_ref[...], preferred_element_type=F32)
    if n_valid is not None:
        col = pl.program_id(0) * tn + lax.broadcasted_iota(jnp.int32, acc.shape, 1)
        acc = jnp.where(col < n_valid, acc, 0.0)
    o_ref[...] = acc.astype(o_ref.dtype)


def _matmul(a, w_stack, layer, tm, tn, out_dtype=BF16, n_out=None, w_transposed=False, name="matmul"):
    m, k = a.shape
    n = w_stack.shape[1 if w_transposed else 2]
    n_out = n if n_out is None else n_out
    if w_transposed:
        w_spec = pl.BlockSpec((1, tn, k), lambda j, i: (layer, j, 0))
    else:
        w_spec = pl.BlockSpec((1, k, tn), lambda j, i: (layer, 0, j))
    return pl.pallas_call(
        functools.partial(_matmul_kernel, n_valid=None if n_out == n else n, tn=tn,
                          w_transposed=w_transposed),
        grid=(n_out // tn, m // tm),
        in_specs=[pl.BlockSpec((tm, k), lambda j, i: (i, 0)), w_spec],
        out_specs=pl.BlockSpec((tm, tn), lambda j, i: (i, j)),
        out_shape=jax.ShapeDtypeStruct((m, n_out), out_dtype),
        scratch_shapes=[pltpu.VMEM((k, tn), BF16)],
        compiler_params=_cparams(("arbitrary", "arbitrary")),
        name=name,
    )(a, w_stack)


def _rel_bucket(dist):
    n = jnp.maximum(dist, 0)
    max_exact = REL_BUCKETS // 2
    nf = jnp.maximum(n, 1).astype(F32)
    large = max_exact + (jnp.log(nf / max_exact) / math.log(REL_MAX_DIST / max_exact)
                         * (REL_BUCKETS - max_exact)).astype(jnp.int32)
    large = jnp.minimum(large, REL_BUCKETS - 1)
    return jnp.where(n < max_exact, n, large)


def _moba_kernel(relb_ref, q_ref, k_ref, vt_ref, o_ref,
                 kmean_ref, bown_ref, bprev_ref, sel_ref, m_ref, acc_ref, *, nb):
    g = pl.program_id(0)
    j = pl.program_id(1)
    blk = MOBA_BLOCK
    dh = ATT_HEAD_DIM
    scale = dh ** -0.5
    ln2 = math.log(2.0)
    nt = (((1,), (1,)), ((), ()))
    heads = range(MOBA_HEAD_GROUP)
    key_i = lax.broadcasted_iota(jnp.int32, (blk, blk), 0)
    qry_i = lax.broadcasted_iota(jnp.int32, (blk, blk), 1)

    @pl.when(j == 0)
    def _():
        for hh in heads:
            head = g * MOBA_HEAD_GROUP + hh
            kf = k_ref[:, hh * dh:(hh + 1) * dh].astype(F32).reshape(nb, blk, dh)
            kmean_ref[hh] = jnp.mean(kf, axis=1)
            for ref, off in ((bown_ref, 0), (bprev_ref, blk)):
                bucket = _rel_bucket(qry_i - key_i + off)
                bias = jnp.zeros((blk, blk), F32)
                for b in range(REL_BUCKETS):
                    bias = jnp.where(bucket == b, relb_ref[head, b], bias)
                ref[hh] = bias

    row0 = pl.multiple_of(j * blk, blk)
    blk_i = lax.broadcasted_iota(jnp.int32, (nb, blk), 0)
    q2 = []
    for hh in heads:
        q = q_ref[:, hh * dh:(hh + 1) * dh]
        q2.append((q.astype(F32) * (scale / ln2)).astype(BF16))
        score = lax.dot_general(kmean_ref[hh], q.astype(F32), nt,
                                preferred_element_type=F32, precision=HIGHEST)
        sc = jnp.where(blk_i < j, score, NEG_INF)
        seladd = jnp.full((nb, blk), NEG_INF, F32)
        for _ in range(MOBA_TOPK):
            mx = jnp.max(sc, axis=0, keepdims=True)
            cand = jnp.where(sc == mx, blk_i, nb)
            cand = jnp.where(mx > NEG_INF, cand, nb)
            idx = jnp.min(cand, axis=0, keepdims=True)
            pick = blk_i == idx
            seladd = jnp.where(pick, 0.0, seladd)
            sc = jnp.where(pick, NEG_INF, sc)
        sel_ref[hh] = seladd

    def scores(kb_row0, hh):
        return lax.dot_general(k_ref[pl.ds(kb_row0, blk), hh * dh:(hh + 1) * dh], q2[hh], nt,
                               preferred_element_type=F32)

    def weighted_values(kb, hh, p):
        return jnp.dot(vt_ref[kb, hh], p.astype(BF16), preferred_element_type=F32)

    for hh in heads:
        s = scores(row0, hh) * ln2 + bown_ref[hh]
        s = jnp.where(key_i <= qry_i, s, NEG_INF)
        m0 = jnp.max(s, axis=0, keepdims=True)
        m_ref[hh] = m0
        acc_ref[hh] = weighted_values(j, hh, jnp.exp(s - m0))

    def merge(hh, kb, m_blk, acc_blk):
        m_blk = m_blk + sel_ref[hh, pl.ds(kb, 1), :]
        m_old = m_ref[hh]
        m_new = jnp.maximum(m_old, m_blk)
        m_ref[hh] = m_new
        acc_ref[hh] = jnp.exp(m_old - m_new) * acc_ref[hh] + jnp.exp(m_blk - m_new) * acc_blk

    def far_block(kb, carry):
        r0 = pl.multiple_of(kb * blk, blk)
        s2 = [scores(r0, hh) for hh in heads]
        m2 = [jnp.max(s2[hh], axis=0, keepdims=True) for hh in heads]
        acc_blk = [weighted_values(kb, hh, jnp.exp2(s2[hh] - m2[hh])) for hh in heads]
        for hh in heads:
            far_bias = relb_ref[g * MOBA_HEAD_GROUP + hh, REL_BUCKETS - 1]
            merge(hh, kb, m2[hh] * ln2 + far_bias, acc_blk[hh])
        return carry

    lax.fori_loop(0, j - 1, far_block, 0)
    kb = jnp.maximum(j - 1, 0)
    r0 = pl.multiple_of(kb * blk, blk)
    for hh in heads:
        s = scores(r0, hh) * ln2 + bprev_ref[hh]
        m_blk = jnp.max(s, axis=0, keepdims=True)
        merge(hh, kb, m_blk, weighted_values(kb, hh, jnp.exp(s - m_blk)))
        acc = acc_ref[hh]
        o_ref[:, hh * dh:(hh + 1) * dh] = (acc[0:dh] / acc[dh:dh + 1]).T.astype(o_ref.dtype)


def _moba(proj, rel_bias):
    s = proj.shape[0]
    nb = s // MOBA_BLOCK
    hg = MOBA_HEAD_GROUP
    gw = hg * ATT_HEAD_DIM
    n_groups = ATT_HEADS // hg
    dhp = ATT_HEAD_DIM + MOBA_ONES_ROWS
    v_t = proj[:, 2 * BRANCH_WIDTH:3 * BRANCH_WIDTH].reshape(nb, MOBA_BLOCK, ATT_HEADS, ATT_HEAD_DIM)
    v_t = jnp.concatenate([v_t.transpose(0, 2, 3, 1),
                           jnp.ones((nb, ATT_HEADS, MOBA_ONES_ROWS, MOBA_BLOCK), BF16)], axis=2)
    return pl.pallas_call(
        functools.partial(_moba_kernel, nb=nb),
        grid=(n_groups, nb),
        in_specs=[pl.BlockSpec(memory_space=pltpu.SMEM),
                  pl.BlockSpec((MOBA_BLOCK, gw), lambda g, j: (j, g)),
                  pl.BlockSpec((s, gw), lambda g, j: (0, n_groups + g)),
                  pl.BlockSpec((nb, hg, dhp, MOBA_BLOCK), lambda g, j: (0, g, 0, 0))],
        out_specs=pl.BlockSpec((MOBA_BLOCK, gw), lambda g, j: (j, g)),
        out_shape=jax.ShapeDtypeStruct((s, BRANCH_WIDTH), BF16),
        scratch_shapes=[pltpu.VMEM((hg, nb, ATT_HEAD_DIM), F32),
                        pltpu.VMEM((hg, MOBA_BLOCK, MOBA_BLOCK), F32),
                        pltpu.VMEM((hg, MOBA_BLOCK, MOBA_BLOCK), F32),
                        pltpu.VMEM((hg, nb, MOBA_BLOCK), F32),
                        pltpu.VMEM((hg, 1, MOBA_BLOCK), F32),
                        pltpu.VMEM((hg, dhp, MOBA_BLOCK), F32)],
        compiler_params=_cparams(("arbitrary", "arbitrary")),
        name="moba",
    )(rel_bias.T, proj, proj, v_t)


def _s5_tables(lam_re, lam_im, log_dt, b_re, b_im, c_re, c_im):
    t_len = S5_CHUNK
    g_cnt, p_cnt, h_cnt = S5_GROUPS, S5_STATE, S5_GROUP
    tg = S5_TILE_GROUPS
    nt = g_cnt // tg
    dt = jnp.exp(log_dt)[:, None]
    ar, ai = lam_re * dt, lam_im * dt

    def lam_pow(steps):
        st = steps.astype(F32)[:, None, None]
        mag = jnp.exp(st * ar)
        return mag * jnp.cos(st * ai), mag * jnp.sin(st * ai)

    pr, pi = lam_pow(jnp.arange(t_len + 1))
    qr, qi = lam_pow((t_len - 1) - jnp.arange(t_len))
    nr, ni = pr[1] - 1.0, pi[1]
    den = lam_re * lam_re + lam_im * lam_im
    rr, ri = (nr * lam_re + ni * lam_im) / den, (ni * lam_re - nr * lam_im) / den
    bbr = rr[..., None] * b_re - ri[..., None] * b_im
    bbi = rr[..., None] * b_im + ri[..., None] * b_re

    def c_times(xr, xi):
        return (c_re[None] * xr[:, :, None, :] - c_im[None] * xi[:, :, None, :],
                c_re[None] * xi[:, :, None, :] + c_im[None] * xr[:, :, None, :])

    cpr, cpi = c_times(pr, pi)
    cqr, cqi = c_times(qr, qi)
    kc = (jnp.einsum('tghp,gpk->tgkh', cqr, bbr, precision=HIGHEST)
          - jnp.einsum('tghp,gpk->tgkh', cqi, bbi, precision=HIGHEST))
    kd = kc.reshape(t_len, nt, tg, h_cnt, h_cnt).transpose(1, 0, 3, 2, 4).reshape(nt, t_len, h_cnt, LANES)
    qr4, qi4 = qr[:, :, :, None], qi[:, :, :, None]
    bzr = qr4 * bbr[None] - qi4 * bbi[None]
    bzi = qr4 * bbi[None] + qi4 * bbr[None]
    bz = jnp.stack([bzr, bzi], axis=0).reshape(2, t_len, nt, tg, p_cnt, h_cnt)
    bz = bz.transpose(2, 1, 5, 0, 3, 4).reshape(nt, t_len, h_cnt, 2 * tg * p_cnt)
    cm = jnp.stack([cpr[1:], -cpi[1:]], axis=0).reshape(2, t_len, nt, tg, h_cnt, p_cnt)
    cm = cm.transpose(2, 1, 0, 5, 3, 4).reshape(nt, t_len, 2 * p_cnt, LANES)
    a_re = pr[t_len].reshape(nt, 1, tg * p_cnt)
    a_im = pi[t_len].reshape(nt, 1, tg * p_cnt)
    return kd.astype(BF16), bz.astype(BF16), cm.astype(BF16), a_re, a_im


def _s5_kernel(u_ref, kd_ref, bz_ref, cm_ref, are_ref, aim_ref, d_ref, y_ref,
               uf_ref, ucat_ref, z_ref, hc_ref, kdf_ref, bzf_ref, cmf_ref, *, nc):
    t_len = S5_CHUNK
    ns = S5_TILE_STATE
    tg, hs, ps = S5_TILE_GROUPS, S5_GROUP, S5_STATE
    chan_grp = lax.broadcasted_iota(jnp.int32, (LANES, LANES), 0) // hs
    lane_grp = lax.broadcasted_iota(jnp.int32, (LANES, LANES), 1) // hs
    state_grp = (lax.broadcasted_iota(jnp.int32, (LANES, 2 * ns), 1) % ns) // ps
    chan_grp_w = lax.broadcasted_iota(jnp.int32, (LANES, 2 * ns), 0) // hs
    lane_grp_p = lax.broadcasted_iota(jnp.int32, (ps, LANES), 1) // hs
    zero = jnp.zeros((), BF16)
    for s in range(t_len):
        rows = slice(s * LANES, (s + 1) * LANES)
        kdf_ref[rows, :] = jnp.where(chan_grp == lane_grp, jnp.concatenate([kd_ref[0, 0, s]] * tg, axis=0), zero)
        bzf_ref[rows, :] = jnp.where(chan_grp_w == state_grp, jnp.concatenate([bz_ref[0, 0, s]] * tg, axis=0), zero)
        for x in range(2):
            piece = cm_ref[0, 0, s, x * ps:(x + 1) * ps, :]
            for g in range(tg):
                cmf_ref[s, x * ns + g * ps:x * ns + (g + 1) * ps, :] = jnp.where(lane_grp_p == g, piece, zero)
    uf_ref[...] = u_ref[...].astype(F32)
    for s in range(t_len):
        ucat_ref[:, s * LANES:(s + 1) * LANES] = uf_ref[pl.ds(s, nc, stride=t_len), :].astype(BF16)
    z_ref[...] = jnp.dot(ucat_ref[...], bzf_ref[...], preferred_element_type=F32)
    a_re = are_ref[0, 0]
    a_im = aim_ref[0, 0]

    def step(c, carry):
        h_re, h_im = carry
        hc_ref[pl.ds(c, 1), 0:ns] = h_re
        hc_ref[pl.ds(c, 1), ns:2 * ns] = h_im
        z_re = z_ref[pl.ds(c, 1), 0:ns]
        z_im = z_ref[pl.ds(c, 1), ns:2 * ns]
        return (a_re * h_re - a_im * h_im + z_re, a_re * h_im + a_im * h_re + z_im)

    zero_row = jnp.zeros((1, ns), F32)
    lax.fori_loop(0, nc, step, (zero_row, zero_row))
    hc = hc_ref[...].astype(BF16)
    d_skip = d_ref[0]
    for t in range(t_len):
        acc = jnp.dot(hc, cmf_ref[t], preferred_element_type=F32)
        acc += jnp.dot(ucat_ref[:, 0:(t + 1) * LANES],
                       kdf_ref[(t_len - 1 - t) * LANES:t_len * LANES, :],
                       preferred_element_type=F32)
        acc += d_skip * uf_ref[pl.ds(t, nc, stride=t_len), :]
        y_ref[pl.ds(t, nc, stride=t_len), :] = jax.nn.gelu(acc)


def _s5_glu_kernel(y_ref, yn_ref, w_ref, b_ref, o_ref, wb_ref):
    @pl.when(pl.program_id(1) == 0)
    def _():
        wb_ref[...] = w_ref[0].astype(BF16)

    z = jnp.dot(y_ref[...].astype(BF16), wb_ref[...], preferred_element_type=F32) + b_ref[0]
    o_ref[...] = (yn_ref[...] * jax.nn.sigmoid(z)).astype(o_ref.dtype)


def _s5(proj, tables, d_skip, w_glu, b_glu, layer):
    s = proj.shape[0]
    nc = s // S5_CHUNK
    nt = S5_GROUPS // S5_TILE_GROUPS
    kd, bz, cm, a_re, a_im = tables
    u_col = 3 * BRANCH_WIDTH // LANES
    ns2 = 2 * S5_TILE_STATE
    y = pl.pallas_call(
        functools.partial(_s5_kernel, nc=nc),
        grid=(nt,),
        in_specs=[pl.BlockSpec((s, LANES), lambda c: (0, u_col + c)),
                  pl.BlockSpec((1, 1, S5_CHUNK, S5_GROUP, LANES), lambda c: (layer, c, 0, 0, 0)),
                  pl.BlockSpec((1, 1, S5_CHUNK, S5_GROUP, ns2), lambda c: (layer, c, 0, 0, 0)),
                  pl.BlockSpec((1, 1, S5_CHUNK, 2 * S5_STATE, LANES), lambda c: (layer, c, 0, 0, 0)),
                  pl.BlockSpec((1, 1, 1, S5_TILE_STATE), lambda c: (layer, c, 0, 0)),
                  pl.BlockSpec((1, 1, 1, S5_TILE_STATE), lambda c: (layer, c, 0, 0)),
                  pl.BlockSpec((1, 1, LANES), lambda c: (layer, 0, c))],
        out_specs=pl.BlockSpec((s, LANES), lambda c: (0, c)),
        out_shape=jax.ShapeDtypeStruct((s, BRANCH_WIDTH), F32),
        scratch_shapes=[pltpu.VMEM((s, LANES), F32),
                        pltpu.VMEM((nc, S5_CHUNK * LANES), BF16),
                        pltpu.VMEM((nc, ns2), F32),
                        pltpu.VMEM((nc, ns2), F32),
                        pltpu.VMEM((S5_CHUNK * LANES, LANES), BF16),
                        pltpu.VMEM((S5_CHUNK * LANES, ns2), BF16),
                        pltpu.VMEM((S5_CHUNK, ns2, LANES), BF16)],
        compiler_params=_cparams(("arbitrary",)),
        name="s5_scan",
    )(proj, kd, bz, cm, a_re, a_im, d_skip.reshape(DEPTH, 1, BRANCH_WIDTH))
    tm, tn = 512, 512
    return pl.pallas_call(
        _s5_glu_kernel,
        grid=(BRANCH_WIDTH // tn, s // tm),
        in_specs=[pl.BlockSpec((tm, BRANCH_WIDTH), lambda j, i: (i, 0)),
                  pl.BlockSpec((tm, tn), lambda j, i: (i, j)),
                  pl.BlockSpec((1, BRANCH_WIDTH, tn), lambda j, i: (layer, 0, j)),
                  pl.BlockSpec((1, 1, tn), lambda j, i: (layer, 0, j))],
        out_specs=pl.BlockSpec((tm, tn), lambda j, i: (i, j)),
        out_shape=jax.ShapeDtypeStruct((s, BRANCH_WIDTH), BF16),
        scratch_shapes=[pltpu.VMEM((BRANCH_WIDTH, tn), BF16)],
        compiler_params=_cparams(("arbitrary", "arbitrary")),
        name="s5_glu",
    )(y, y, w_glu, b_glu.reshape(DEPTH, 1, BRANCH_WIDTH))


def _gla_kernel(q_ref, k_ref, v_ref, r_ref, tail_ref, wg_ref, bg_ref, ng_ref, o_ref, st_ref):
    dk = GLA_KEY // GLA_HEADS
    dv = GLA_VAL // GLA_HEADS
    cs = GLA_CHUNK
    sub = GLA_SUB
    nt = (((1,), (1,)), ((), ()))
    tn = (((0,), (0,)), ((), ()))

    @pl.when(pl.program_id(0) == 0)
    def _():
        st_ref[...] = jnp.zeros_like(st_ref)

    gate_in = jnp.dot(tail_ref[:, 0:GLA_GATE_RANK], wg_ref[0].astype(BF16),
                      preferred_element_type=F32) + bg_ref[0]
    log_a = jax.nn.log_sigmoid(gate_in) / GLA_GATE_TAU
    ri = lax.broadcasted_iota(jnp.int32, (cs, cs), 0)
    ci = lax.broadcasted_iota(jnp.int32, (cs, cs), 1)
    tril = (ri >= ci).astype(F32)
    bcum_all = jnp.dot(tril, log_a, preferred_element_type=F32, precision=HIGHEST)
    sub_row = lax.broadcasted_iota(jnp.int32, (sub, 1), 0)

    for h in range(GLA_HEADS):
        q = q_ref[:, h * dk:(h + 1) * dk].astype(F32) * dk ** -0.5
        k = k_ref[:, h * dk:(h + 1) * dk].astype(F32)
        v_bf = v_ref[:, h * dv:(h + 1) * dv]
        v = v_bf.astype(F32)
        bc = bcum_all[:, h * dk:(h + 1) * dk]
        state = st_ref[h]
        o_inter = lax.dot_general((q * jnp.exp(bc)).astype(BF16), state.astype(BF16), nt,
                                  preferred_element_type=F32)
        parts = []
        for i in range(cs // sub):
            lo = i * sub
            b_i, q_i, k_i, v_i = bc[lo:lo + sub], q[lo:lo + sub], k[lo:lo + sub], v[lo:lo + sub]
            o_i = o_inter[lo:lo + sub]
            if i > 0:
                ref = bc[lo - 1:lo]
                q_h = (q_i * jnp.exp(b_i - ref)).astype(BF16)
                k_h = (k[:lo] * jnp.exp(ref - bc[:lo])).astype(BF16)
                attn = lax.dot_general(q_h, k_h, nt, preferred_element_type=F32)
                o_i = o_i + jnp.dot(attn.astype(BF16), v_bf[:lo], preferred_element_type=F32)
            for dlt in range(sub):
                if dlt == 0:
                    a = jnp.sum(q_i * k_i, axis=-1, keepdims=True)
                    o_i = o_i + a * v_i
                else:
                    b_s = pltpu.roll(b_i, dlt, 0)
                    k_s = pltpu.roll(k_i, dlt, 0)
                    v_s = pltpu.roll(v_i, dlt, 0)
                    e = jnp.exp(jnp.minimum(b_i - b_s, 0.0))
                    a = jnp.sum(q_i * k_s * e, axis=-1, keepdims=True)
                    a = jnp.where(sub_row >= dlt, a, 0.0)
                    o_i = o_i + a * v_s
            parts.append(o_i)
        o = jnp.concatenate(parts, axis=0)
        b_last = bc[cs - 1:cs]
        k_dec = (k * jnp.exp(b_last - bc)).astype(BF16)
        st_ref[h] = state * jnp.exp(b_last) + lax.dot_general(v_bf, k_dec, tn,
                                                             preferred_element_type=F32)
        o = o * lax.rsqrt(jnp.mean(o * o, axis=-1, keepdims=True) + NORM_EPS)
        o = o * ng_ref[0, :, h * dv:(h + 1) * dv]
        r = r_ref[:, h * dv:(h + 1) * dv].astype(F32)
        o_ref[:, h * dv:(h + 1) * dv] = (o * (r * jax.nn.sigmoid(r))).astype(o_ref.dtype)


def _gla(proj, w_gate, b_gate, norm_g, layer):
    s = proj.shape[0]
    cs = GLA_CHUNK
    return pl.pallas_call(
        _gla_kernel,
        grid=(s // cs,),
        in_specs=[pl.BlockSpec((cs, GLA_KEY), lambda c: (c, 4 * BRANCH_WIDTH // GLA_KEY)),
                  pl.BlockSpec((cs, GLA_KEY), lambda c: (c, 4 * BRANCH_WIDTH // GLA_KEY + 1)),
                  pl.BlockSpec((cs, GLA_VAL), lambda c: (c, 5)),
                  pl.BlockSpec((cs, GLA_VAL), lambda c: (c, 6)),
                  pl.BlockSpec((cs, TAIL_W), lambda c: (c, TAIL_COL // TAIL_W)),
                  pl.BlockSpec((1, GLA_GATE_RANK, GLA_KEY), lambda c: (layer, 0, 0)),
                  pl.BlockSpec((1, 1, GLA_KEY), lambda c: (layer, 0, 0)),
                  pl.BlockSpec((1, 1, GLA_VAL), lambda c: (layer, 0, 0))],
        out_specs=pl.BlockSpec((cs, GLA_VAL), lambda c: (c, 0)),
        out_shape=jax.ShapeDtypeStruct((s, GLA_VAL), BF16),
        scratch_shapes=[pltpu.VMEM((GLA_HEADS, GLA_VAL // GLA_HEADS, GLA_KEY // GLA_HEADS), F32)],
        compiler_params=_cparams(("arbitrary",)),
        name="gla",
    )(proj, proj, proj, proj, proj, w_gate, b_gate.reshape(DEPTH, 1, GLA_KEY),
      norm_g.reshape(DEPTH, 1, GLA_VAL))


def _merge_kernel(ya_ref, ys_ref, yg_ref, tail_ref, wb_ref, wg0_ref, wg1_ref, wg2_ref,
                  bg0_ref, bg1_ref, bg2_ref, o_ref, wbb_ref, wgb_ref):
    wg_refs = (wg0_ref, wg1_ref, wg2_ref)
    bg_refs = (bg0_ref, bg1_ref, bg2_ref)

    @pl.when(pl.program_id(1) == 0)
    def _():
        wbb_ref[...] = wb_ref[0].astype(BF16)
        for n in range(N_BRANCH):
            wgb_ref[n] = wg_refs[n][0].astype(BF16)

    mz = tail_ref[:, GLA_GATE_RANK:GLA_GATE_RANK + MERGE_RANK]
    acc = None
    for n, y_ref in enumerate((ya_ref, ys_ref, yg_ref)):
        up = jnp.dot(y_ref[...], wbb_ref[n], preferred_element_type=F32)
        gate = jax.nn.sigmoid(jnp.dot(mz, wgb_ref[n], preferred_element_type=F32) + bg_refs[n][0])
        acc = gate * up if acc is None else acc + gate * up
    o_ref[...] = acc.astype(o_ref.dtype)


def _merge(y_att, y_s5, y_gla, proj, w_branch, w_merge_gate, b_merge_gate, layer, tm=512, tn=512):
    s = proj.shape[0]
    n_col = D_MODEL // tn
    bg = b_merge_gate.reshape(DEPTH, 1, N_BRANCH * D_MODEL)
    ybs = pl.BlockSpec((tm, BRANCH_WIDTH), lambda j, i: (i, 0))

    def gate_col(n):
        return lambda j, i: (layer, 0, n * n_col + j)

    return pl.pallas_call(
        _merge_kernel,
        grid=(n_col, s // tm),
        in_specs=[ybs, ybs, ybs,
                  pl.BlockSpec((tm, TAIL_W), lambda j, i: (i, TAIL_COL // TAIL_W)),
                  pl.BlockSpec((1, N_BRANCH, BRANCH_WIDTH, tn), lambda j, i: (layer, 0, 0, j))]
                 + [pl.BlockSpec((1, MERGE_RANK, tn), gate_col(n)) for n in range(N_BRANCH)]
                 + [pl.BlockSpec((1, 1, tn), gate_col(n)) for n in range(N_BRANCH)],
        out_specs=pl.BlockSpec((tm, tn), lambda j, i: (i, j)),
        out_shape=jax.ShapeDtypeStruct((s, D_MODEL), BF16),
        scratch_shapes=[pltpu.VMEM((N_BRANCH, BRANCH_WIDTH, tn), BF16),
                        pltpu.VMEM((N_BRANCH, MERGE_RANK, tn), BF16)],
        compiler_params=_cparams(("arbitrary", "arbitrary")),
        name="merge",
    )(y_att, y_s5, y_gla, proj, w_branch, w_merge_gate, w_merge_gate, w_merge_gate, bg, bg, bg)


def _swiglu_kernel(a_ref, w1_ref, w3_ref, o_ref, w1b_ref, w3b_ref):
    @pl.when(pl.program_id(1) == 0)
    def _():
        w1b_ref[...] = w1_ref[0].astype(BF16)
        w3b_ref[...] = w3_ref[0].astype(BF16)

    a = a_ref[...]
    g = jnp.dot(a, w1b_ref[...], preferred_element_type=F32)
    u = jnp.dot(a, w3b_ref[...], preferred_element_type=F32)
    o_ref[...] = (g * jax.nn.sigmoid(g) * u).astype(o_ref.dtype)


def _swiglu_hidden(a, w1_stack, w3_stack, layer, tm=1024, tn=256):
    m, k = a.shape
    n = w1_stack.shape[2]
    wspec = pl.BlockSpec((1, k, tn), lambda j, i: (layer, 0, j))
    return pl.pallas_call(
        _swiglu_kernel,
        grid=(n // tn, m // tm),
        in_specs=[pl.BlockSpec((tm, k), lambda j, i: (i, 0)), wspec, wspec],
        out_specs=pl.BlockSpec((tm, tn), lambda j, i: (i, j)),
        out_shape=jax.ShapeDtypeStruct((m, n), BF16),
        scratch_shapes=[pltpu.VMEM((k, tn), BF16), pltpu.VMEM((k, tn), BF16)],
        compiler_params=_cparams(("arbitrary", "arbitrary")),
        name="swiglu_hidden",
    )(a, w1_stack, w3_stack)


def _moe_plan(route):
    s = route.shape[0]
    tile = MOE_TILE
    n_tiles = (TOP_K * s) // tile + N_EXPERTS
    ids = route[:, ROUTE_ID:ROUTE_ID + TOP_K].astype(jnp.int32)
    onehot = jnp.sum(jax.nn.one_hot(ids, N_EXPERTS, dtype=jnp.int32), axis=1)
    before = jnp.cumsum(onehot, axis=0) - onehot
    counts = jnp.sum(onehot, axis=0)
    padded = (counts + tile - 1) // tile * tile
    ends = jnp.cumsum(padded)
    offsets = ends - padded
    pos = offsets[ids] + jnp.take_along_axis(before, ids, axis=1)
    tile_start = jnp.arange(n_tiles, dtype=jnp.int32) * tile
    tile_expert = jnp.minimum(jnp.sum(tile_start[:, None] >= ends[None, :], axis=1), N_EXPERTS - 1)
    n_active = (ends[-1] // tile).reshape(1)
    tokens = jnp.broadcast_to(jnp.arange(s, dtype=jnp.int32)[:, None], pos.shape)
    row_token = jnp.zeros((n_tiles * tile,), jnp.int32).at[pos.reshape(-1)].set(tokens.reshape(-1))
    pos1 = pos[:, 0].reshape(s // LANES, LANES)
    pos2 = pos[:, 1].reshape(s // LANES, LANES)
    return (pos1, pos2, row_token.reshape(-1, LANES), tile_expert.astype(jnp.int32),
            n_active.astype(jnp.int32), n_tiles)


def _moe_dispatch_kernel(rt_ref, na_ref, h_ref, o_ref, buf_ref, sem):
    i = pl.program_id(0)

    @pl.when(i < na_ref[0])
    def _():
        _gather_rows(lambda r: (_row_copy(h_ref, buf_ref, sem, _smem_at(rt_ref, i * MOE_TILE + r), r),), MOE_TILE)
        o_ref[...] = buf_ref[...].astype(o_ref.dtype)

    @pl.when(i >= na_ref[0])
    def _():
        o_ref[...] = jnp.zeros_like(o_ref)


def _moe_dispatch(h, row_token, n_active, n_tiles):
    tile = MOE_TILE
    return pl.pallas_call(
        _moe_dispatch_kernel,
        grid_spec=pltpu.PrefetchScalarGridSpec(
            num_scalar_prefetch=2, grid=(n_tiles,),
            in_specs=[pl.BlockSpec(memory_space=pl.ANY)],
            out_specs=pl.BlockSpec((tile, D_MODEL), lambda i, rt, na: (i, 0)),
            scratch_shapes=[pltpu.VMEM((tile, D_MODEL), F32), pltpu.SemaphoreType.DMA(())]),
        out_shape=jax.ShapeDtypeStruct((n_tiles * tile, D_MODEL), BF16),
        compiler_params=_cparams(("arbitrary",)),
        name="moe_dispatch",
    )(row_token, n_active, h)


def _new_expert_panel(te_ref):
    i = pl.program_id(1)
    return jnp.logical_or(i == 0, te_ref[i] != te_ref[jnp.maximum(i - 1, 0)])


def _moe_hidden_kernel(te_ref, na_ref, a_ref, w1_ref, w3_ref, o_ref, w1b_ref, w3b_ref):
    @pl.when(_new_expert_panel(te_ref))
    def _():
        w1b_ref[...] = w1_ref[0, 0].astype(BF16)
        w3b_ref[...] = w3_ref[0, 0].astype(BF16)

    @pl.when(pl.program_id(1) < na_ref[0])
    def _():
        a = a_ref[...]
        g = jnp.dot(a, w1b_ref[...], preferred_element_type=F32)
        u = jnp.dot(a, w3b_ref[...], preferred_element_type=F32)
        o_ref[...] = (g * jax.nn.sigmoid(g) * u).astype(o_ref.dtype)

    @pl.when(pl.program_id(1) >= na_ref[0])
    def _():
        o_ref[...] = jnp.zeros_like(o_ref)


def _moe_down_kernel(te_ref, na_ref, a_ref, w_ref, o_ref, wb_ref):
    @pl.when(_new_expert_panel(te_ref))
    def _():
        wb_ref[...] = w_ref[0, 0].astype(BF16)

    @pl.when(pl.program_id(1) < na_ref[0])
    def _():
        o_ref[...] = jnp.dot(a_ref[...], wb_ref[...], preferred_element_type=F32).astype(o_ref.dtype)

    @pl.when(pl.program_id(1) >= na_ref[0])
    def _():
        o_ref[...] = jnp.zeros_like(o_ref)


def _moe_experts(xs, tile_expert, n_active, w1_stack, w3_stack, w2_stack, layer, n_tiles,
                 tn_hidden=256, tn_down=1024):
    tile = MOE_TILE
    rows = n_tiles * tile
    w_in_spec = pl.BlockSpec((1, 1, D_MODEL, tn_hidden), lambda n, i, te, na: (layer, te[i], 0, n))
    hid = pl.pallas_call(
        _moe_hidden_kernel,
        grid_spec=pltpu.PrefetchScalarGridSpec(
            num_scalar_prefetch=2, grid=(D_FF_EXPERT // tn_hidden, n_tiles),
            in_specs=[pl.BlockSpec((tile, D_MODEL), lambda n, i, te, na: (i, 0)), w_in_spec, w_in_spec],
            out_specs=pl.BlockSpec((tile, tn_hidden), lambda n, i, te, na: (i, n)),
            scratch_shapes=[pltpu.VMEM((D_MODEL, tn_hidden), BF16), pltpu.VMEM((D_MODEL, tn_hidden), BF16)]),
        out_shape=jax.ShapeDtypeStruct((rows, D_FF_EXPERT), BF16),
        compiler_params=_cparams(("arbitrary", "arbitrary")),
        name="moe_hidden",
    )(tile_expert, n_active, xs, w1_stack, w3_stack)
    return pl.pallas_call(
        _moe_down_kernel,
        grid_spec=pltpu.PrefetchScalarGridSpec(
            num_scalar_prefetch=2, grid=(D_MODEL // tn_down, n_tiles),
            in_specs=[pl.BlockSpec((tile, D_FF_EXPERT), lambda n, i, te, na: (i, 0)),
                      pl.BlockSpec((1, 1, D_FF_EXPERT, tn_down), lambda n, i, te, na: (layer, te[i], 0, n))],
            out_specs=pl.BlockSpec((tile, tn_down), lambda n, i, te, na: (i, n)),
            scratch_shapes=[pltpu.VMEM((D_FF_EXPERT, tn_down), BF16)]),
        out_shape=jax.ShapeDtypeStruct((rows, D_MODEL), F32),
        compiler_params=_cparams(("arbitrary", "arbitrary")),
        name="moe_down",
    )(tile_expert, n_active, hid, w2_stack)


def kernel(x, c, w_cond, b_cond, rel_bias, w_mod, b_mod, w_in, s5_lambda_re, s5_lambda_im, s5_log_dt, s5_b_re, s5_b_im, s5_c_re, s5_c_im, s5_d, s5_w_glu, s5_b_glu, gla_w_gate, gla_b_gate, gla_norm_g, w_branch, w_merge_gate, b_merge_gate, w_out, ln1_g, ln1_b, ffn_w1, ffn_w3, ffn_w2, router_w, router_b, exp_w1, exp_w3, exp_w2, ln2_g, ln2_b):
    bsz, seq, _ = x.shape
    assert bsz == 1
    mod = _conditioning(c, w_cond, b_cond, w_mod, b_mod)
    xs = x.reshape(seq, D_MODEL)
    w_in_t = jnp.swapaxes(w_in, 1, 2)
    hm = _modulate(xs, mod[0, 1], mod[0, 0])
    s5_tables = jax.vmap(_s5_tables)(s5_lambda_re, s5_lambda_im, s5_log_dt, s5_b_re, s5_b_im, s5_c_re, s5_c_im)
    for l in range(DEPTH):
        shift_f, scale_f, gate_m, gate_f = mod[l, 3], mod[l, 4], mod[l, 2], mod[l, 5]
        proj = _matmul(hm, w_in_t, l, tm=512, tn=512, n_out=D_IN_PAD, w_transposed=True,
                       name="in_proj")
        y_att = _moba(proj, rel_bias)
        y_s5 = _s5(proj, s5_tables, s5_d, s5_w_glu, s5_b_glu, l)
        y_gla = _gla(proj, gla_w_gate, gla_b_gate, gla_norm_g, l)
        merged = _merge(y_att, y_s5, y_gla, proj, w_branch, w_merge_gate, b_merge_gate, l)
        y = _matmul(merged, w_out, l, tm=512, tn=512, name="out_proj")
        dense = l % 2 == 0
        router = None if dense else (router_w[l // 2], router_b[l // 2])
        outs = _deepnorm_ln(xs, y, gate_m, ln1_g[l], ln1_b[l], nxt=(scale_f, shift_f), router=router)
        xs, hf = outs[0], outs[1]
        if dense:
            hid = _swiglu_hidden(hf, ffn_w1, ffn_w3, l // 2)
            f = _matmul(hid, ffn_w2, l // 2, tm=256, tn=512, name="ffn_down")
            moe = None
        else:
            route = outs[2]
            pos1, pos2, row_token, tile_expert, n_active, n_tiles = _moe_plan(route)
            xsorted = _moe_dispatch(hf, row_token, n_active, n_tiles)
            ys = _moe_experts(xsorted, tile_expert, n_active, exp_w1, exp_w3, exp_w2, l // 2, n_tiles)
            f = None
            moe = (ys, pos1, pos2, route)
        nxt = (mod[l + 1, 1], mod[l + 1, 0]) if l + 1 < DEPTH else None
        outs = _deepnorm_ln(xs, f, gate_f, ln2_g[l], ln2_b[l], nxt=nxt, moe=moe)
        xs = outs[0]
        if nxt is not None:
            hm = outs[1]
    return xs.reshape(bsz, seq, D_MODEL)
```

```python
import functools
import math

import jax
import jax.numpy as jnp
from jax import lax
from jax.experimental import pallas as pl
from jax.experimental.pallas import tpu as pltpu

F32 = jnp.float32
BF16 = jnp.bfloat16
HIGHEST = lax.Precision.HIGHEST

D_MODEL = 4096
DEPTH = 4
BRANCH_WIDTH = 1024
N_BRANCH = 3
ATT_HEADS = 8
ATT_HEAD_DIM = 128
MOBA_BLOCK = 256
MOBA_TOPK = 3
MOBA_HEAD_GROUP = 4
MOBA_ONES_ROWS = 16
REL_BUCKETS = 32
REL_MAX_DIST = 128
S5_GROUP = 16
S5_GROUPS = 64
S5_STATE = 64
S5_CHUNK = 16
S5_TILE_GROUPS = 8
S5_TILE_STATE = S5_TILE_GROUPS * S5_STATE
GLA_HEADS = 4
GLA_KEY = 512
GLA_VAL = 1024
GLA_GATE_RANK = 16
GLA_GATE_TAU = 16.0
GLA_CHUNK = 64
GLA_SUB = 16
MERGE_RANK = 256
COND_RANK = 512
D_FF = 8192
N_EXPERTS = 8
TOP_K = 2
ROUTE_ID = 8
ROUTE_W = 10
MOE_TILE = 256
D_FF_EXPERT = 1792
DN_ALPHA = (2 * DEPTH) ** 0.25
LN_EPS = 1e-5
NORM_EPS = 1e-6

D_IN = 7440
D_IN_PAD = 7680
TAIL_COL = 7168
TAIL_W = 512
LANES = 128
VMEM_LIMIT = 56 * 1024 * 1024
NEG_INF = float("-inf")


def _cparams(sem):
    return pltpu.CompilerParams(dimension_semantics=sem, vmem_limit_bytes=VMEM_LIMIT)


def _cond_kernel(c_ref, w_ref, b_ref, o_ref):
    z = jnp.dot(c_ref[...], w_ref[...], preferred_element_type=F32, precision=HIGHEST) + b_ref[...]
    o_ref[...] = z * jax.nn.sigmoid(z)


def _mod_kernel(cond_ref, w_ref, b_ref, o_ref):
    o_ref[0] = jnp.dot(cond_ref[...], w_ref[0], preferred_element_type=F32,
                       precision=HIGHEST) + b_ref[0]


def _conditioning(c, w_cond, b_cond, w_mod, b_mod):
    c8 = jnp.broadcast_to(c, (8, D_MODEL))
    cond = pl.pallas_call(
        _cond_kernel,
        out_shape=jax.ShapeDtypeStruct((8, COND_RANK), F32),
        compiler_params=_cparams(None),
        name="cond",
    )(c8, w_cond, b_cond.reshape(1, COND_RANK))
    n_mod = 6 * D_MODEL
    tn = 3072
    mod = pl.pallas_call(
        _mod_kernel,
        grid=(DEPTH, n_mod // tn),
        in_specs=[pl.BlockSpec((8, COND_RANK), lambda l, n: (0, 0)),
                  pl.BlockSpec((1, COND_RANK, tn), lambda l, n: (l, 0, n)),
                  pl.BlockSpec((1, 1, tn), lambda l, n: (l, 0, n))],
        out_specs=pl.BlockSpec((1, 8, tn), lambda l, n: (l, 0, n)),
        out_shape=jax.ShapeDtypeStruct((DEPTH, 8, n_mod), F32),
        compiler_params=_cparams(("arbitrary", "arbitrary")),
        name="mod",
    )(cond, w_mod, b_mod.reshape(DEPTH, 1, n_mod))
    return mod[:, 0, :].reshape(DEPTH, 6, 1, D_MODEL)


def _modulate_kernel(x_ref, scale_ref, shift_ref, o_ref):
    o_ref[...] = (x_ref[...] * (1.0 + scale_ref[...]) + shift_ref[...]).astype(o_ref.dtype)


def _modulate(x, scale, shift, tm=512):
    s = x.shape[0]
    vec = pl.BlockSpec((1, D_MODEL), lambda m: (0, 0))
    return pl.pallas_call(
        _modulate_kernel,
        grid=(s // tm,),
        in_specs=[pl.BlockSpec((tm, D_MODEL), lambda m: (m, 0)), vec, vec],
        out_specs=pl.BlockSpec((tm, D_MODEL), lambda m: (m, 0)),
        out_shape=jax.ShapeDtypeStruct((s, D_MODEL), BF16),
        compiler_params=_cparams(("arbitrary",)),
        name="modulate",
    )(x, scale, shift)


def _route_top2(logits):
    lane = lax.broadcasted_iota(jnp.int32, logits.shape, 1)
    m1 = jnp.max(logits, axis=-1, keepdims=True)
    i1 = jnp.min(jnp.where(logits == m1, lane, LANES), axis=-1, keepdims=True)
    rest = jnp.where(lane == i1, NEG_INF, logits)
    m2 = jnp.max(rest, axis=-1, keepdims=True)
    i2 = jnp.min(jnp.where(rest == m2, lane, LANES), axis=-1, keepdims=True)
    e2 = jnp.exp(m2 - m1)
    denom = 1.0 + e2
    rec = jnp.where(lane == ROUTE_ID, i1.astype(F32), 0.0)
    rec = jnp.where(lane == ROUTE_ID + 1, i2.astype(F32), rec)
    rec = jnp.where(lane == ROUTE_W, 1.0 / denom, rec)
    return jnp.where(lane == ROUTE_W + 1, e2 / denom, rec)


def _row_copy(src_ref, dst_ref, sem, src_row, dst_row):
    return pltpu.make_async_copy(src_ref.at[pl.ds(src_row, 1)], dst_ref.at[pl.ds(dst_row, 1)], sem)


def _smem_at(ref, i):
    return ref[i // LANES, i % LANES]


def _gather_rows(copies_of, n_rows):
    def start(r, carry):
        for cp in copies_of(r):
            cp.start()
        return carry

    def wait(r, carry):
        for cp in copies_of(r):
            cp.wait()
        return carry

    lax.fori_loop(0, n_rows, start, 0)
    lax.fori_loop(0, n_rows, wait, 0)


def _ln_kernel(*refs, has_next, has_router, moe_combine, tm):
    pos = 0
    if moe_combine:
        pos1_ref, pos2_ref, x_ref, route_ref, ys_ref = refs[:5]
        pos = 5
    else:
        x_ref, y_ref = refs[:2]
        pos = 2
    gate_ref, g_ref, b_ref = refs[pos:pos + 3]
    pos += 3
    if has_next:
        scale_ref, shift_ref = refs[pos:pos + 2]
        pos += 2
    if has_router:
        rw_ref, rb_ref = refs[pos:pos + 2]
        pos += 2
    xo_ref = refs[pos]
    pos += 1
    if moe_combine:
        buf1_ref, buf2_ref, sem = refs[-3:]
        base = pl.program_id(0) * tm

        _gather_rows(lambda r: (_row_copy(ys_ref, buf1_ref, sem, _smem_at(pos1_ref, base + r), r),
                                _row_copy(ys_ref, buf2_ref, sem, _smem_at(pos2_ref, base + r), r)), tm)
        route = route_ref[...]
        lane = lax.broadcasted_iota(jnp.int32, route.shape, 1)
        w1 = jnp.sum(jnp.where(lane == ROUTE_W, route, 0.0), axis=-1, keepdims=True)
        w2 = jnp.sum(jnp.where(lane == ROUTE_W + 1, route, 0.0), axis=-1, keepdims=True)
        y = w1 * buf1_ref[...] + w2 * buf2_ref[...]
    else:
        y = y_ref[...].astype(F32)
    z = DN_ALPHA * x_ref[...] + (1.0 + gate_ref[...]) * y
    mu = jnp.mean(z, axis=-1, keepdims=True)
    zc = z - mu
    var = jnp.mean(zc * zc, axis=-1, keepdims=True)
    xn = zc * lax.rsqrt(var + LN_EPS) * g_ref[...] + b_ref[...]
    xo_ref[...] = xn
    if has_next:
        ho_ref = refs[pos]
        pos += 1
        h = xn * (1.0 + scale_ref[...]) + shift_ref[...]
        ho_ref[...] = h.astype(ho_ref.dtype)
        if has_router:
            co_ref = refs[pos]
            logits = jnp.dot(h, rw_ref[...], preferred_element_type=F32, precision=HIGHEST)
            lane = lax.broadcasted_iota(jnp.int32, logits.shape, 1)
            logits = jnp.where(lane < N_EXPERTS, logits + rb_ref[...], NEG_INF)
            co_ref[...] = _route_top2(logits)


def _deepnorm_ln(x, y, gate, g, b, nxt=None, router=None, moe=None, tm=256):
    s = x.shape[0]
    n_pre = 0 if moe is None else 2
    imap = (lambda m: (m, 0)) if moe is None else (lambda m, p1, p2: (m, 0))
    vmap = (lambda m: (0, 0)) if moe is None else (lambda m, p1, p2: (0, 0))
    row = pl.BlockSpec((tm, D_MODEL), imap)
    vec = pl.BlockSpec((1, D_MODEL), vmap)
    scratch = []
    if moe is None:
        args = [x, y]
        in_specs = [row, row]
    else:
        ys, pos1, pos2, route = moe
        args = [pos1, pos2, x, route, ys]
        in_specs = [row, pl.BlockSpec((tm, LANES), imap), pl.BlockSpec(memory_space=pl.ANY)]
        scratch = [pltpu.VMEM((tm, D_MODEL), F32), pltpu.VMEM((tm, D_MODEL), F32),
                   pltpu.SemaphoreType.DMA(())]
    args += [gate, g.reshape(1, D_MODEL), b.reshape(1, D_MODEL)]
    in_specs += [vec, vec, vec]
    out_shape = [jax.ShapeDtypeStruct((s, D_MODEL), F32)]
    out_specs = [row]
    if nxt is not None:
        args += [nxt[0], nxt[1]]
        in_specs += [vec, vec]
        out_shape.append(jax.ShapeDtypeStruct((s, D_MODEL), BF16 if router is None else F32))
        out_specs.append(row)
    if router is not None:
        rw, rb = router
        rw_pad = jnp.pad(rw, ((0, 0), (0, LANES - N_EXPERTS)))
        rb_pad = jnp.pad(rb, (0, LANES - N_EXPERTS)).reshape(1, LANES)
        args += [rw_pad, rb_pad]
        in_specs += [pl.BlockSpec((D_MODEL, LANES), vmap), pl.BlockSpec((1, LANES), vmap)]
        out_shape.append(jax.ShapeDtypeStruct((s, LANES), F32))
        out_specs.append(pl.BlockSpec((tm, LANES), imap))
    return pl.pallas_call(
        functools.partial(_ln_kernel, has_next=nxt is not None, has_router=router is not None,
                          moe_combine=moe is not None, tm=tm),
        grid_spec=pltpu.PrefetchScalarGridSpec(
            num_scalar_prefetch=n_pre, grid=(s // tm,), in_specs=in_specs, out_specs=out_specs,
            scratch_shapes=scratch),
        out_shape=out_shape,
        compiler_params=_cparams(("arbitrary",)),
        name="deepnorm_ln",
    )(*args)


def _matmul_kernel(a_ref, w_ref, o_ref, wb_ref, *, n_valid, tn, w_transposed):
    @pl.when(pl.program_id(1) == 0)
    def _():
        w = w_ref[0]
        wb_ref[...] = (w.T if w_transposed else w).astype(BF16)

    acc = jnp.dot(a_ref[...], wb_ref[...], preferred_element_type=F32)
    if n_valid is not None:
        col = pl.program_id(0) * tn + lax.broadcasted_iota(jnp.int32, acc.shape, 1)
        acc = jnp.where(col < n_valid, acc, 0.0)
    o_ref[...] = acc.astype(o_ref.dtype)


def _matmul(a, w_stack, layer, tm, tn, out_dtype=BF16, n_out=None, w_transposed=False, name="matmul"):
    m, k = a.shape
    n = w_stack.shape[1 if w_transposed else 2]
    n_out = n if n_out is None else n_out
    if w_transposed:
        w_spec = pl.BlockSpec((1, tn, k), lambda j, i: (layer, j, 0))
    else:
        w_spec = pl.BlockSpec((1, k, tn), lambda j, i: (layer, 0, j))
    return pl.pallas_call(
        functools.partial(_matmul_kernel, n_valid=None if n_out == n else n, tn=tn,
                          w_transposed=w_transposed),
        grid=(n_out // tn, m // tm),
        in_specs=[pl.BlockSpec((tm, k), lambda j, i: (i, 0)), w_spec],
        out_specs=pl.BlockSpec((tm, tn), lambda j, i: (i, j)),
        out_shape=jax.ShapeDtypeStruct((m, n_out), out_dtype),
        scratch_shapes=[pltpu.VMEM((k, tn), BF16)],
        compiler_params=_cparams(("arbitrary", "arbitrary")),
        name=name,
    )(a, w_stack)


def _rel_bucket(dist):
    n = jnp.maximum(dist, 0)
    max_exact = REL_BUCKETS // 2
    nf = jnp.maximum(n, 1).astype(F32)
    large = max_exact + (jnp.log(nf / max_exact) / math.log(REL_MAX_DIST / max_exact)
                         * (REL_BUCKETS - max_exact)).astype(jnp.int32)
    large = jnp.minimum(large, REL_BUCKETS - 1)
    return jnp.where(n < max_exact, n, large)


def _moba_kernel(relb_ref, q_ref, k_ref, vt_ref, o_ref,
                 kmean_ref, bown_ref, bprev_ref, sel_ref, m_ref, acc_ref, *, nb):
    g = pl.program_id(0)
    j = pl.program_id(1)
    blk = MOBA_BLOCK
    dh = ATT_HEAD_DIM
    scale = dh ** -0.5
    ln2 = math.log(2.0)
    nt = (((1,), (1,)), ((), ()))
    heads = range(MOBA_HEAD_GROUP)
    key_i = lax.broadcasted_iota(jnp.int32, (blk, blk), 0)
    qry_i = lax.broadcasted_iota(jnp.int32, (blk, blk), 1)

    @pl.when(j == 0)
    def _():
        for hh in heads:
            head = g * MOBA_HEAD_GROUP + hh
            kf = k_ref[:, hh * dh:(hh + 1) * dh].astype(F32).reshape(nb, blk, dh)
            kmean_ref[hh] = jnp.mean(kf, axis=1)
            for ref, off in ((bown_ref, 0), (bprev_ref, blk)):
                bucket = _rel_bucket(qry_i - key_i + off)
                bias = jnp.zeros((blk, blk), F32)
                for b in range(REL_BUCKETS):
                    bias = jnp.where(bucket == b, relb_ref[head, b], bias)
                ref[hh] = bias

    row0 = pl.multiple_of(j * blk, blk)
    blk_i = lax.broadcasted_iota(jnp.int32, (nb, blk), 0)
    q2 = []
    for hh in heads:
        q = q_ref[:, hh * dh:(hh + 1) * dh]
        q2.append((q.astype(F32) * (scale / ln2)).astype(BF16))
        score = lax.dot_general(kmean_ref[hh], q.astype(F32), nt,
                                preferred_element_type=F32, precision=HIGHEST)
        sc = jnp.where(blk_i < j, score, NEG_INF)
        seladd = jnp.full((nb, blk), NEG_INF, F32)
        for _ in range(MOBA_TOPK):
            mx = jnp.max(sc, axis=0, keepdims=True)
            cand = jnp.where(sc == mx, blk_i, nb)
            cand = jnp.where(mx > NEG_INF, cand, nb)
            idx = jnp.min(cand, axis=0, keepdims=True)
            pick = blk_i == idx
            seladd = jnp.where(pick, 0.0, seladd)
            sc = jnp.where(pick, NEG_INF, sc)
        sel_ref[hh] = seladd

    def scores(kb_row0, hh):
        return lax.dot_general(k_ref[pl.ds(kb_row0, blk), hh * dh:(hh + 1) * dh], q2[hh], nt,
                               preferred_element_type=F32)

    def weighted_values(kb, hh, p):
        return jnp.dot(vt_ref[kb, hh], p.astype(BF16), preferred_element_type=F32)

    for hh in heads:
        s = scores(row0, hh) * ln2 + bown_ref[hh]
        s = jnp.where(key_i <= qry_i, s, NEG_INF)
        m0 = jnp.max(s, axis=0, keepdims=True)
        m_ref[hh] = m0
        acc_ref[hh] = weighted_values(j, hh, jnp.exp(s - m0))

    def merge(hh, kb, m_blk, acc_blk):
        m_blk = m_blk + sel_ref[hh, pl.ds(kb, 1), :]
        m_old = m_ref[hh]
        m_new = jnp.maximum(m_old, m_blk)
        m_ref[hh] = m_new
        acc_ref[hh] = jnp.exp(m_old - m_new) * acc_ref[hh] + jnp.exp(m_blk - m_new) * acc_blk

    def far_blocks(pair, carry):
        chains = [(2 * pair + second, hh, second) for second in (0, 1) for hh in heads]
        s2 = [scores(pl.multiple_of(kb * blk, blk), hh) for kb, hh, _ in chains]
        m2 = [jnp.max(s, axis=0, keepdims=True) for s in s2]
        acc_blk = [weighted_values(kb, hh, jnp.exp2(s - m)) for (kb, hh, _), s, m in zip(chains, s2, m2)]
        second_ok = jnp.where(2 * pair + 1 < j - 1, 0.0, NEG_INF)
        for (kb, hh, second), m, acc in zip(chains, m2, acc_blk):
            m_blk = m * ln2 + relb_ref[g * MOBA_HEAD_GROUP + hh, REL_BUCKETS - 1]
            merge(hh, kb, m_blk + second_ok if second else m_blk, acc)
        return carry

    lax.fori_loop(0, j // 2, far_blocks, 0)
    kb = jnp.maximum(j - 1, 0)
    r0 = pl.multiple_of(kb * blk, blk)
    for hh in heads:
        s = scores(r0, hh) * ln2 + bprev_ref[hh]
        m_blk = jnp.max(s, axis=0, keepdims=True)
        merge(hh, kb, m_blk, weighted_values(kb, hh, jnp.exp(s - m_blk)))
        acc = acc_ref[hh]
        o_ref[:, hh * dh:(hh + 1) * dh] = (acc[0:dh] / acc[dh:dh + 1]).T.astype(o_ref.dtype)


def _moba(proj, rel_bias):
    s = proj.shape[0]
    nb = s // MOBA_BLOCK
    hg = MOBA_HEAD_GROUP
    gw = hg * ATT_HEAD_DIM
    n_groups = ATT_HEADS // hg
    dhp = ATT_HEAD_DIM + MOBA_ONES_ROWS
    v_t = proj[:, 2 * BRANCH_WIDTH:3 * BRANCH_WIDTH].reshape(nb, MOBA_BLOCK, ATT_HEADS, ATT_HEAD_DIM)
    v_t = jnp.concatenate([v_t.transpose(0, 2, 3, 1),
                           jnp.ones((nb, ATT_HEADS, MOBA_ONES_ROWS, MOBA_BLOCK), BF16)], axis=2)
    return pl.pallas_call(
        functools.partial(_moba_kernel, nb=nb),
        grid=(n_groups, nb),
        in_specs=[pl.BlockSpec(memory_space=pltpu.SMEM),
                  pl.BlockSpec((MOBA_BLOCK, gw), lambda g, j: (j, g)),
                  pl.BlockSpec((s, gw), lambda g, j: (0, n_groups + g)),
                  pl.BlockSpec((nb, hg, dhp, MOBA_BLOCK), lambda g, j: (0, g, 0, 0))],
        out_specs=pl.BlockSpec((MOBA_BLOCK, gw), lambda g, j: (j, g)),
        out_shape=jax.ShapeDtypeStruct((s, BRANCH_WIDTH), BF16),
        scratch_shapes=[pltpu.VMEM((hg, nb, ATT_HEAD_DIM), F32),
                        pltpu.VMEM((hg, MOBA_BLOCK, MOBA_BLOCK), F32),
                        pltpu.VMEM((hg, MOBA_BLOCK, MOBA_BLOCK), F32),
                        pltpu.VMEM((hg, nb, MOBA_BLOCK), F32),
                        pltpu.VMEM((hg, 1, MOBA_BLOCK), F32),
                        pltpu.VMEM((hg, dhp, MOBA_BLOCK), F32)],
        compiler_params=_cparams(("arbitrary", "arbitrary")),
        name="moba",
    )(rel_bias.T, proj, proj, v_t)


def _s5_tables(lam_re, lam_im, log_dt, b_re, b_im, c_re, c_im):
    t_len = S5_CHUNK
    g_cnt, p_cnt, h_cnt = S5_GROUPS, S5_STATE, S5_GROUP
    tg = S5_TILE_GROUPS
    nt = g_cnt // tg
    dt = jnp.exp(log_dt)[:, None]
    ar, ai = lam_re * dt, lam_im * dt

    def lam_pow(steps):
        st = steps.astype(F32)[:, None, None]
        mag = jnp.exp(st * ar)
        return mag * jnp.cos(st * ai), mag * jnp.sin(st * ai)

    pr, pi = lam_pow(jnp.arange(t_len + 1))
    qr, qi = lam_pow((t_len - 1) - jnp.arange(t_len))
    nr, ni = pr[1] - 1.0, pi[1]
    den = lam_re * lam_re + lam_im * lam_im
    rr, ri = (nr * lam_re + ni * lam_im) / den, (ni * lam_re - nr * lam_im) / den
    bbr = rr[..., None] * b_re - ri[..., None] * b_im
    bbi = rr[..., None] * b_im + ri[..., None] * b_re

    def c_times(xr, xi):
        return (c_re[None] * xr[:, :, None, :] - c_im[None] * xi[:, :, None, :],
                c_re[None] * xi[:, :, None, :] + c_im[None] * xr[:, :, None, :])

    cqr, cqi = c_times(qr, qi)
    kc = (jnp.einsum('tghp,gpk->tgkh', cqr, bbr, precision=HIGHEST)
          - jnp.einsum('tghp,gpk->tgkh', cqi, bbi, precision=HIGHEST))
    kd = kc.reshape(t_len, nt, tg, h_cnt, h_cnt).transpose(1, 0, 3, 2, 4).reshape(nt, t_len, h_cnt, LANES)

    def per_tile(x):
        return x.reshape(-1, nt, tg, p_cnt).transpose(1, 0, 2, 3)[:, :, None]

    def b_tile(x):
        return x.reshape(nt, tg, p_cnt, h_cnt).transpose(0, 3, 1, 2)[:, None]

    q_re, q_im, bb_re, bb_im = per_tile(qr), per_tile(qi), b_tile(bbr), b_tile(bbi)
    bz = jnp.stack([q_re * bb_re - q_im * bb_im, q_re * bb_im + q_im * bb_re], axis=3)
    bz = bz.reshape(nt, t_len, h_cnt, 2 * tg * p_cnt)

    def p_tile(x):
        return x.reshape(t_len, nt, tg, p_cnt).transpose(1, 0, 3, 2)[..., None]

    def c_tile(x):
        return x.reshape(nt, tg, h_cnt, p_cnt).transpose(0, 3, 1, 2)[:, None]

    p_re, p_im, cc_re, cc_im = p_tile(pr[1:]), p_tile(pi[1:]), c_tile(c_re), c_tile(c_im)
    cm = jnp.stack([cc_re * p_re - cc_im * p_im, -(cc_re * p_im + cc_im * p_re)], axis=2)
    cm = cm.reshape(nt, t_len, 2 * p_cnt, LANES)
    a_re = pr[t_len].reshape(nt, 1, tg * p_cnt)
    a_im = pi[t_len].reshape(nt, 1, tg * p_cnt)
    return kd.astype(BF16), bz.astype(BF16), cm.astype(BF16), a_re, a_im


def _s5_kernel(u_ref, kd_ref, bz_ref, cm_ref, are_ref, aim_ref, d_ref, y_ref,
               uf_ref, ucat_ref, z_ref, hc_ref, kdf_ref, bzf_ref, cmf_ref, *, nc):
    t_len = S5_CHUNK
    ns = S5_TILE_STATE
    tg, hs, ps = S5_TILE_GROUPS, S5_GROUP, S5_STATE
    chan_grp = lax.broadcasted_iota(jnp.int32, (LANES, LANES), 0) // hs
    lane_grp = lax.broadcasted_iota(jnp.int32, (LANES, LANES), 1) // hs
    state_grp = (lax.broadcasted_iota(jnp.int32, (LANES, 2 * ns), 1) % ns) // ps
    chan_grp_w = lax.broadcasted_iota(jnp.int32, (LANES, 2 * ns), 0) // hs
    lane_grp_p = lax.broadcasted_iota(jnp.int32, (ps, LANES), 1) // hs
    zero = jnp.zeros((), BF16)
    for s in range(t_len):
        rows = slice(s * LANES, (s + 1) * LANES)
        kdf_ref[rows, :] = jnp.where(chan_grp == lane_grp, jnp.concatenate([kd_ref[0, 0, s]] * tg, axis=0), zero)
        bzf_ref[rows, :] = jnp.where(chan_grp_w == state_grp, jnp.concatenate([bz_ref[0, 0, s]] * tg, axis=0), zero)
        for x in range(2):
            piece = cm_ref[0, 0, s, x * ps:(x + 1) * ps, :]
            for g in range(tg):
                cmf_ref[s, x * ns + g * ps:x * ns + (g + 1) * ps, :] = jnp.where(lane_grp_p == g, piece, zero)
    uf_ref[...] = u_ref[...].astype(F32)
    for s in range(t_len):
        ucat_ref[:, s * LANES:(s + 1) * LANES] = uf_ref[pl.ds(s, nc, stride=t_len), :].astype(BF16)
    z_ref[...] = jnp.dot(ucat_ref[...], bzf_ref[...], preferred_element_type=F32)
    a_re = are_ref[0, 0]
    a_im = aim_ref[0, 0]

    def step(c, carry):
        h_re, h_im = carry
        hc_ref[pl.ds(c, 1), 0:ns] = h_re
        hc_ref[pl.ds(c, 1), ns:2 * ns] = h_im
        z_re = z_ref[pl.ds(c, 1), 0:ns]
        z_im = z_ref[pl.ds(c, 1), ns:2 * ns]
        return (a_re * h_re - a_im * h_im + z_re, a_re * h_im + a_im * h_re + z_im)

    zero_row = jnp.zeros((1, ns), F32)
    lax.fori_loop(0, nc, step, (zero_row, zero_row))
    hc = hc_ref[...].astype(BF16)
    d_skip = d_ref[0]
    for t in range(t_len):
        acc = jnp.dot(hc, cmf_ref[t], preferred_element_type=F32)
        acc += jnp.dot(ucat_ref[:, 0:(t + 1) * LANES],
                       kdf_ref[(t_len - 1 - t) * LANES:t_len * LANES, :],
                       preferred_element_type=F32)
        acc += d_skip * uf_ref[pl.ds(t, nc, stride=t_len), :]
        y_ref[pl.ds(t, nc, stride=t_len), :] = jax.nn.gelu(acc)


def _s5_glu_kernel(y_ref, yn_ref, w_ref, b_ref, o_ref, wb_ref):
    @pl.when(pl.program_id(1) == 0)
    def _():
        wb_ref[...] = w_ref[0].astype(BF16)

    z = jnp.dot(y_ref[...].astype(BF16), wb_ref[...], preferred_element_type=F32) + b_ref[0]
    o_ref[...] = (yn_ref[...] * jax.nn.sigmoid(z)).astype(o_ref.dtype)


def _s5(proj, tables, d_skip, w_glu, b_glu, layer):
    s = proj.shape[0]
    nc = s // S5_CHUNK
    nt = S5_GROUPS // S5_TILE_GROUPS
    kd, bz, cm, a_re, a_im = tables
    u_col = 3 * BRANCH_WIDTH // LANES
    ns2 = 2 * S5_TILE_STATE
    y = pl.pallas_call(
        functools.partial(_s5_kernel, nc=nc),
        grid=(nt,),
        in_specs=[pl.BlockSpec((s, LANES), lambda c: (0, u_col + c)),
                  pl.BlockSpec((1, 1, S5_CHUNK, S5_GROUP, LANES), lambda c: (layer, c, 0, 0, 0)),
                  pl.BlockSpec((1, 1, S5_CHUNK, S5_GROUP, ns2), lambda c: (layer, c, 0, 0, 0)),
                  pl.BlockSpec((1, 1, S5_CHUNK, 2 * S5_STATE, LANES), lambda c: (layer, c, 0, 0, 0)),
                  pl.BlockSpec((1, 1, 1, S5_TILE_STATE), lambda c: (layer, c, 0, 0)),
                  pl.BlockSpec((1, 1, 1, S5_TILE_STATE), lambda c: (layer, c, 0, 0)),
                  pl.BlockSpec((1, 1, LANES), lambda c: (layer, 0, c))],
        out_specs=pl.BlockSpec((s, LANES), lambda c: (0, c)),
        out_shape=jax.ShapeDtypeStruct((s, BRANCH_WIDTH), F32),
        scratch_shapes=[pltpu.VMEM((s, LANES), F32),
                        pltpu.VMEM((nc, S5_CHUNK * LANES), BF16),
                        pltpu.VMEM((nc, ns2), F32),
                        pltpu.VMEM((nc, ns2), F32),
                        pltpu.VMEM((S5_CHUNK * LANES, LANES), BF16),
                        pltpu.VMEM((S5_CHUNK * LANES, ns2), BF16),
                        pltpu.VMEM((S5_CHUNK, ns2, LANES), BF16)],
        compiler_params=_cparams(("arbitrary",)),
        name="s5_scan",
    )(proj, kd, bz, cm, a_re, a_im, d_skip.reshape(DEPTH, 1, BRANCH_WIDTH))
    tm, tn = 512, 512
    return pl.pallas_call(
        _s5_glu_kernel,
        grid=(BRANCH_WIDTH // tn, s // tm),
        in_specs=[pl.BlockSpec((tm, BRANCH_WIDTH), lambda j, i: (i, 0)),
                  pl.BlockSpec((tm, tn), lambda j, i: (i, j)),
                  pl.BlockSpec((1, BRANCH_WIDTH, tn), lambda j, i: (layer, 0, j)),
                  pl.BlockSpec((1, 1, tn), lambda j, i: (layer, 0, j))],
        out_specs=pl.BlockSpec((tm, tn), lambda j, i: (i, j)),
        out_shape=jax.ShapeDtypeStruct((s, BRANCH_WIDTH), BF16),
        scratch_shapes=[pltpu.VMEM((BRANCH_WIDTH, tn), BF16)],
        compiler_params=_cparams(("arbitrary", "arbitrary")),
        name="s5_glu",
    )(y, y, w_glu, b_glu.reshape(DEPTH, 1, BRANCH_WIDTH))


def _gla_kernel(q_ref, k_ref, v_ref, r_ref, tail_ref, wg_ref, bg_ref, ng_ref, o_ref, st_ref):
    dk = GLA_KEY // GLA_HEADS
    dv = GLA_VAL // GLA_HEADS
    cs = GLA_CHUNK
    sub = GLA_SUB
    nt = (((1,), (1,)), ((), ()))
    tn = (((0,), (0,)), ((), ()))

    @pl.when(pl.program_id(0) == 0)
    def _():
        st_ref[...] = jnp.zeros_like(st_ref)

    gate_in = jnp.dot(tail_ref[:, 0:GLA_GATE_RANK], wg_ref[0].astype(BF16),
                      preferred_element_type=F32) + bg_ref[0]
    log_a = jax.nn.log_sigmoid(gate_in) / GLA_GATE_TAU
    ri = lax.broadcasted_iota(jnp.int32, (cs, cs), 0)
    ci = lax.broadcasted_iota(jnp.int32, (cs, cs), 1)
    tril = (ri >= ci).astype(F32)
    bcum_all = jnp.dot(tril, log_a, preferred_element_type=F32, precision=HIGHEST)
    sub_row = lax.broadcasted_iota(jnp.int32, (sub, 1), 0)

    for h in range(GLA_HEADS):
        q = q_ref[:, h * dk:(h + 1) * dk].astype(F32) * dk ** -0.5
        k = k_ref[:, h * dk:(h + 1) * dk].astype(F32)
        v_bf = v_ref[:, h * dv:(h + 1) * dv]
        v = v_bf.astype(F32)
        bc = bcum_all[:, h * dk:(h + 1) * dk]
        state = st_ref[h]
        o_inter = lax.dot_general((q * jnp.exp(bc)).astype(BF16), state.astype(BF16), nt,
                                  preferred_element_type=F32)
        parts = []
        for i in range(cs // sub):
            lo = i * sub
            b_i, q_i, k_i, v_i = bc[lo:lo + sub], q[lo:lo + sub], k[lo:lo + sub], v[lo:lo + sub]
            o_i = o_inter[lo:lo + sub]
            if i > 0:
                ref = bc[lo - 1:lo]
                q_h = (q_i * jnp.exp(b_i - ref)).astype(BF16)
                k_h = (k[:lo] * jnp.exp(ref - bc[:lo])).astype(BF16)
                attn = lax.dot_general(q_h, k_h, nt, preferred_element_type=F32)
                o_i = o_i + jnp.dot(attn.astype(BF16), v_bf[:lo], preferred_element_type=F32)
            for dlt in range(sub):
                if dlt == 0:
                    a = jnp.sum(q_i * k_i, axis=-1, keepdims=True)
                    o_i = o_i + a * v_i
                else:
                    b_s = pltpu.roll(b_i, dlt, 0)
                    k_s = pltpu.roll(k_i, dlt, 0)
                    v_s = pltpu.roll(v_i, dlt, 0)
                    e = jnp.exp(jnp.minimum(b_i - b_s, 0.0))
                    a = jnp.sum(q_i * k_s * e, axis=-1, keepdims=True)
                    a = jnp.where(sub_row >= dlt, a, 0.0)
                    o_i = o_i + a * v_s
            parts.append(o_i)
        o = jnp.concatenate(parts, axis=0)
        b_last = bc[cs - 1:cs]
        k_dec = (k * jnp.exp(b_last - bc)).astype(BF16)
        st_ref[h] = state * jnp.exp(b_last) + lax.dot_general(v_bf, k_dec, tn,
                                                             preferred_element_type=F32)
        o = o * lax.rsqrt(jnp.mean(o * o, axis=-1, keepdims=True) + NORM_EPS)
        o = o * ng_ref[0, :, h * dv:(h + 1) * dv]
        r = r_ref[:, h * dv:(h + 1) * dv].astype(F32)
        o_ref[:, h * dv:(h + 1) * dv] = (o * (r * jax.nn.sigmoid(r))).astype(o_ref.dtype)


def _gla(proj, w_gate, b_gate, norm_g, layer):
    s = proj.shape[0]
    cs = GLA_CHUNK
    return pl.pallas_call(
        _gla_kernel,
        grid=(s // cs,),
        in_specs=[pl.BlockSpec((cs, GLA_KEY), lambda c: (c, 4 * BRANCH_WIDTH // GLA_KEY)),
                  pl.BlockSpec((cs, GLA_KEY), lambda c: (c, 4 * BRANCH_WIDTH // GLA_KEY + 1)),
                  pl.BlockSpec((cs, GLA_VAL), lambda c: (c, 5)),
                  pl.BlockSpec((cs, GLA_VAL), lambda c: (c, 6)),
                  pl.BlockSpec((cs, TAIL_W), lambda c: (c, TAIL_COL // TAIL_W)),
                  pl.BlockSpec((1, GLA_GATE_RANK, GLA_KEY), lambda c: (layer, 0, 0)),
                  pl.BlockSpec((1, 1, GLA_KEY), lambda c: (layer, 0, 0)),
                  pl.BlockSpec((1, 1, GLA_VAL), lambda c: (layer, 0, 0))],
        out_specs=pl.BlockSpec((cs, GLA_VAL), lambda c: (c, 0)),
        out_shape=jax.ShapeDtypeStruct((s, GLA_VAL), BF16),
        scratch_shapes=[pltpu.VMEM((GLA_HEADS, GLA_VAL // GLA_HEADS, GLA_KEY // GLA_HEADS), F32)],
        compiler_params=_cparams(("arbitrary",)),
        name="gla",
    )(proj, proj, proj, proj, proj, w_gate, b_gate.reshape(DEPTH, 1, GLA_KEY),
      norm_g.reshape(DEPTH, 1, GLA_VAL))


def _merge_kernel(ya_ref, ys_ref, yg_ref, tail_ref, wb_ref, wg0_ref, wg1_ref, wg2_ref,
                  bg0_ref, bg1_ref, bg2_ref, o_ref, wbb_ref, wgb_ref):
    wg_refs = (wg0_ref, wg1_ref, wg2_ref)
    bg_refs = (bg0_ref, bg1_ref, bg2_ref)

    @pl.when(pl.program_id(1) == 0)
    def _():
        wbb_ref[...] = wb_ref[0].astype(BF16)
        for n in range(N_BRANCH):
            wgb_ref[n] = wg_refs[n][0].astype(BF16)

    mz = tail_ref[:, GLA_GATE_RANK:GLA_GATE_RANK + MERGE_RANK]
    acc = None
    for n, y_ref in enumerate((ya_ref, ys_ref, yg_ref)):
        up = jnp.dot(y_ref[...], wbb_ref[n], preferred_element_type=F32)
        gate = jax.nn.sigmoid(jnp.dot(mz, wgb_ref[n], preferred_element_type=F32) + bg_refs[n][0])
        acc = gate * up if acc is None else acc + gate * up
    o_ref[...] = acc.astype(o_ref.dtype)


def _merge(y_att, y_s5, y_gla, proj, w_branch, w_merge_gate, b_merge_gate, layer, tm=512, tn=512):
    s = proj.shape[0]
    n_col = D_MODEL // tn
    bg = b_merge_gate.reshape(DEPTH, 1, N_BRANCH * D_MODEL)
    ybs = pl.BlockSpec((tm, BRANCH_WIDTH), lambda j, i: (i, 0))

    def gate_col(n):
        return lambda j, i: (layer, 0, n * n_col + j)

    return pl.pallas_call(
        _merge_kernel,
        grid=(n_col, s // tm),
        in_specs=[ybs, ybs, ybs,
                  pl.BlockSpec((tm, TAIL_W), lambda j, i: (i, TAIL_COL // TAIL_W)),
                  pl.BlockSpec((1, N_BRANCH, BRANCH_WIDTH, tn), lambda j, i: (layer, 0, 0, j))]
                 + [pl.BlockSpec((1, MERGE_RANK, tn), gate_col(n)) for n in range(N_BRANCH)]
                 + [pl.BlockSpec((1, 1, tn), gate_col(n)) for n in range(N_BRANCH)],
        out_specs=pl.BlockSpec((tm, tn), lambda j, i: (i, j)),
        out_shape=jax.ShapeDtypeStruct((s, D_MODEL), BF16),
        scratch_shapes=[pltpu.VMEM((N_BRANCH, BRANCH_WIDTH, tn), BF16),
                        pltpu.VMEM((N_BRANCH, MERGE_RANK, tn), BF16)],
        compiler_params=_cparams(("arbitrary", "arbitrary")),
        name="merge",
    )(y_att, y_s5, y_gla, proj, w_branch, w_merge_gate, w_merge_gate, w_merge_gate, bg, bg, bg)


def _swiglu_kernel(a_ref, w1_ref, w3_ref, o_ref, w1b_ref, w3b_ref):
    @pl.when(pl.program_id(1) == 0)
    def _():
        w1b_ref[...] = w1_ref[0].astype(BF16)
        w3b_ref[...] = w3_ref[0].astype(BF16)

    a = a_ref[...]
    g = jnp.dot(a, w1b_ref[...], preferred_element_type=F32)
    u = jnp.dot(a, w3b_ref[...], preferred_element_type=F32)
    o_ref[...] = (g * jax.nn.sigmoid(g) * u).astype(o_ref.dtype)


def _swiglu_hidden(a, w1_stack, w3_stack, layer, tm=1024, tn=256):
    m, k = a.shape
    n = w1_stack.shape[2]
    wspec = pl.BlockSpec((1, k, tn), lambda j, i: (layer, 0, j))
    return pl.pallas_call(
        _swiglu_kernel,
        grid=(n // tn, m // tm),
        in_specs=[pl.BlockSpec((tm, k), lambda j, i: (i, 0)), wspec, wspec],
        out_specs=pl.BlockSpec((tm, tn), lambda j, i: (i, j)),
        out_shape=jax.ShapeDtypeStruct((m, n), BF16),
        scratch_shapes=[pltpu.VMEM((k, tn), BF16), pltpu.VMEM((k, tn), BF16)],
        compiler_params=_cparams(("arbitrary", "arbitrary")),
        name="swiglu_hidden",
    )(a, w1_stack, w3_stack)


def _moe_plan(route):
    s = route.shape[0]
    tile = MOE_TILE
    n_tiles = (TOP_K * s) // tile + N_EXPERTS
    ids = route[:, ROUTE_ID:ROUTE_ID + TOP_K].astype(jnp.int32)
    onehot = jnp.sum(jax.nn.one_hot(ids, N_EXPERTS, dtype=jnp.int32), axis=1)
    before = jnp.cumsum(onehot, axis=0) - onehot
    counts = jnp.sum(onehot, axis=0)
    padded = (counts + tile - 1) // tile * tile
    ends = jnp.cumsum(padded)
    offsets = ends - padded
    pos = offsets[ids] + jnp.take_along_axis(before, ids, axis=1)
    tile_start = jnp.arange(n_tiles, dtype=jnp.int32) * tile
    tile_expert = jnp.minimum(jnp.sum(tile_start[:, None] >= ends[None, :], axis=1), N_EXPERTS - 1)
    n_active = (ends[-1] // tile).reshape(1)
    tokens = jnp.broadcast_to(jnp.arange(s, dtype=jnp.int32)[:, None], pos.shape)
    row_token = jnp.zeros((n_tiles * tile,), jnp.int32).at[pos.reshape(-1)].set(tokens.reshape(-1))
    pos1 = pos[:, 0].reshape(s // LANES, LANES)
    pos2 = pos[:, 1].reshape(s // LANES, LANES)
    return (pos1, pos2, row_token.reshape(-1, LANES), tile_expert.astype(jnp.int32),
            n_active.astype(jnp.int32), n_tiles)


def _moe_dispatch_kernel(rt_ref, na_ref, h_ref, o_ref, buf_ref, sem):
    i = pl.program_id(0)
    n_active = na_ref[0]

    def tile_copies(tile):
        slot = tile % 2
        return lambda r: (_row_copy(h_ref, buf_ref.at[slot], sem.at[slot],
                                    _smem_at(rt_ref, tile * MOE_TILE + r), r),)

    def start_tile(tile):
        def start(r, carry):
            for cp in tile_copies(tile)(r):
                cp.start()
            return carry
        lax.fori_loop(0, MOE_TILE, start, 0)

    @pl.when(jnp.logical_and(i == 0, n_active > 0))
    def _():
        start_tile(i)

    @pl.when(i + 1 < n_active)
    def _():
        start_tile(i + 1)

    @pl.when(i < n_active)
    def _():
        def wait(r, carry):
            for cp in tile_copies(i)(r):
                cp.wait()
            return carry
        lax.fori_loop(0, MOE_TILE, wait, 0)
        o_ref[...] = buf_ref[i % 2].astype(o_ref.dtype)

    @pl.when(i >= n_active)
    def _():
        o_ref[...] = jnp.zeros_like(o_ref)


def _moe_dispatch(h, row_token, n_active, n_tiles):
    tile = MOE_TILE
    return pl.pallas_call(
        _moe_dispatch_kernel,
        grid_spec=pltpu.PrefetchScalarGridSpec(
            num_scalar_prefetch=2, grid=(n_tiles,),
            in_specs=[pl.BlockSpec(memory_space=pl.ANY)],
            out_specs=pl.BlockSpec((tile, D_MODEL), lambda i, rt, na: (i, 0)),
            scratch_shapes=[pltpu.VMEM((2, tile, D_MODEL), F32), pltpu.SemaphoreType.DMA((2,))]),
        out_shape=jax.ShapeDtypeStruct((n_tiles * tile, D_MODEL), BF16),
        compiler_params=_cparams(("arbitrary",)),
        name="moe_dispatch",
    )(row_token, n_active, h)


def _new_expert_panel(te_ref):
    i = pl.program_id(1)
    return jnp.logical_or(i == 0, te_ref[i] != te_ref[jnp.maximum(i - 1, 0)])


def _moe_hidden_kernel(te_ref, na_ref, a_ref, w1_ref, w3_ref, o_ref, w1b_ref, w3b_ref):
    @pl.when(_new_expert_panel(te_ref))
    def _():
        w1b_ref[...] = w1_ref[0, 0].astype(BF16)
        w3b_ref[...] = w3_ref[0, 0].astype(BF16)

    @pl.when(pl.program_id(1) < na_ref[0])
    def _():
        a = a_ref[...]
        g = jnp.dot(a, w1b_ref[...], preferred_element_type=F32)
        u = jnp.dot(a, w3b_ref[...], preferred_element_type=F32)
        o_ref[...] = (g * jax.nn.sigmoid(g) * u).astype(o_ref.dtype)

    @pl.when(pl.program_id(1) >= na_ref[0])
    def _():
        o_ref[...] = jnp.zeros_like(o_ref)


def _moe_down_kernel(te_ref, na_ref, a_ref, w_ref, o_ref, wb_ref):
    @pl.when(_new_expert_panel(te_ref))
    def _():
        wb_ref[...] = w_ref[0, 0].astype(BF16)

    @pl.when(pl.program_id(1) < na_ref[0])
    def _():
        o_ref[...] = jnp.dot(a_ref[...], wb_ref[...], preferred_element_type=F32).astype(o_ref.dtype)

    @pl.when(pl.program_id(1) >= na_ref[0])
    def _():
        o_ref[...] = jnp.zeros_like(o_ref)


def _moe_experts(xs, tile_expert, n_active, w1_stack, w3_stack, w2_stack, layer, n_tiles,
                 tn_hidden=256, tn_down=1024):
    tile = MOE_TILE
    rows = n_tiles * tile
    w_in_spec = pl.BlockSpec((1, 1, D_MODEL, tn_hidden), lambda n, i, te, na: (layer, te[i], 0, n))
    hid = pl.pallas_call(
        _moe_hidden_kernel,
        grid_spec=pltpu.PrefetchScalarGridSpec(
            num_scalar_prefetch=2, grid=(D_FF_EXPERT // tn_hidden, n_tiles),
            in_specs=[pl.BlockSpec((tile, D_MODEL), lambda n, i, te, na: (i, 0)), w_in_spec, w_in_spec],
            out_specs=pl.BlockSpec((tile, tn_hidden), lambda n, i, te, na: (i, n)),
            scratch_shapes=[pltpu.VMEM((D_MODEL, tn_hidden), BF16), pltpu.VMEM((D_MODEL, tn_hidden), BF16)]),
        out_shape=jax.ShapeDtypeStruct((rows, D_FF_EXPERT), BF16),
        compiler_params=_cparams(("arbitrary", "arbitrary")),
        name="moe_hidden",
    )(tile_expert, n_active, xs, w1_stack, w3_stack)
    return pl.pallas_call(
        _moe_down_kernel,
        grid_spec=pltpu.PrefetchScalarGridSpec(
            num_scalar_prefetch=2, grid=(D_MODEL // tn_down, n_tiles),
            in_specs=[pl.BlockSpec((tile, D_FF_EXPERT), lambda n, i, te, na: (i, 0)),
                      pl.BlockSpec((1, 1, D_FF_EXPERT, tn_down), lambda n, i, te, na: (layer, te[i], 0, n))],
            out_specs=pl.BlockSpec((tile, tn_down), lambda n, i, te, na: (i, n)),
            scratch_shapes=[pltpu.VMEM((D_FF_EXPERT, tn_down), BF16)]),
        out_shape=jax.ShapeDtypeStruct((rows, D_MODEL), F32),
        compiler_params=_cparams(("arbitrary", "arbitrary")),
        name="moe_down",
    )(tile_expert, n_active, hid, w2_stack)


def kernel(x, c, w_cond, b_cond, rel_bias, w_mod, b_mod, w_in, s5_lambda_re, s5_lambda_im, s5_log_dt, s5_b_re, s5_b_im, s5_c_re, s5_c_im, s5_d, s5_w_glu, s5_b_glu, gla_w_gate, gla_b_gate, gla_norm_g, w_branch, w_merge_gate, b_merge_gate, w_out, ln1_g, ln1_b, ffn_w1, ffn_w3, ffn_w2, router_w, router_b, exp_w1, exp_w3, exp_w2, ln2_g, ln2_b):
    bsz, seq, _ = x.shape
    assert bsz == 1
    mod = _conditioning(c, w_cond, b_cond, w_mod, b_mod)
    xs = x.reshape(seq, D_MODEL)
    w_in_t = jnp.swapaxes(w_in, 1, 2)
    hm = _modulate(xs, mod[0, 1], mod[0, 0])
    s5_tables = jax.vmap(_s5_tables)(s5_lambda_re, s5_lambda_im, s5_log_dt, s5_b_re, s5_b_im, s5_c_re, s5_c_im)
    for l in range(DEPTH):
        shift_f, scale_f, gate_m, gate_f = mod[l, 3], mod[l, 4], mod[l, 2], mod[l, 5]
        proj = _matmul(hm, w_in_t, l, tm=512, tn=512, n_out=D_IN_PAD, w_transposed=True,
                       name="in_proj")
        y_att = _moba(proj, rel_bias)
        y_s5 = _s5(proj, s5_tables, s5_d, s5_w_glu, s5_b_glu, l)
        y_gla = _gla(proj, gla_w_gate, gla_b_gate, gla_norm_g, l)
        merged = _merge(y_att, y_s5, y_gla, proj, w_branch, w_merge_gate, b_merge_gate, l)
        y = _matmul(merged, w_out, l, tm=512, tn=512, name="out_proj")
        dense = l % 2 == 0
        router = None if dense else (router_w[l // 2], router_b[l // 2])
        outs = _deepnorm_ln(xs, y, gate_m, ln1_g[l], ln1_b[l], nxt=(scale_f, shift_f), router=router)
        xs, hf = outs[0], outs[1]
        if dense:
            hid = _swiglu_hidden(hf, ffn_w1, ffn_w3, l // 2)
            f = _matmul(hid, ffn_w2, l // 2, tm=256, tn=512, name="ffn_down")
            moe = None
        else:
            route = outs[2]
            pos1, pos2, row_token, tile_expert, n_active, n_tiles = _moe_plan(route)
            xsorted = _moe_dispatch(hf, row_token, n_active, n_tiles)
            ys = _moe_experts(xsorted, tile_expert, n_active, exp_w1, exp_w3, exp_w2, l // 2, n_tiles)
            f = None
            moe = (ys, pos1, pos2, route)
        nxt = (mod[l + 1, 1], mod[l + 1, 0]) if l + 1 < DEPTH else None
        outs = _deepnorm_ln(xs, f, gate_f, ln2_g[l], ln2_b[l], nxt=nxt, moe=moe)
        xs = outs[0]
        if nxt is not None:
            hm = outs[1]
    return xs.reshape(bsz, seq, D_MODEL)
```

```python
import functools
import math

import jax
import jax.numpy as jnp
from jax import lax
from jax.experimental import pallas as pl
from jax.experimental.pallas import tpu as pltpu

F32 = jnp.float32
BF16 = jnp.bfloat16
HIGHEST = lax.Precision.HIGHEST

D_MODEL = 4096
DEPTH = 4
BRANCH_WIDTH = 1024
N_BRANCH = 3
ATT_HEADS = 8
ATT_HEAD_DIM = 128
MOBA_BLOCK = 256
MOBA_TOPK = 3
MOBA_HEAD_GROUP = 4
MOBA_ONES_ROWS = 16
REL_BUCKETS = 32
REL_MAX_DIST = 128
S5_GROUP = 16
S5_GROUPS = 64
S5_STATE = 64
S5_CHUNK = 16
S5_TILE_GROUPS = 8
S5_TILE_STATE = S5_TILE_GROUPS * S5_STATE
GLA_HEADS = 4
GLA_KEY = 512
GLA_VAL = 1024
GLA_GATE_RANK = 16
GLA_GATE_TAU = 16.0
GLA_CHUNK = 64
GLA_SUB = 16
MERGE_RANK = 256
COND_RANK = 512
D_FF = 8192
N_EXPERTS = 8
TOP_K = 2
ROUTE_ID = 8
ROUTE_W = 10
MOE_TILE = 256
DMA_ISSUE_UNROLL = 8
D_FF_EXPERT = 1792
DN_ALPHA = (2 * DEPTH) ** 0.25
LN_EPS = 1e-5
NORM_EPS = 1e-6

D_IN = 7440
D_IN_PAD = 7680
TAIL_COL = 7168
TAIL_W = 512
LANES = 128
VMEM_LIMIT = 56 * 1024 * 1024
NEG_INF = float("-inf")


def _cparams(sem):
    return pltpu.CompilerParams(dimension_semantics=sem, vmem_limit_bytes=VMEM_LIMIT)


def _cond_kernel(c_ref, w_ref, b_ref, o_ref):
    z = jnp.dot(c_ref[...], w_ref[...], preferred_element_type=F32, precision=HIGHEST) + b_ref[...]
    o_ref[...] = z * jax.nn.sigmoid(z)


def _mod_kernel(cond_ref, w_ref, b_ref, o_ref):
    o_ref[0] = jnp.dot(cond_ref[...], w_ref[0], preferred_element_type=F32,
                       precision=HIGHEST) + b_ref[0]


def _conditioning(c, w_cond, b_cond, w_mod, b_mod):
    c8 = jnp.broadcast_to(c, (8, D_MODEL))
    cond = pl.pallas_call(
        _cond_kernel,
        out_shape=jax.ShapeDtypeStruct((8, COND_RANK), F32),
        compiler_params=_cparams(None),
        name="cond",
    )(c8, w_cond, b_cond.reshape(1, COND_RANK))
    n_mod = 6 * D_MODEL
    tn = 3072
    mod = pl.pallas_call(
        _mod_kernel,
        grid=(DEPTH, n_mod // tn),
        in_specs=[pl.BlockSpec((8, COND_RANK), lambda l, n: (0, 0)),
                  pl.BlockSpec((1, COND_RANK, tn), lambda l, n: (l, 0, n)),
                  pl.BlockSpec((1, 1, tn), lambda l, n: (l, 0, n))],
        out_specs=pl.BlockSpec((1, 8, tn), lambda l, n: (l, 0, n)),
        out_shape=jax.ShapeDtypeStruct((DEPTH, 8, n_mod), F32),
        compiler_params=_cparams(("arbitrary", "arbitrary")),
        name="mod",
    )(cond, w_mod, b_mod.reshape(DEPTH, 1, n_mod))
    return mod[:, 0, :].reshape(DEPTH, 6, 1, D_MODEL)


def _modulate_kernel(x_ref, scale_ref, shift_ref, o_ref):
    o_ref[...] = (x_ref[...] * (1.0 + scale_ref[...]) + shift_ref[...]).astype(o_ref.dtype)


def _modulate(x, scale, shift, tm=512):
    s = x.shape[0]
    vec = pl.BlockSpec((1, D_MODEL), lambda m: (0, 0))
    return pl.pallas_call(
        _modulate_kernel,
        grid=(s // tm,),
        in_specs=[pl.BlockSpec((tm, D_MODEL), lambda m: (m, 0)), vec, vec],
        out_specs=pl.BlockSpec((tm, D_MODEL), lambda m: (m, 0)),
        out_shape=jax.ShapeDtypeStruct((s, D_MODEL), BF16),
        compiler_params=_cparams(("arbitrary",)),
        name="modulate",
    )(x, scale, shift)


def _route_top2(logits):
    lane = lax.broadcasted_iota(jnp.int32, logits.shape, 1)
    m1 = jnp.max(logits, axis=-1, keepdims=True)
    i1 = jnp.min(jnp.where(logits == m1, lane, LANES), axis=-1, keepdims=True)
    rest = jnp.where(lane == i1, NEG_INF, logits)
    m2 = jnp.max(rest, axis=-1, keepdims=True)
    i2 = jnp.min(jnp.where(rest == m2, lane, LANES), axis=-1, keepdims=True)
    e2 = jnp.exp(m2 - m1)
    denom = 1.0 + e2
    rec = jnp.where(lane == ROUTE_ID, i1.astype(F32), 0.0)
    rec = jnp.where(lane == ROUTE_ID + 1, i2.astype(F32), rec)
    rec = jnp.where(lane == ROUTE_W, 1.0 / denom, rec)
    return jnp.where(lane == ROUTE_W + 1, e2 / denom, rec)


def _row_copy(src_ref, dst_ref, sem, src_row, dst_row):
    return pltpu.make_async_copy(src_ref.at[pl.ds(src_row, 1)], dst_ref.at[pl.ds(dst_row, 1)], sem)


def _smem_at(ref, i):
    return ref[i // LANES, i % LANES]


def _start_rows(copies_of, n_rows):
    def start(r, carry):
        for cp in copies_of(r):
            cp.start()
        return carry

    lax.fori_loop(0, n_rows, start, 0, unroll=DMA_ISSUE_UNROLL)


def _wait_rows(src_ref, dst_ref, sem):
    pltpu.make_async_copy(src_ref.at[pl.ds(0, dst_ref.shape[0])], dst_ref, sem).wait()


def _ln_kernel(*refs, has_next, has_router, moe_combine, tm):
    pos = 0
    if moe_combine:
        pos1_ref, pos2_ref, x_ref, route_ref, ys_ref = refs[:5]
        pos = 5
    else:
        x_ref, y_ref = refs[:2]
        pos = 2
    gate_ref, g_ref, b_ref = refs[pos:pos + 3]
    pos += 3
    if has_next:
        scale_ref, shift_ref = refs[pos:pos + 2]
        pos += 2
    if has_router:
        rw_ref, rb_ref = refs[pos:pos + 2]
        pos += 2
    xo_ref = refs[pos]
    pos += 1
    if moe_combine:
        buf1_ref, buf2_ref, sem = refs[-3:]
        base = pl.program_id(0) * tm

        _start_rows(lambda r: (_row_copy(ys_ref, buf1_ref, sem, _smem_at(pos1_ref, base + r), r),
                               _row_copy(ys_ref, buf2_ref, sem, _smem_at(pos2_ref, base + r), r)), tm)
        _wait_rows(ys_ref, buf1_ref, sem)
        _wait_rows(ys_ref, buf2_ref, sem)
        route = route_ref[...]
        lane = lax.broadcasted_iota(jnp.int32, route.shape, 1)
        w1 = jnp.sum(jnp.where(lane == ROUTE_W, route, 0.0), axis=-1, keepdims=True)
        w2 = jnp.sum(jnp.where(lane == ROUTE_W + 1, route, 0.0), axis=-1, keepdims=True)
        y = w1 * buf1_ref[...] + w2 * buf2_ref[...]
    else:
        y = y_ref[...].astype(F32)
    z = DN_ALPHA * x_ref[...] + (1.0 + gate_ref[...]) * y
    mu = jnp.mean(z, axis=-1, keepdims=True)
    zc = z - mu
    var = jnp.mean(zc * zc, axis=-1, keepdims=True)
    xn = zc * lax.rsqrt(var + LN_EPS) * g_ref[...] + b_ref[...]
    xo_ref[...] = xn
    if has_next:
        ho_ref = refs[pos]
        pos += 1
        h = xn * (1.0 + scale_ref[...]) + shift_ref[...]
        ho_ref[...] = h.astype(ho_ref.dtype)
        if has_router:
            co_ref = refs[pos]
            logits = jnp.dot(h, rw_ref[...], preferred_element_type=F32, precision=HIGHEST)
            lane = lax.broadcasted_iota(jnp.int32, logits.shape, 1)
            logits = jnp.where(lane < N_EXPERTS, logits + rb_ref[...], NEG_INF)
            co_ref[...] = _route_top2(logits)


def _deepnorm_ln(x, y, gate, g, b, nxt=None, router=None, moe=None, tm=256):
    s = x.shape[0]
    n_pre = 0 if moe is None else 2
    imap = (lambda m: (m, 0)) if moe is None else (lambda m, p1, p2: (m, 0))
    vmap = (lambda m: (0, 0)) if moe is None else (lambda m, p1, p2: (0, 0))
    row = pl.BlockSpec((tm, D_MODEL), imap)
    vec = pl.BlockSpec((1, D_MODEL), vmap)
    scratch = []
    if moe is None:
        args = [x, y]
        in_specs = [row, row]
    else:
        ys, pos1, pos2, route = moe
        args = [pos1, pos2, x, route, ys]
        in_specs = [row, pl.BlockSpec((tm, LANES), imap), pl.BlockSpec(memory_space=pl.ANY)]
        scratch = [pltpu.VMEM((tm, D_MODEL), F32), pltpu.VMEM((tm, D_MODEL), F32),
                   pltpu.SemaphoreType.DMA(())]
    args += [gate, g.reshape(1, D_MODEL), b.reshape(1, D_MODEL)]
    in_specs += [vec, vec, vec]
    out_shape = [jax.ShapeDtypeStruct((s, D_MODEL), F32)]
    out_specs = [row]
    if nxt is not None:
        args += [nxt[0], nxt[1]]
        in_specs += [vec, vec]
        out_shape.append(jax.ShapeDtypeStruct((s, D_MODEL), BF16 if router is None else F32))
        out_specs.append(row)
    if router is not None:
        rw, rb = router
        rw_pad = jnp.pad(rw, ((0, 0), (0, LANES - N_EXPERTS)))
        rb_pad = jnp.pad(rb, (0, LANES - N_EXPERTS)).reshape(1, LANES)
        args += [rw_pad, rb_pad]
        in_specs += [pl.BlockSpec((D_MODEL, LANES), vmap), pl.BlockSpec((1, LANES), vmap)]
        out_shape.append(jax.ShapeDtypeStruct((s, LANES), F32))
        out_specs.append(pl.BlockSpec((tm, LANES), imap))
    return pl.pallas_call(
        functools.partial(_ln_kernel, has_next=nxt is not None, has_router=router is not None,
                          moe_combine=moe is not None, tm=tm),
        grid_spec=pltpu.PrefetchScalarGridSpec(
            num_scalar_prefetch=n_pre, grid=(s // tm,), in_specs=in_specs, out_specs=out_specs,
            scratch_shapes=scratch),
        out_shape=out_shape,
        compiler_params=_cparams(("arbitrary",)),
        name="deepnorm_ln",
    )(*args)


def _matmul_kernel(a_ref, w_ref, o_ref, wb_ref, *, n_valid, tn, w_transposed):
    @pl.when(pl.program_id(1) == 0)
    def _():
        w = w_ref[0]
        wb_ref[...] = (w.T if w_transposed else w).astype(BF16)

    acc = jnp.dot(a_ref[...], wb_ref[...], preferred_element_type=F32)
    if n_valid is not None:
        col = pl.program_id(0) * tn + lax.broadcasted_iota(jnp.int32, acc.shape, 1)
        acc = jnp.where(col < n_valid, acc, 0.0)
    o_ref[...] = acc.astype(o_ref.dtype)


def _matmul(a, w_stack, layer, tm, tn, out_dtype=BF16, n_out=None, w_transposed=False, name="matmul"):
    m, k = a.shape
    n = w_stack.shape[1 if w_transposed else 2]
    n_out = n if n_out is None else n_out
    if w_transposed:
        w_spec = pl.BlockSpec((1, tn, k), lambda j, i: (layer, j, 0))
    else:
        w_spec = pl.BlockSpec((1, k, tn), lambda j, i: (layer, 0, j))
    return pl.pallas_call(
        functools.partial(_matmul_kernel, n_valid=None if n_out == n else n, tn=tn,
                          w_transposed=w_transposed),
        grid=(n_out // tn, m // tm),
        in_specs=[pl.BlockSpec((tm, k), lambda j, i: (i, 0)), w_spec],
        out_specs=pl.BlockSpec((tm, tn), lambda j, i: (i, j)),
        out_shape=jax.ShapeDtypeStruct((m, n_out), out_dtype),
        scratch_shapes=[pltpu.VMEM((k, tn), BF16)],
        compiler_params=_cparams(("arbitrary", "arbitrary")),
        name=name,
    )(a, w_stack)


def _rel_bucket(dist):
    n = jnp.maximum(dist, 0)
    max_exact = REL_BUCKETS // 2
    nf = jnp.maximum(n, 1).astype(F32)
    large = max_exact + (jnp.log(nf / max_exact) / math.log(REL_MAX_DIST / max_exact)
                         * (REL_BUCKETS - max_exact)).astype(jnp.int32)
    large = jnp.minimum(large, REL_BUCKETS - 1)
    return jnp.where(n < max_exact, n, large)


def _moba_kernel(relb_ref, q_ref, k_ref, vt_ref, o_ref,
                 kmean_ref, bown_ref, bprev_ref, sel_ref, m_ref, acc_ref, *, nb):
    g = pl.program_id(0)
    j = pl.program_id(1)
    blk = MOBA_BLOCK
    dh = ATT_HEAD_DIM
    scale = dh ** -0.5
    ln2 = math.log(2.0)
    nt = (((1,), (1,)), ((), ()))
    heads = range(MOBA_HEAD_GROUP)
    key_i = lax.broadcasted_iota(jnp.int32, (blk, blk), 0)
    qry_i = lax.broadcasted_iota(jnp.int32, (blk, blk), 1)

    @pl.when(j == 0)
    def _():
        for hh in heads:
            head = g * MOBA_HEAD_GROUP + hh
            kf = k_ref[:, hh * dh:(hh + 1) * dh].astype(F32).reshape(nb, blk, dh)
            kmean_ref[hh] = jnp.mean(kf, axis=1)
            for ref, off in ((bown_ref, 0), (bprev_ref, blk)):
                bucket = _rel_bucket(qry_i - key_i + off)
                bias = jnp.zeros((blk, blk), F32)
                for b in range(REL_BUCKETS):
                    bias = jnp.where(bucket == b, relb_ref[head, b], bias)
                ref[hh] = bias

    row0 = pl.multiple_of(j * blk, blk)
    blk_i = lax.broadcasted_iota(jnp.int32, (nb, blk), 0)
    q2 = []
    for hh in heads:
        q = q_ref[:, hh * dh:(hh + 1) * dh]
        q2.append((q.astype(F32) * (scale / ln2)).astype(BF16))
        score = lax.dot_general(kmean_ref[hh], q.astype(F32), nt,
                                preferred_element_type=F32, precision=HIGHEST)
        sc = jnp.where(blk_i < j, score, NEG_INF)
        seladd = jnp.full((nb, blk), NEG_INF, F32)
        for _ in range(MOBA_TOPK):
            mx = jnp.max(sc, axis=0, keepdims=True)
            cand = jnp.where(sc == mx, blk_i, nb)
            cand = jnp.where(mx > NEG_INF, cand, nb)
            idx = jnp.min(cand, axis=0, keepdims=True)
            pick = blk_i == idx
            seladd = jnp.where(pick, 0.0, seladd)
            sc = jnp.where(pick, NEG_INF, sc)
        sel_ref[hh] = seladd

    def scores(kb_row0, hh):
        return lax.dot_general(k_ref[pl.ds(kb_row0, blk), hh * dh:(hh + 1) * dh], q2[hh], nt,
                               preferred_element_type=F32)

    def weighted_values(kb, hh, p):
        return jnp.dot(vt_ref[kb, hh], p.astype(BF16), preferred_element_type=F32)

    for hh in heads:
        s = scores(row0, hh) * ln2 + bown_ref[hh]
        s = jnp.where(key_i <= qry_i, s, NEG_INF)
        m0 = jnp.max(s, axis=0, keepdims=True)
        m_ref[hh] = m0
        acc_ref[hh] = weighted_values(j, hh, jnp.exp(s - m0))

    def merge(hh, kb, m_blk, acc_blk):
        m_blk = m_blk + sel_ref[hh, pl.ds(kb, 1), :]
        m_old = m_ref[hh]
        m_new = jnp.maximum(m_old, m_blk)
        m_ref[hh] = m_new
        acc_ref[hh] = jnp.exp(m_old - m_new) * acc_ref[hh] + jnp.exp(m_blk - m_new) * acc_blk

    def far_blocks(pair, carry):
        chains = [(2 * pair + second, hh, second) for second in (0, 1) for hh in heads]
        s2 = [scores(pl.multiple_of(kb * blk, blk), hh) for kb, hh, _ in chains]
        m2 = [jnp.max(s, axis=0, keepdims=True) for s in s2]
        acc_blk = [weighted_values(kb, hh, jnp.exp2(s - m)) for (kb, hh, _), s, m in zip(chains, s2, m2)]
        second_ok = jnp.where(2 * pair + 1 < j - 1, 0.0, NEG_INF)
        for (kb, hh, second), m, acc in zip(chains, m2, acc_blk):
            m_blk = m * ln2 + relb_ref[g * MOBA_HEAD_GROUP + hh, REL_BUCKETS - 1]
            merge(hh, kb, m_blk + second_ok if second else m_blk, acc)
        return carry

    lax.fori_loop(0, j // 2, far_blocks, 0)
    kb = jnp.maximum(j - 1, 0)
    r0 = pl.multiple_of(kb * blk, blk)
    for hh in heads:
        s = scores(r0, hh) * ln2 + bprev_ref[hh]
        m_blk = jnp.max(s, axis=0, keepdims=True)
        merge(hh, kb, m_blk, weighted_values(kb, hh, jnp.exp(s - m_blk)))
        acc = acc_ref[hh]
        o_ref[:, hh * dh:(hh + 1) * dh] = (acc[0:dh] / acc[dh:dh + 1]).T.astype(o_ref.dtype)


def _moba(proj, rel_bias):
    s = proj.shape[0]
    nb = s // MOBA_BLOCK
    hg = MOBA_HEAD_GROUP
    gw = hg * ATT_HEAD_DIM
    n_groups = ATT_HEADS // hg
    dhp = ATT_HEAD_DIM + MOBA_ONES_ROWS
    v_t = proj[:, 2 * BRANCH_WIDTH:3 * BRANCH_WIDTH].reshape(nb, MOBA_BLOCK, ATT_HEADS, ATT_HEAD_DIM)
    v_t = jnp.concatenate([v_t.transpose(0, 2, 3, 1),
                           jnp.ones((nb, ATT_HEADS, MOBA_ONES_ROWS, MOBA_BLOCK), BF16)], axis=2)
    return pl.pallas_call(
        functools.partial(_moba_kernel, nb=nb),
        grid=(n_groups, nb),
        in_specs=[pl.BlockSpec(memory_space=pltpu.SMEM),
                  pl.BlockSpec((MOBA_BLOCK, gw), lambda g, j: (j, g)),
                  pl.BlockSpec((s, gw), lambda g, j: (0, n_groups + g)),
                  pl.BlockSpec((nb, hg, dhp, MOBA_BLOCK), lambda g, j: (0, g, 0, 0))],
        out_specs=pl.BlockSpec((MOBA_BLOCK, gw), lambda g, j: (j, g)),
        out_shape=jax.ShapeDtypeStruct((s, BRANCH_WIDTH), BF16),
        scratch_shapes=[pltpu.VMEM((hg, nb, ATT_HEAD_DIM), F32),
                        pltpu.VMEM((hg, MOBA_BLOCK, MOBA_BLOCK), F32),
                        pltpu.VMEM((hg, MOBA_BLOCK, MOBA_BLOCK), F32),
                        pltpu.VMEM((hg, nb, MOBA_BLOCK), F32),
                        pltpu.VMEM((hg, 1, MOBA_BLOCK), F32),
                        pltpu.VMEM((hg, dhp, MOBA_BLOCK), F32)],
        compiler_params=_cparams(("arbitrary", "arbitrary")),
        name="moba",
    )(rel_bias.T, proj, proj, v_t)


def _s5_tables(lam_re, lam_im, log_dt, b_re, b_im, c_re, c_im):
    t_len = S5_CHUNK
    g_cnt, p_cnt, h_cnt = S5_GROUPS, S5_STATE, S5_GROUP
    tg = S5_TILE_GROUPS
    nt = g_cnt // tg
    dt = jnp.exp(log_dt)[:, None]
    ar, ai = lam_re * dt, lam_im * dt

    def lam_pow(steps):
        st = steps.astype(F32)[:, None, None]
        mag = jnp.exp(st * ar)
        return mag * jnp.cos(st * ai), mag * jnp.sin(st * ai)

    pr, pi = lam_pow(jnp.arange(t_len + 1))
    qr, qi = lam_pow((t_len - 1) - jnp.arange(t_len))
    nr, ni = pr[1] - 1.0, pi[1]
    den = lam_re * lam_re + lam_im * lam_im
    rr, ri = (nr * lam_re + ni * lam_im) / den, (ni * lam_re - nr * lam_im) / den
    bbr = rr[..., None] * b_re - ri[..., None] * b_im
    bbi = rr[..., None] * b_im + ri[..., None] * b_re

    def c_times(xr, xi):
        return (c_re[None] * xr[:, :, None, :] - c_im[None] * xi[:, :, None, :],
                c_re[None] * xi[:, :, None, :] + c_im[None] * xr[:, :, None, :])

    cqr, cqi = c_times(qr, qi)
    kc = (jnp.einsum('tghp,gpk->tgkh', cqr, bbr, precision=HIGHEST)
          - jnp.einsum('tghp,gpk->tgkh', cqi, bbi, precision=HIGHEST))
    kd = kc.reshape(t_len, nt, tg, h_cnt, h_cnt).transpose(1, 0, 3, 2, 4).reshape(nt, t_len, h_cnt, LANES)

    def per_tile(x):
        return x.reshape(-1, nt, tg, p_cnt).transpose(1, 0, 2, 3)[:, :, None]

    def b_tile(x):
        return x.reshape(nt, tg, p_cnt, h_cnt).transpose(0, 3, 1, 2)[:, None]

    q_re, q_im, bb_re, bb_im = per_tile(qr), per_tile(qi), b_tile(bbr), b_tile(bbi)
    bz = jnp.stack([q_re * bb_re - q_im * bb_im, q_re * bb_im + q_im * bb_re], axis=3)
    bz = bz.reshape(nt, t_len, h_cnt, 2 * tg * p_cnt)

    def p_tile(x):
        return x.reshape(t_len, nt, tg, p_cnt).transpose(1, 0, 3, 2)[..., None]

    def c_tile(x):
        return x.reshape(nt, tg, h_cnt, p_cnt).transpose(0, 3, 1, 2)[:, None]

    p_re, p_im, cc_re, cc_im = p_tile(pr[1:]), p_tile(pi[1:]), c_tile(c_re), c_tile(c_im)
    cm = jnp.stack([cc_re * p_re - cc_im * p_im, -(cc_re * p_im + cc_im * p_re)], axis=2)
    cm = cm.reshape(nt, t_len, 2 * p_cnt, LANES)
    a_re = pr[t_len].reshape(nt, 1, tg * p_cnt)
    a_im = pi[t_len].reshape(nt, 1, tg * p_cnt)
    return kd.astype(BF16), bz.astype(BF16), cm.astype(BF16), a_re, a_im


def _s5_kernel(u_ref, kd_ref, bz_ref, cm_ref, are_ref, aim_ref, d_ref, y_ref,
               uf_ref, ucat_ref, z_ref, hc_ref, kdf_ref, bzf_ref, cmf_ref, *, nc):
    t_len = S5_CHUNK
    ns = S5_TILE_STATE
    tg, hs, ps = S5_TILE_GROUPS, S5_GROUP, S5_STATE
    chan_grp = lax.broadcasted_iota(jnp.int32, (LANES, LANES), 0) // hs
    lane_grp = lax.broadcasted_iota(jnp.int32, (LANES, LANES), 1) // hs
    state_grp = (lax.broadcasted_iota(jnp.int32, (LANES, 2 * ns), 1) % ns) // ps
    chan_grp_w = lax.broadcasted_iota(jnp.int32, (LANES, 2 * ns), 0) // hs
    lane_grp_p = lax.broadcasted_iota(jnp.int32, (ps, LANES), 1) // hs
    zero = jnp.zeros((), BF16)
    for s in range(t_len):
        rows = slice(s * LANES, (s + 1) * LANES)
        kdf_ref[rows, :] = jnp.where(chan_grp == lane_grp, jnp.concatenate([kd_ref[0, 0, s]] * tg, axis=0), zero)
        bzf_ref[rows, :] = jnp.where(chan_grp_w == state_grp, jnp.concatenate([bz_ref[0, 0, s]] * tg, axis=0), zero)
        for x in range(2):
            piece = cm_ref[0, 0, s, x * ps:(x + 1) * ps, :]
            for g in range(tg):
                cmf_ref[s, x * ns + g * ps:x * ns + (g + 1) * ps, :] = jnp.where(lane_grp_p == g, piece, zero)
    uf_ref[...] = u_ref[...].astype(F32)
    for s in range(t_len):
        ucat_ref[:, s * LANES:(s + 1) * LANES] = uf_ref[pl.ds(s, nc, stride=t_len), :].astype(BF16)
    z_ref[...] = jnp.dot(ucat_ref[...], bzf_ref[...], preferred_element_type=F32)
    a_re = are_ref[0, 0]
    a_im = aim_ref[0, 0]

    def step(c, carry):
        h_re, h_im = carry
        hc_ref[pl.ds(c, 1), 0:ns] = h_re
        hc_ref[pl.ds(c, 1), ns:2 * ns] = h_im
        z_re = z_ref[pl.ds(c, 1), 0:ns]
        z_im = z_ref[pl.ds(c, 1), ns:2 * ns]
        return (a_re * h_re - a_im * h_im + z_re, a_re * h_im + a_im * h_re + z_im)

    zero_row = jnp.zeros((1, ns), F32)
    lax.fori_loop(0, nc, step, (zero_row, zero_row))
    hc = hc_ref[...].astype(BF16)
    d_skip = d_ref[0]
    for t in range(t_len):
        acc = jnp.dot(hc, cmf_ref[t], preferred_element_type=F32)
        acc += jnp.dot(ucat_ref[:, 0:(t + 1) * LANES],
                       kdf_ref[(t_len - 1 - t) * LANES:t_len * LANES, :],
                       preferred_element_type=F32)
        acc += d_skip * uf_ref[pl.ds(t, nc, stride=t_len), :]
        y_ref[pl.ds(t, nc, stride=t_len), :] = jax.nn.gelu(acc)


def _s5_glu_kernel(y_ref, yn_ref, w_ref, b_ref, o_ref, wb_ref):
    @pl.when(pl.program_id(1) == 0)
    def _():
        wb_ref[...] = w_ref[0].astype(BF16)

    z = jnp.dot(y_ref[...].astype(BF16), wb_ref[...], preferred_element_type=F32) + b_ref[0]
    o_ref[...] = (yn_ref[...] * jax.nn.sigmoid(z)).astype(o_ref.dtype)


def _s5(proj, tables, d_skip, w_glu, b_glu, layer):
    s = proj.shape[0]
    nc = s // S5_CHUNK
    nt = S5_GROUPS // S5_TILE_GROUPS
    kd, bz, cm, a_re, a_im = tables
    u_col = 3 * BRANCH_WIDTH // LANES
    ns2 = 2 * S5_TILE_STATE
    y = pl.pallas_call(
        functools.partial(_s5_kernel, nc=nc),
        grid=(nt,),
        in_specs=[pl.BlockSpec((s, LANES), lambda c: (0, u_col + c)),
                  pl.BlockSpec((1, 1, S5_CHUNK, S5_GROUP, LANES), lambda c: (layer, c, 0, 0, 0)),
                  pl.BlockSpec((1, 1, S5_CHUNK, S5_GROUP, ns2), lambda c: (layer, c, 0, 0, 0)),
                  pl.BlockSpec((1, 1, S5_CHUNK, 2 * S5_STATE, LANES), lambda c: (layer, c, 0, 0, 0)),
                  pl.BlockSpec((1, 1, 1, S5_TILE_STATE), lambda c: (layer, c, 0, 0)),
                  pl.BlockSpec((1, 1, 1, S5_TILE_STATE), lambda c: (layer, c, 0, 0)),
                  pl.BlockSpec((1, 1, LANES), lambda c: (layer, 0, c))],
        out_specs=pl.BlockSpec((s, LANES), lambda c: (0, c)),
        out_shape=jax.ShapeDtypeStruct((s, BRANCH_WIDTH), F32),
        scratch_shapes=[pltpu.VMEM((s, LANES), F32),
                        pltpu.VMEM((nc, S5_CHUNK * LANES), BF16),
                        pltpu.VMEM((nc, ns2), F32),
                        pltpu.VMEM((nc, ns2), F32),
                        pltpu.VMEM((S5_CHUNK * LANES, LANES), BF16),
                        pltpu.VMEM((S5_CHUNK * LANES, ns2), BF16),
                        pltpu.VMEM((S5_CHUNK, ns2, LANES), BF16)],
        compiler_params=_cparams(("arbitrary",)),
        name="s5_scan",
    )(proj, kd, bz, cm, a_re, a_im, d_skip.reshape(DEPTH, 1, BRANCH_WIDTH))
    tm, tn = 512, 512
    return pl.pallas_call(
        _s5_glu_kernel,
        grid=(BRANCH_WIDTH // tn, s // tm),
        in_specs=[pl.BlockSpec((tm, BRANCH_WIDTH), lambda j, i: (i, 0)),
                  pl.BlockSpec((tm, tn), lambda j, i: (i, j)),
                  pl.BlockSpec((1, BRANCH_WIDTH, tn), lambda j, i: (layer, 0, j)),
                  pl.BlockSpec((1, 1, tn), lambda j, i: (layer, 0, j))],
        out_specs=pl.BlockSpec((tm, tn), lambda j, i: (i, j)),
        out_shape=jax.ShapeDtypeStruct((s, BRANCH_WIDTH), BF16),
        scratch_shapes=[pltpu.VMEM((BRANCH_WIDTH, tn), BF16)],
        compiler_params=_cparams(("arbitrary", "arbitrary")),
        name="s5_glu",
    )(y, y, w_glu, b_glu.reshape(DEPTH, 1, BRANCH_WIDTH))


def _gla_kernel(q_ref, k_ref, v_ref, r_ref, tail_ref, wg_ref, bg_ref, ng_ref, o_ref, st_ref):
    dk = GLA_KEY // GLA_HEADS
    dv = GLA_VAL // GLA_HEADS
    cs = GLA_CHUNK
    sub = GLA_SUB
    nt = (((1,), (1,)), ((), ()))
    tn = (((0,), (0,)), ((), ()))

    @pl.when(pl.program_id(0) == 0)
    def _():
        st_ref[...] = jnp.zeros_like(st_ref)

    gate_in = jnp.dot(tail_ref[:, 0:GLA_GATE_RANK], wg_ref[0].astype(BF16),
                      preferred_element_type=F32) + bg_ref[0]
    log_a = jax.nn.log_sigmoid(gate_in) / GLA_GATE_TAU
    ri = lax.broadcasted_iota(jnp.int32, (cs, cs), 0)
    ci = lax.broadcasted_iota(jnp.int32, (cs, cs), 1)
    tril = (ri >= ci).astype(F32)
    bcum_all = jnp.dot(tril, log_a, preferred_element_type=F32, precision=HIGHEST)
    sub_row = lax.broadcasted_iota(jnp.int32, (sub, sub), 0)
    sub_col = lax.broadcasted_iota(jnp.int32, (sub, sub), 1)

    for h in range(GLA_HEADS):
        q = q_ref[:, h * dk:(h + 1) * dk].astype(F32) * dk ** -0.5
        k = k_ref[:, h * dk:(h + 1) * dk].astype(F32)
        v_bf = v_ref[:, h * dv:(h + 1) * dv]
        bc = bcum_all[:, h * dk:(h + 1) * dk]
        state = st_ref[h]
        o_inter = lax.dot_general((q * jnp.exp(bc)).astype(BF16), state.astype(BF16), nt,
                                  preferred_element_type=F32)
        parts = []
        for i in range(cs // sub):
            lo = i * sub
            b_i, q_i, k_i = bc[lo:lo + sub], q[lo:lo + sub], k[lo:lo + sub]
            o_i = o_inter[lo:lo + sub]
            if i > 0:
                ref = bc[lo - 1:lo]
                q_h = (q_i * jnp.exp(b_i - ref)).astype(BF16)
                k_h = (k[:lo] * jnp.exp(ref - bc[:lo])).astype(BF16)
                attn = lax.dot_general(q_h, k_h, nt, preferred_element_type=F32)
                o_i = o_i + jnp.dot(attn.astype(BF16), v_bf[:lo], preferred_element_type=F32)
            diag = jnp.where(sub_col == sub_row, jnp.sum(q_i * k_i, axis=-1, keepdims=True), 0.0)
            for dlt in range(1, sub):
                b_s = pltpu.roll(b_i, dlt, 0)
                k_s = pltpu.roll(k_i, dlt, 0)
                e = jnp.exp(jnp.minimum(b_i - b_s, 0.0))
                a = jnp.sum(q_i * k_s * e, axis=-1, keepdims=True)
                diag = jnp.where(sub_col == sub_row - dlt, a, diag)
            o_i = o_i + jnp.dot(diag.astype(BF16), v_bf[lo:lo + sub], preferred_element_type=F32)
            parts.append(o_i)
        o = jnp.concatenate(parts, axis=0)
        b_last = bc[cs - 1:cs]
        k_dec = (k * jnp.exp(b_last - bc)).astype(BF16)
        st_ref[h] = state * jnp.exp(b_last) + lax.dot_general(v_bf, k_dec, tn,
                                                             preferred_element_type=F32)
        o = o * lax.rsqrt(jnp.mean(o * o, axis=-1, keepdims=True) + NORM_EPS)
        o = o * ng_ref[0, :, h * dv:(h + 1) * dv]
        r = r_ref[:, h * dv:(h + 1) * dv].astype(F32)
        o_ref[:, h * dv:(h + 1) * dv] = (o * (r * jax.nn.sigmoid(r))).astype(o_ref.dtype)


def _gla(proj, w_gate, b_gate, norm_g, layer):
    s = proj.shape[0]
    cs = GLA_CHUNK
    return pl.pallas_call(
        _gla_kernel,
        grid=(s // cs,),
        in_specs=[pl.BlockSpec((cs, GLA_KEY), lambda c: (c, 4 * BRANCH_WIDTH // GLA_KEY)),
                  pl.BlockSpec((cs, GLA_KEY), lambda c: (c, 4 * BRANCH_WIDTH // GLA_KEY + 1)),
                  pl.BlockSpec((cs, GLA_VAL), lambda c: (c, 5)),
                  pl.BlockSpec((cs, GLA_VAL), lambda c: (c, 6)),
                  pl.BlockSpec((cs, TAIL_W), lambda c: (c, TAIL_COL // TAIL_W)),
                  pl.BlockSpec((1, GLA_GATE_RANK, GLA_KEY), lambda c: (layer, 0, 0)),
                  pl.BlockSpec((1, 1, GLA_KEY), lambda c: (layer, 0, 0)),
                  pl.BlockSpec((1, 1, GLA_VAL), lambda c: (layer, 0, 0))],
        out_specs=pl.BlockSpec((cs, GLA_VAL), lambda c: (c, 0)),
        out_shape=jax.ShapeDtypeStruct((s, GLA_VAL), BF16),
        scratch_shapes=[pltpu.VMEM((GLA_HEADS, GLA_VAL // GLA_HEADS, GLA_KEY // GLA_HEADS), F32)],
        compiler_params=_cparams(("arbitrary",)),
        name="gla",
    )(proj, proj, proj, proj, proj, w_gate, b_gate.reshape(DEPTH, 1, GLA_KEY),
      norm_g.reshape(DEPTH, 1, GLA_VAL))


def _merge_kernel(ya_ref, ys_ref, yg_ref, tail_ref, wb_ref, wg0_ref, wg1_ref, wg2_ref,
                  bg0_ref, bg1_ref, bg2_ref, o_ref, wbb_ref, wgb_ref):
    wg_refs = (wg0_ref, wg1_ref, wg2_ref)
    bg_refs = (bg0_ref, bg1_ref, bg2_ref)

    @pl.when(pl.program_id(1) == 0)
    def _():
        wbb_ref[...] = wb_ref[0].astype(BF16)
        for n in range(N_BRANCH):
            wgb_ref[n] = wg_refs[n][0].astype(BF16)

    mz = tail_ref[:, GLA_GATE_RANK:GLA_GATE_RANK + MERGE_RANK]
    acc = None
    for n, y_ref in enumerate((ya_ref, ys_ref, yg_ref)):
        up = jnp.dot(y_ref[...], wbb_ref[n], preferred_element_type=F32)
        gate = jax.nn.sigmoid(jnp.dot(mz, wgb_ref[n], preferred_element_type=F32) + bg_refs[n][0])
        acc = gate * up if acc is None else acc + gate * up
    o_ref[...] = acc.astype(o_ref.dtype)


def _merge(y_att, y_s5, y_gla, proj, w_branch, w_merge_gate, b_merge_gate, layer, tm=512, tn=512):
    s = proj.shape[0]
    n_col = D_MODEL // tn
    bg = b_merge_gate.reshape(DEPTH, 1, N_BRANCH * D_MODEL)
    ybs = pl.BlockSpec((tm, BRANCH_WIDTH), lambda j, i: (i, 0))

    def gate_col(n):
        return lambda j, i: (layer, 0, n * n_col + j)

    return pl.pallas_call(
        _merge_kernel,
        grid=(n_col, s // tm),
        in_specs=[ybs, ybs, ybs,
                  pl.BlockSpec((tm, TAIL_W), lambda j, i: (i, TAIL_COL // TAIL_W)),
                  pl.BlockSpec((1, N_BRANCH, BRANCH_WIDTH, tn), lambda j, i: (layer, 0, 0, j))]
                 + [pl.BlockSpec((1, MERGE_RANK, tn), gate_col(n)) for n in range(N_BRANCH)]
                 + [pl.BlockSpec((1, 1, tn), gate_col(n)) for n in range(N_BRANCH)],
        out_specs=pl.BlockSpec((tm, tn), lambda j, i: (i, j)),
        out_shape=jax.ShapeDtypeStruct((s, D_MODEL), BF16),
        scratch_shapes=[pltpu.VMEM((N_BRANCH, BRANCH_WIDTH, tn), BF16),
                        pltpu.VMEM((N_BRANCH, MERGE_RANK, tn), BF16)],
        compiler_params=_cparams(("arbitrary", "arbitrary")),
        name="merge",
    )(y_att, y_s5, y_gla, proj, w_branch, w_merge_gate, w_merge_gate, w_merge_gate, bg, bg, bg)


def _swiglu_kernel(a_ref, w1_ref, w3_ref, o_ref, w1b_ref, w3b_ref):
    @pl.when(pl.program_id(1) == 0)
    def _():
        w1b_ref[...] = w1_ref[0].astype(BF16)
        w3b_ref[...] = w3_ref[0].astype(BF16)

    a = a_ref[...]
    g = jnp.dot(a, w1b_ref[...], preferred_element_type=F32)
    u = jnp.dot(a, w3b_ref[...], preferred_element_type=F32)
    o_ref[...] = (g * jax.nn.sigmoid(g) * u).astype(o_ref.dtype)


def _swiglu_hidden(a, w1_stack, w3_stack, layer, tm=1024, tn=256):
    m, k = a.shape
    n = w1_stack.shape[2]
    wspec = pl.BlockSpec((1, k, tn), lambda j, i: (layer, 0, j))
    return pl.pallas_call(
        _swiglu_kernel,
        grid=(n // tn, m // tm),
        in_specs=[pl.BlockSpec((tm, k), lambda j, i: (i, 0)), wspec, wspec],
        out_specs=pl.BlockSpec((tm, tn), lambda j, i: (i, j)),
        out_shape=jax.ShapeDtypeStruct((m, n), BF16),
        scratch_shapes=[pltpu.VMEM((k, tn), BF16), pltpu.VMEM((k, tn), BF16)],
        compiler_params=_cparams(("arbitrary", "arbitrary")),
        name="swiglu_hidden",
    )(a, w1_stack, w3_stack)


def _moe_plan(route):
    s = route.shape[0]
    tile = MOE_TILE
    n_tiles = (TOP_K * s) // tile + N_EXPERTS
    ids = route[:, ROUTE_ID:ROUTE_ID + TOP_K].astype(jnp.int32)
    onehot = jnp.sum(jax.nn.one_hot(ids, N_EXPERTS, dtype=jnp.int32), axis=1)
    before = jnp.cumsum(onehot, axis=0) - onehot
    counts = jnp.sum(onehot, axis=0)
    padded = (counts + tile - 1) // tile * tile
    ends = jnp.cumsum(padded)
    offsets = ends - padded
    pos = offsets[ids] + jnp.take_along_axis(before, ids, axis=1)
    tile_start = jnp.arange(n_tiles, dtype=jnp.int32) * tile
    tile_expert = jnp.minimum(jnp.sum(tile_start[:, None] >= ends[None, :], axis=1), N_EXPERTS - 1)
    n_active = (ends[-1] // tile).reshape(1)
    tokens = jnp.broadcast_to(jnp.arange(s, dtype=jnp.int32)[:, None], pos.shape)
    row_token = jnp.zeros((n_tiles * tile,), jnp.int32).at[pos.reshape(-1)].set(tokens.reshape(-1))
    pos1 = pos[:, 0].reshape(s // LANES, LANES)
    pos2 = pos[:, 1].reshape(s // LANES, LANES)
    return (pos1, pos2, row_token.reshape(-1, LANES), tile_expert.astype(jnp.int32),
            n_active.astype(jnp.int32), n_tiles)


def _moe_dispatch_kernel(rt_ref, na_ref, h_ref, o_ref, buf_ref, sem):
    i = pl.program_id(0)
    n_active = na_ref[0]

    def start_tile(tile):
        slot = tile % 2
        _start_rows(lambda r: (_row_copy(h_ref, buf_ref.at[slot], sem.at[slot],
                                         _smem_at(rt_ref, tile * MOE_TILE + r), r),), MOE_TILE)

    @pl.when(jnp.logical_and(i == 0, n_active > 0))
    def _():
        start_tile(i)

    @pl.when(i + 1 < n_active)
    def _():
        start_tile(i + 1)

    @pl.when(i < n_active)
    def _():
        slot = i % 2
        _wait_rows(h_ref, buf_ref.at[slot], sem.at[slot])
        o_ref[...] = buf_ref[slot].astype(o_ref.dtype)

    @pl.when(i >= n_active)
    def _():
        o_ref[...] = jnp.zeros_like(o_ref)


def _moe_dispatch(h, row_token, n_active, n_tiles):
    tile = MOE_TILE
    return pl.pallas_call(
        _moe_dispatch_kernel,
        grid_spec=pltpu.PrefetchScalarGridSpec(
            num_scalar_prefetch=2, grid=(n_tiles,),
            in_specs=[pl.BlockSpec(memory_space=pl.ANY)],
            out_specs=pl.BlockSpec((tile, D_MODEL), lambda i, rt, na: (i, 0)),
            scratch_shapes=[pltpu.VMEM((2, tile, D_MODEL), F32), pltpu.SemaphoreType.DMA((2,))]),
        out_shape=jax.ShapeDtypeStruct((n_tiles * tile, D_MODEL), BF16),
        compiler_params=_cparams(("arbitrary",)),
        name="moe_dispatch",
    )(row_token, n_active, h)


def _new_expert_panel(te_ref):
    i = pl.program_id(1)
    return jnp.logical_or(i == 0, te_ref[i] != te_ref[jnp.maximum(i - 1, 0)])


def _moe_hidden_kernel(te_ref, na_ref, a_ref, w1_ref, w3_ref, o_ref, w1b_ref, w3b_ref):
    @pl.when(_new_expert_panel(te_ref))
    def _():
        w1b_ref[...] = w1_ref[0, 0].astype(BF16)
        w3b_ref[...] = w3_ref[0, 0].astype(BF16)

    @pl.when(pl.program_id(1) < na_ref[0])
    def _():
        a = a_ref[...]
        g = jnp.dot(a, w1b_ref[...], preferred_element_type=F32)
        u = jnp.dot(a, w3b_ref[...], preferred_element_type=F32)
        o_ref[...] = (g * jax.nn.sigmoid(g) * u).astype(o_ref.dtype)

    @pl.when(pl.program_id(1) >= na_ref[0])
    def _():
        o_ref[...] = jnp.zeros_like(o_ref)


def _moe_down_kernel(te_ref, na_ref, a_ref, w_ref, o_ref, wb_ref):
    @pl.when(_new_expert_panel(te_ref))
    def _():
        wb_ref[...] = w_ref[0, 0].astype(BF16)

    @pl.when(pl.program_id(1) < na_ref[0])
    def _():
        o_ref[...] = jnp.dot(a_ref[...], wb_ref[...], preferred_element_type=F32).astype(o_ref.dtype)

    @pl.when(pl.program_id(1) >= na_ref[0])
    def _():
        o_ref[...] = jnp.zeros_like(o_ref)


def _moe_experts(xs, tile_expert, n_active, w1_stack, w3_stack, w2_stack, layer, n_tiles,
                 tn_hidden=256, tn_down=1024):
    tile = MOE_TILE
    rows = n_tiles * tile
    w_in_spec = pl.BlockSpec((1, 1, D_MODEL, tn_hidden), lambda n, i, te, na: (layer, te[i], 0, n))
    hid = pl.pallas_call(
        _moe_hidden_kernel,
        grid_spec=pltpu.PrefetchScalarGridSpec(
            num_scalar_prefetch=2, grid=(D_FF_EXPERT // tn_hidden, n_tiles),
            in_specs=[pl.BlockSpec((tile, D_MODEL), lambda n, i, te, na: (i, 0)), w_in_spec, w_in_spec],
            out_specs=pl.BlockSpec((tile, tn_hidden), lambda n, i, te, na: (i, n)),
            scratch_shapes=[pltpu.VMEM((D_MODEL, tn_hidden), BF16), pltpu.VMEM((D_MODEL, tn_hidden), BF16)]),
        out_shape=jax.ShapeDtypeStruct((rows, D_FF_EXPERT), BF16),
        compiler_params=_cparams(("arbitrary", "arbitrary")),
        name="moe_hidden",
    )(tile_expert, n_active, xs, w1_stack, w3_stack)
    return pl.pallas_call(
        _moe_down_kernel,
        grid_spec=pltpu.PrefetchScalarGridSpec(
            num_scalar_prefetch=2, grid=(D_MODEL // tn_down, n_tiles),
            in_specs=[pl.BlockSpec((tile, D_FF_EXPERT), lambda n, i, te, na: (i, 0)),
                      pl.BlockSpec((1, 1, D_FF_EXPERT, tn_down), lambda n, i, te, na: (layer, te[i], 0, n))],
            out_specs=pl.BlockSpec((tile, tn_down), lambda n, i, te, na: (i, n)),
            scratch_shapes=[pltpu.VMEM((D_FF_EXPERT, tn_down), BF16)]),
        out_shape=jax.ShapeDtypeStruct((rows, D_MODEL), F32),
        compiler_params=_cparams(("arbitrary", "arbitrary")),
        name="moe_down",
    )(tile_expert, n_active, hid, w2_stack)


def kernel(x, c, w_cond, b_cond, rel_bias, w_mod, b_mod, w_in, s5_lambda_re, s5_lambda_im, s5_log_dt, s5_b_re, s5_b_im, s5_c_re, s5_c_im, s5_d, s5_w_glu, s5_b_glu, gla_w_gate, gla_b_gate, gla_norm_g, w_branch, w_merge_gate, b_merge_gate, w_out, ln1_g, ln1_b, ffn_w1, ffn_w3, ffn_w2, router_w, router_b, exp_w1, exp_w3, exp_w2, ln2_g, ln2_b):
    bsz, seq, _ = x.shape
    assert bsz == 1
    mod = _conditioning(c, w_cond, b_cond, w_mod, b_mod)
    xs = x.reshape(seq, D_MODEL)
    w_in_t = jnp.swapaxes(w_in, 1, 2)
    hm = _modulate(xs, mod[0, 1], mod[0, 0])
    s5_tables = jax.vmap(_s5_tables)(s5_lambda_re, s5_lambda_im, s5_log_dt, s5_b_re, s5_b_im, s5_c_re, s5_c_im)
    for l in range(DEPTH):
        shift_f, scale_f, gate_m, gate_f = mod[l, 3], mod[l, 4], mod[l, 2], mod[l, 5]
        proj = _matmul(hm, w_in_t, l, tm=512, tn=512, n_out=D_IN_PAD, w_transposed=True,
                       name="in_proj")
        y_att = _moba(proj, rel_bias)
        y_s5 = _s5(proj, s5_tables, s5_d, s5_w_glu, s5_b_glu, l)
        y_gla = _gla(proj, gla_w_gate, gla_b_gate, gla_norm_g, l)
        merged = _merge(y_att, y_s5, y_gla, proj, w_branch, w_merge_gate, b_merge_gate, l)
        y = _matmul(merged, w_out, l, tm=512, tn=512, name="out_proj")
        dense = l % 2 == 0
        router = None if dense else (router_w[l // 2], router_b[l // 2])
        outs = _deepnorm_ln(xs, y, gate_m, ln1_g[l], ln1_b[l], nxt=(scale_f, shift_f), router=router)
        xs, hf = outs[0], outs[1]
        if dense:
            hid = _swiglu_hidden(hf, ffn_w1, ffn_w3, l // 2)
            f = _matmul(hid, ffn_w2, l // 2, tm=256, tn=512, name="ffn_down")
            moe = None
        else:
            route = outs[2]
            pos1, pos2, row_token, tile_expert, n_active, n_tiles = _moe_plan(route)
            xsorted = _moe_dispatch(hf, row_token, n_active, n_tiles)
            ys = _moe_experts(xsorted, tile_expert, n_active, exp_w1, exp_w3, exp_w2, l // 2, n_tiles)
            f = None
            moe = (ys, pos1, pos2, route)
        nxt = (mod[l + 1, 1], mod[l + 1, 0]) if l + 1 < DEPTH else None
        outs = _deepnorm_ln(xs, f, gate_f, ln2_g[l], ln2_b[l], nxt=nxt, moe=moe)
        xs = outs[0]
        if nxt is not None:
            hm = outs[1]
    return xs.reshape(bsz, seq, D_MODEL)
```

```python
import functools
import math

import jax
import jax.numpy as jnp
from jax import lax
from jax.experimental import pallas as pl
from jax.experimental.pallas import tpu as pltpu

F32 = jnp.float32
BF16 = jnp.bfloat16
HIGHEST = lax.Precision.HIGHEST

D_MODEL = 4096
DEPTH = 4
BRANCH_WIDTH = 1024
N_BRANCH = 3
ATT_HEADS = 8
ATT_HEAD_DIM = 128
MOBA_BLOCK = 256
MOBA_TOPK = 3
MOBA_HEAD_GROUP = 4
MOBA_ONES_ROWS = 16
REL_BUCKETS = 32
REL_MAX_DIST = 128
S5_GROUP = 16
S5_GROUPS = 64
S5_STATE = 64
S5_CHUNK = 16
S5_TILE_GROUPS = 8
S5_TILE_STATE = S5_TILE_GROUPS * S5_STATE
GLA_HEADS = 4
GLA_KEY = 512
GLA_VAL = 1024
GLA_GATE_RANK = 16
GLA_GATE_TAU = 16.0
GLA_CHUNK = 64
GLA_SUB = 16
MERGE_RANK = 256
COND_RANK = 512
D_FF = 8192
N_EXPERTS = 8
TOP_K = 2
ROUTE_ID = 8
ROUTE_W = 10
MOE_TILE = 256
DMA_ISSUE_UNROLL = 8
D_FF_EXPERT = 1792
DN_ALPHA = (2 * DEPTH) ** 0.25
LN_EPS = 1e-5
NORM_EPS = 1e-6

D_IN = 7440
D_IN_PAD = 7680
TAIL_COL = 7168
TAIL_W = 512
LANES = 128
VMEM_LIMIT = 56 * 1024 * 1024
NEG_INF = float("-inf")


def _cparams(sem):
    return pltpu.CompilerParams(dimension_semantics=sem, vmem_limit_bytes=VMEM_LIMIT)


def _cond_kernel(c_ref, w_ref, b_ref, o_ref):
    z = jnp.dot(c_ref[...], w_ref[...], preferred_element_type=F32, precision=HIGHEST) + b_ref[...]
    o_ref[...] = z * jax.nn.sigmoid(z)


def _mod_kernel(cond_ref, w_ref, b_ref, o_ref):
    o_ref[0] = jnp.dot(cond_ref[...], w_ref[0], preferred_element_type=F32,
                       precision=HIGHEST) + b_ref[0]


def _conditioning(c, w_cond, b_cond, w_mod, b_mod):
    c8 = jnp.broadcast_to(c, (8, D_MODEL))
    cond = pl.pallas_call(
        _cond_kernel,
        out_shape=jax.ShapeDtypeStruct((8, COND_RANK), F32),
        compiler_params=_cparams(None),
        name="cond",
    )(c8, w_cond, b_cond.reshape(1, COND_RANK))
    n_mod = 6 * D_MODEL
    tn = 3072
    mod = pl.pallas_call(
        _mod_kernel,
        grid=(DEPTH, n_mod // tn),
        in_specs=[pl.BlockSpec((8, COND_RANK), lambda l, n: (0, 0)),
                  pl.BlockSpec((1, COND_RANK, tn), lambda l, n: (l, 0, n)),
                  pl.BlockSpec((1, 1, tn), lambda l, n: (l, 0, n))],
        out_specs=pl.BlockSpec((1, 8, tn), lambda l, n: (l, 0, n)),
        out_shape=jax.ShapeDtypeStruct((DEPTH, 8, n_mod), F32),
        compiler_params=_cparams(("arbitrary", "arbitrary")),
        name="mod",
    )(cond, w_mod, b_mod.reshape(DEPTH, 1, n_mod))
    return mod[:, 0, :].reshape(DEPTH, 6, 1, D_MODEL)


def _modulate_kernel(x_ref, scale_ref, shift_ref, o_ref):
    o_ref[...] = (x_ref[...] * (1.0 + scale_ref[...]) + shift_ref[...]).astype(o_ref.dtype)


def _modulate(x, scale, shift, tm=512):
    s = x.shape[0]
    vec = pl.BlockSpec((1, D_MODEL), lambda m: (0, 0))
    return pl.pallas_call(
        _modulate_kernel,
        grid=(s // tm,),
        in_specs=[pl.BlockSpec((tm, D_MODEL), lambda m: (m, 0)), vec, vec],
        out_specs=pl.BlockSpec((tm, D_MODEL), lambda m: (m, 0)),
        out_shape=jax.ShapeDtypeStruct((s, D_MODEL), BF16),
        compiler_params=_cparams(("arbitrary",)),
        name="modulate",
    )(x, scale, shift)


def _route_top2(logits):
    lane = lax.broadcasted_iota(jnp.int32, logits.shape, 1)
    m1 = jnp.max(logits, axis=-1, keepdims=True)
    i1 = jnp.min(jnp.where(logits == m1, lane, LANES), axis=-1, keepdims=True)
    rest = jnp.where(lane == i1, NEG_INF, logits)
    m2 = jnp.max(rest, axis=-1, keepdims=True)
    i2 = jnp.min(jnp.where(rest == m2, lane, LANES), axis=-1, keepdims=True)
    e2 = jnp.exp(m2 - m1)
    denom = 1.0 + e2
    rec = jnp.where(lane == ROUTE_ID, i1.astype(F32), 0.0)
    rec = jnp.where(lane == ROUTE_ID + 1, i2.astype(F32), rec)
    rec = jnp.where(lane == ROUTE_W, 1.0 / denom, rec)
    return jnp.where(lane == ROUTE_W + 1, e2 / denom, rec)


def _row_copy(src_ref, dst_ref, sem, src_row, dst_row):
    return pltpu.make_async_copy(src_ref.at[pl.ds(src_row, 1)], dst_ref.at[pl.ds(dst_row, 1)], sem)


def _smem_at(ref, i):
    return ref[i // LANES, i % LANES]


def _start_rows(copies_of, n_rows):
    def start(r, carry):
        for cp in copies_of(r):
            cp.start()
        return carry

    lax.fori_loop(0, n_rows, start, 0, unroll=DMA_ISSUE_UNROLL)


def _wait_rows(src_ref, dst_ref, sem):
    pltpu.make_async_copy(src_ref.at[pl.ds(0, dst_ref.shape[0])], dst_ref, sem).wait()


def _ln_kernel(*refs, has_next, has_router, moe_combine, tm):
    pos = 0
    if moe_combine:
        pos1_ref, pos2_ref, x_ref, route_ref, ys_ref = refs[:5]
        pos = 5
    else:
        x_ref, y_ref = refs[:2]
        pos = 2
    gate_ref, g_ref, b_ref = refs[pos:pos + 3]
    pos += 3
    if has_next:
        scale_ref, shift_ref = refs[pos:pos + 2]
        pos += 2
    if has_router:
        rw_ref, rb_ref = refs[pos:pos + 2]
        pos += 2
    xo_ref = refs[pos]
    pos += 1
    if moe_combine:
        buf1_ref, buf2_ref, sem = refs[-3:]
        base = pl.program_id(0) * tm

        _start_rows(lambda r: (_row_copy(ys_ref, buf1_ref, sem, _smem_at(pos1_ref, base + r), r),
                               _row_copy(ys_ref, buf2_ref, sem, _smem_at(pos2_ref, base + r), r)), tm)
        _wait_rows(ys_ref, buf1_ref, sem)
        _wait_rows(ys_ref, buf2_ref, sem)
        route = route_ref[...]
        lane = lax.broadcasted_iota(jnp.int32, route.shape, 1)
        w1 = jnp.sum(jnp.where(lane == ROUTE_W, route, 0.0), axis=-1, keepdims=True)
        w2 = jnp.sum(jnp.where(lane == ROUTE_W + 1, route, 0.0), axis=-1, keepdims=True)
        y = w1 * buf1_ref[...] + w2 * buf2_ref[...]
    else:
        y = y_ref[...].astype(F32)
    z = DN_ALPHA * x_ref[...] + (1.0 + gate_ref[...]) * y
    mu = jnp.mean(z, axis=-1, keepdims=True)
    zc = z - mu
    var = jnp.mean(zc * zc, axis=-1, keepdims=True)
    xn = zc * lax.rsqrt(var + LN_EPS) * g_ref[...] + b_ref[...]
    xo_ref[...] = xn
    if has_next:
        ho_ref = refs[pos]
        pos += 1
        h = xn * (1.0 + scale_ref[...]) + shift_ref[...]
        ho_ref[...] = h.astype(ho_ref.dtype)
        if has_router:
            co_ref = refs[pos]
            logits = jnp.dot(h, rw_ref[...], preferred_element_type=F32, precision=HIGHEST)
            lane = lax.broadcasted_iota(jnp.int32, logits.shape, 1)
            logits = jnp.where(lane < N_EXPERTS, logits + rb_ref[...], NEG_INF)
            co_ref[...] = _route_top2(logits)


def _deepnorm_ln(x, y, gate, g, b, nxt=None, router=None, moe=None, tm=256):
    s = x.shape[0]
    n_pre = 0 if moe is None else 2
    imap = (lambda m: (m, 0)) if moe is None else (lambda m, p1, p2: (m, 0))
    vmap = (lambda m: (0, 0)) if moe is None else (lambda m, p1, p2: (0, 0))
    row = pl.BlockSpec((tm, D_MODEL), imap)
    vec = pl.BlockSpec((1, D_MODEL), vmap)
    scratch = []
    if moe is None:
        args = [x, y]
        in_specs = [row, row]
    else:
        ys, pos1, pos2, route = moe
        args = [pos1, pos2, x, route, ys]
        in_specs = [row, pl.BlockSpec((tm, LANES), imap), pl.BlockSpec(memory_space=pl.ANY)]
        scratch = [pltpu.VMEM((tm, D_MODEL), F32), pltpu.VMEM((tm, D_MODEL), F32),
                   pltpu.SemaphoreType.DMA(())]
    args += [gate, g.reshape(1, D_MODEL), b.reshape(1, D_MODEL)]
    in_specs += [vec, vec, vec]
    out_shape = [jax.ShapeDtypeStruct((s, D_MODEL), F32)]
    out_specs = [row]
    if nxt is not None:
        args += [nxt[0], nxt[1]]
        in_specs += [vec, vec]
        out_shape.append(jax.ShapeDtypeStruct((s, D_MODEL), BF16 if router is None else F32))
        out_specs.append(row)
    if router is not None:
        rw, rb = router
        rw_pad = jnp.pad(rw, ((0, 0), (0, LANES - N_EXPERTS)))
        rb_pad = jnp.pad(rb, (0, LANES - N_EXPERTS)).reshape(1, LANES)
        args += [rw_pad, rb_pad]
        in_specs += [pl.BlockSpec((D_MODEL, LANES), vmap), pl.BlockSpec((1, LANES), vmap)]
        out_shape.append(jax.ShapeDtypeStruct((s, LANES), F32))
        out_specs.append(pl.BlockSpec((tm, LANES), imap))
    return pl.pallas_call(
        functools.partial(_ln_kernel, has_next=nxt is not None, has_router=router is not None,
                          moe_combine=moe is not None, tm=tm),
        grid_spec=pltpu.PrefetchScalarGridSpec(
            num_scalar_prefetch=n_pre, grid=(s // tm,), in_specs=in_specs, out_specs=out_specs,
            scratch_shapes=scratch),
        out_shape=out_shape,
        compiler_params=_cparams(("arbitrary",)),
        name="deepnorm_ln",
    )(*args)


def _matmul_kernel(a_ref, w_ref, o_ref, wb_ref, *, n_valid, tn, w_transposed):
    @pl.when(pl.program_id(1) == 0)
    def _():
        w = w_ref[0]
        wb_ref[...] = (w.T if w_transposed else w).astype(BF16)

    acc = jnp.dot(a_ref[...], wb_ref[...], preferred_element_type=F32)
    if n_valid is not None:
        col = pl.program_id(0) * tn + lax.broadcasted_iota(jnp.int32, acc.shape, 1)
        acc = jnp.where(col < n_valid, acc, 0.0)
    o_ref[...] = acc.astype(o_ref.dtype)


def _matmul(a, w_stack, layer, tm, tn, out_dtype=BF16, n_out=None, w_transposed=False, name="matmul"):
    m, k = a.shape
    n = w_stack.shape[1 if w_transposed else 2]
    n_out = n if n_out is None else n_out
    if w_transposed:
        w_spec = pl.BlockSpec((1, tn, k), lambda j, i: (layer, j, 0))
    else:
        w_spec = pl.BlockSpec((1, k, tn), lambda j, i: (layer, 0, j))
    return pl.pallas_call(
        functools.partial(_matmul_kernel, n_valid=None if n_out == n else n, tn=tn,
                          w_transposed=w_transposed),
        grid=(n_out // tn, m // tm),
        in_specs=[pl.BlockSpec((tm, k), lambda j, i: (i, 0)), w_spec],
        out_specs=pl.BlockSpec((tm, tn), lambda j, i: (i, j)),
        out_shape=jax.ShapeDtypeStruct((m, n_out), out_dtype),
        scratch_shapes=[pltpu.VMEM((k, tn), BF16)],
        compiler_params=_cparams(("arbitrary", "arbitrary")),
        name=name,
    )(a, w_stack)


def _rel_bucket(dist):
    n = jnp.maximum(dist, 0)
    max_exact = REL_BUCKETS // 2
    nf = jnp.maximum(n, 1).astype(F32)
    large = max_exact + (jnp.log(nf / max_exact) / math.log(REL_MAX_DIST / max_exact)
                         * (REL_BUCKETS - max_exact)).astype(jnp.int32)
    large = jnp.minimum(large, REL_BUCKETS - 1)
    return jnp.where(n < max_exact, n, large)


def _moba_kernel(relb_ref, q_ref, k_ref, vt_ref, o_ref,
                 kmean_ref, bown_ref, bprev_ref, sel_ref, m_ref, acc_ref, *, nb):
    g = pl.program_id(0)
    j = pl.program_id(1)
    blk = MOBA_BLOCK
    dh = ATT_HEAD_DIM
    scale = dh ** -0.5
    ln2 = math.log(2.0)
    nt = (((1,), (1,)), ((), ()))
    heads = range(MOBA_HEAD_GROUP)
    key_i = lax.broadcasted_iota(jnp.int32, (blk, blk), 0)
    qry_i = lax.broadcasted_iota(jnp.int32, (blk, blk), 1)

    @pl.when(j == 0)
    def _():
        for hh in heads:
            head = g * MOBA_HEAD_GROUP + hh
            kf = k_ref[:, hh * dh:(hh + 1) * dh].astype(F32).reshape(nb, blk, dh)
            kmean_ref[hh] = jnp.mean(kf, axis=1)
            for ref, off in ((bown_ref, 0), (bprev_ref, blk)):
                bucket = _rel_bucket(qry_i - key_i + off)
                bias = jnp.zeros((blk, blk), F32)
                for b in range(REL_BUCKETS):
                    bias = jnp.where(bucket == b, relb_ref[head, b], bias)
                ref[hh] = bias

    row0 = pl.multiple_of(j * blk, blk)
    blk_i = lax.broadcasted_iota(jnp.int32, (nb, blk), 0)
    q2 = []
    for hh in heads:
        q = q_ref[:, hh * dh:(hh + 1) * dh]
        q2.append((q.astype(F32) * (scale / ln2)).astype(BF16))
        score = lax.dot_general(kmean_ref[hh], q.astype(F32), nt,
                                preferred_element_type=F32, precision=HIGHEST)
        sc = jnp.where(blk_i < j, score, NEG_INF)
        seladd = jnp.full((nb, blk), NEG_INF, F32)
        for _ in range(MOBA_TOPK):
            mx = jnp.max(sc, axis=0, keepdims=True)
            cand = jnp.where(sc == mx, blk_i, nb)
            cand = jnp.where(mx > NEG_INF, cand, nb)
            idx = jnp.min(cand, axis=0, keepdims=True)
            pick = blk_i == idx
            seladd = jnp.where(pick, 0.0, seladd)
            sc = jnp.where(pick, NEG_INF, sc)
        sel_ref[hh] = seladd

    def scores(kb_row0, hh):
        return lax.dot_general(k_ref[pl.ds(kb_row0, blk), hh * dh:(hh + 1) * dh], q2[hh], nt,
                               preferred_element_type=F32)

    def weighted_values(kb, hh, p):
        return jnp.dot(vt_ref[kb, hh], p.astype(BF16), preferred_element_type=F32)

    kb_prev = jnp.maximum(j - 1, 0)
    r_prev = pl.multiple_of(kb_prev * blk, blk)
    s_own = [jnp.where(key_i <= qry_i, scores(row0, hh) * ln2 + bown_ref[hh], NEG_INF) for hh in heads]
    s_prev = [scores(r_prev, hh) * ln2 + bprev_ref[hh] for hh in heads]
    m_own = [jnp.max(s, axis=0, keepdims=True) for s in s_own]
    m_prev = [jnp.max(s, axis=0, keepdims=True) for s in s_prev]
    acc_own = [weighted_values(j, hh, jnp.exp(s_own[hh] - m_own[hh])) for hh in heads]
    acc_prev = [weighted_values(kb_prev, hh, jnp.exp(s_prev[hh] - m_prev[hh])) for hh in heads]
    for hh in heads:
        m_blk = m_prev[hh] + sel_ref[hh, pl.ds(kb_prev, 1), :]
        m_new = jnp.maximum(m_own[hh], m_blk)
        m_ref[hh] = m_new
        acc_ref[hh] = jnp.exp(m_own[hh] - m_new) * acc_own[hh] + jnp.exp(m_blk - m_new) * acc_prev[hh]

    def merge(hh, kb, m_blk, acc_blk):
        m_blk = m_blk + sel_ref[hh, pl.ds(kb, 1), :]
        m_old = m_ref[hh]
        m_new = jnp.maximum(m_old, m_blk)
        m_ref[hh] = m_new
        acc_ref[hh] = jnp.exp(m_old - m_new) * acc_ref[hh] + jnp.exp(m_blk - m_new) * acc_blk

    def far_blocks(pair, carry):
        chains = [(2 * pair + second, hh, second) for second in (0, 1) for hh in heads]
        s2 = [scores(pl.multiple_of(kb * blk, blk), hh) for kb, hh, _ in chains]
        m2 = [jnp.max(s, axis=0, keepdims=True) for s in s2]
        acc_blk = [weighted_values(kb, hh, jnp.exp2(s - m)) for (kb, hh, _), s, m in zip(chains, s2, m2)]
        second_ok = jnp.where(2 * pair + 1 < j - 1, 0.0, NEG_INF)
        for (kb, hh, second), m, acc in zip(chains, m2, acc_blk):
            m_blk = m * ln2 + relb_ref[g * MOBA_HEAD_GROUP + hh, REL_BUCKETS - 1]
            merge(hh, kb, m_blk + second_ok if second else m_blk, acc)
        return carry

    lax.fori_loop(0, j // 2, far_blocks, 0)
    for hh in heads:
        acc = acc_ref[hh]
        o_ref[:, hh * dh:(hh + 1) * dh] = (acc[0:dh] / acc[dh:dh + 1]).T.astype(o_ref.dtype)


def _moba(proj, rel_bias):
    s = proj.shape[0]
    nb = s // MOBA_BLOCK
    hg = MOBA_HEAD_GROUP
    gw = hg * ATT_HEAD_DIM
    n_groups = ATT_HEADS // hg
    dhp = ATT_HEAD_DIM + MOBA_ONES_ROWS
    v_t = proj[:, 2 * BRANCH_WIDTH:3 * BRANCH_WIDTH].reshape(nb, MOBA_BLOCK, ATT_HEADS, ATT_HEAD_DIM)
    v_t = jnp.concatenate([v_t.transpose(0, 2, 3, 1),
                           jnp.ones((nb, ATT_HEADS, MOBA_ONES_ROWS, MOBA_BLOCK), BF16)], axis=2)
    return pl.pallas_call(
        functools.partial(_moba_kernel, nb=nb),
        grid=(n_groups, nb),
        in_specs=[pl.BlockSpec(memory_space=pltpu.SMEM),
                  pl.BlockSpec((MOBA_BLOCK, gw), lambda g, j: (j, g)),
                  pl.BlockSpec((s, gw), lambda g, j: (0, n_groups + g)),
                  pl.BlockSpec((nb, hg, dhp, MOBA_BLOCK), lambda g, j: (0, g, 0, 0))],
        out_specs=pl.BlockSpec((MOBA_BLOCK, gw), lambda g, j: (j, g)),
        out_shape=jax.ShapeDtypeStruct((s, BRANCH_WIDTH), BF16),
        scratch_shapes=[pltpu.VMEM((hg, nb, ATT_HEAD_DIM), F32),
                        pltpu.VMEM((hg, MOBA_BLOCK, MOBA_BLOCK), F32),
                        pltpu.VMEM((hg, MOBA_BLOCK, MOBA_BLOCK), F32),
                        pltpu.VMEM((hg, nb, MOBA_BLOCK), F32),
                        pltpu.VMEM((hg, 1, MOBA_BLOCK), F32),
                        pltpu.VMEM((hg, dhp, MOBA_BLOCK), F32)],
        compiler_params=_cparams(("arbitrary", "arbitrary")),
        name="moba",
    )(rel_bias.T, proj, proj, v_t)


def _s5_tables(lam_re, lam_im, log_dt, b_re, b_im, c_re, c_im):
    t_len = S5_CHUNK
    g_cnt, p_cnt, h_cnt = S5_GROUPS, S5_STATE, S5_GROUP
    tg = S5_TILE_GROUPS
    nt = g_cnt // tg
    dt = jnp.exp(log_dt)[:, None]
    ar, ai = lam_re * dt, lam_im * dt

    def lam_pow(steps):
        st = steps.astype(F32)[:, None, None]
        mag = jnp.exp(st * ar)
        return mag * jnp.cos(st * ai), mag * jnp.sin(st * ai)

    pr, pi = lam_pow(jnp.arange(t_len + 1))
    qr, qi = lam_pow((t_len - 1) - jnp.arange(t_len))
    nr, ni = pr[1] - 1.0, pi[1]
    den = lam_re * lam_re + lam_im * lam_im
    rr, ri = (nr * lam_re + ni * lam_im) / den, (ni * lam_re - nr * lam_im) / den
    bbr = rr[..., None] * b_re - ri[..., None] * b_im
    bbi = rr[..., None] * b_im + ri[..., None] * b_re

    def c_times(xr, xi):
        return (c_re[None] * xr[:, :, None, :] - c_im[None] * xi[:, :, None, :],
                c_re[None] * xi[:, :, None, :] + c_im[None] * xr[:, :, None, :])

    cqr, cqi = c_times(qr, qi)
    kc = (jnp.einsum('tghp,gpk->tgkh', cqr, bbr, precision=HIGHEST)
          - jnp.einsum('tghp,gpk->tgkh', cqi, bbi, precision=HIGHEST))
    kd = kc.reshape(t_len, nt, tg, h_cnt, h_cnt).transpose(1, 0, 3, 2, 4).reshape(nt, t_len, h_cnt, LANES)

    def per_tile(x):
        return x.reshape(-1, nt, tg, p_cnt).transpose(1, 0, 2, 3)[:, :, None]

    def b_tile(x):
        return x.reshape(nt, tg, p_cnt, h_cnt).transpose(0, 3, 1, 2)[:, None]

    q_re, q_im, bb_re, bb_im = per_tile(qr), per_tile(qi), b_tile(bbr), b_tile(bbi)
    bz = jnp.stack([q_re * bb_re - q_im * bb_im, q_re * bb_im + q_im * bb_re], axis=3)
    bz = bz.reshape(nt, t_len, h_cnt, 2 * tg * p_cnt)

    def p_tile(x):
        return x.reshape(t_len, nt, tg, p_cnt).transpose(1, 0, 3, 2)[..., None]

    def c_tile(x):
        return x.reshape(nt, tg, h_cnt, p_cnt).transpose(0, 3, 1, 2)[:, None]

    p_re, p_im, cc_re, cc_im = p_tile(pr[1:]), p_tile(pi[1:]), c_tile(c_re), c_tile(c_im)
    cm = jnp.stack([cc_re * p_re - cc_im * p_im, -(cc_re * p_im + cc_im * p_re)], axis=2)
    cm = cm.reshape(nt, t_len, 2 * p_cnt, LANES)
    a_re = pr[t_len].reshape(nt, 1, tg * p_cnt)
    a_im = pi[t_len].reshape(nt, 1, tg * p_cnt)
    return kd.astype(BF16), bz.astype(BF16), cm.astype(BF16), a_re, a_im


def _s5_kernel(u_ref, kd_ref, bz_ref, cm_ref, are_ref, aim_ref, d_ref, y_ref,
               uf_ref, ucat_ref, z_ref, hc_ref, kdf_ref, bzf_ref, cmf_ref, *, nc):
    t_len = S5_CHUNK
    ns = S5_TILE_STATE
    tg, hs, ps = S5_TILE_GROUPS, S5_GROUP, S5_STATE
    chan_grp = lax.broadcasted_iota(jnp.int32, (LANES, LANES), 0) // hs
    lane_grp = lax.broadcasted_iota(jnp.int32, (LANES, LANES), 1) // hs
    state_grp = (lax.broadcasted_iota(jnp.int32, (LANES, 2 * ns), 1) % ns) // ps
    chan_grp_w = lax.broadcasted_iota(jnp.int32, (LANES, 2 * ns), 0) // hs
    lane_grp_p = lax.broadcasted_iota(jnp.int32, (ps, LANES), 1) // hs
    zero = jnp.zeros((), BF16)
    for s in range(t_len):
        rows = slice(s * LANES, (s + 1) * LANES)
        kdf_ref[rows, :] = jnp.where(chan_grp == lane_grp, jnp.concatenate([kd_ref[0, 0, s]] * tg, axis=0), zero)
        bzf_ref[rows, :] = jnp.where(chan_grp_w == state_grp, jnp.concatenate([bz_ref[0, 0, s]] * tg, axis=0), zero)
        for x in range(2):
            piece = cm_ref[0, 0, s, x * ps:(x + 1) * ps, :]
            for g in range(tg):
                cmf_ref[s, x * ns + g * ps:x * ns + (g + 1) * ps, :] = jnp.where(lane_grp_p == g, piece, zero)
    uf_ref[...] = u_ref[...].astype(F32)
    for s in range(t_len):
        ucat_ref[:, s * LANES:(s + 1) * LANES] = uf_ref[pl.ds(s, nc, stride=t_len), :].astype(BF16)
    z_ref[...] = jnp.dot(ucat_ref[...], bzf_ref[...], preferred_element_type=F32)
    a_re = are_ref[0, 0]
    a_im = aim_ref[0, 0]

    def step(c, carry):
        h_re, h_im = carry
        hc_ref[pl.ds(c, 1), 0:ns] = h_re
        hc_ref[pl.ds(c, 1), ns:2 * ns] = h_im
        z_re = z_ref[pl.ds(c, 1), 0:ns]
        z_im = z_ref[pl.ds(c, 1), ns:2 * ns]
        return (a_re * h_re - a_im * h_im + z_re, a_re * h_im + a_im * h_re + z_im)

    zero_row = jnp.zeros((1, ns), F32)
    lax.fori_loop(0, nc, step, (zero_row, zero_row), unroll=8)
    hc = hc_ref[...].astype(BF16)
    d_skip = d_ref[0]
    for t in range(t_len):
        acc = jnp.dot(hc, cmf_ref[t], preferred_element_type=F32)
        acc += jnp.dot(ucat_ref[:, 0:(t + 1) * LANES],
                       kdf_ref[(t_len - 1 - t) * LANES:t_len * LANES, :],
                       preferred_element_type=F32)
        acc += d_skip * uf_ref[pl.ds(t, nc, stride=t_len), :]
        y_ref[pl.ds(t, nc, stride=t_len), :] = jax.nn.gelu(acc)


def _s5_glu_kernel(y_ref, yn_ref, w_ref, b_ref, o_ref, wb_ref):
    @pl.when(pl.program_id(1) == 0)
    def _():
        wb_ref[...] = w_ref[0].astype(BF16)

    z = jnp.dot(y_ref[...].astype(BF16), wb_ref[...], preferred_element_type=F32) + b_ref[0]
    o_ref[...] = (yn_ref[...] * jax.nn.sigmoid(z)).astype(o_ref.dtype)


def _s5(proj, tables, d_skip, w_glu, b_glu, layer):
    s = proj.shape[0]
    nc = s // S5_CHUNK
    nt = S5_GROUPS // S5_TILE_GROUPS
    kd, bz, cm, a_re, a_im = tables
    u_col = 3 * BRANCH_WIDTH // LANES
    ns2 = 2 * S5_TILE_STATE
    y = pl.pallas_call(
        functools.partial(_s5_kernel, nc=nc),
        grid=(nt,),
        in_specs=[pl.BlockSpec((s, LANES), lambda c: (0, u_col + c)),
                  pl.BlockSpec((1, 1, S5_CHUNK, S5_GROUP, LANES), lambda c: (layer, c, 0, 0, 0)),
                  pl.BlockSpec((1, 1, S5_CHUNK, S5_GROUP, ns2), lambda c: (layer, c, 0, 0, 0)),
                  pl.BlockSpec((1, 1, S5_CHUNK, 2 * S5_STATE, LANES), lambda c: (layer, c, 0, 0, 0)),
                  pl.BlockSpec((1, 1, 1, S5_TILE_STATE), lambda c: (layer, c, 0, 0)),
                  pl.BlockSpec((1, 1, 1, S5_TILE_STATE), lambda c: (layer, c, 0, 0)),
                  pl.BlockSpec((1, 1, LANES), lambda c: (layer, 0, c))],
        out_specs=pl.BlockSpec((s, LANES), lambda c: (0, c)),
        out_shape=jax.ShapeDtypeStruct((s, BRANCH_WIDTH), F32),
        scratch_shapes=[pltpu.VMEM((s, LANES), F32),
                        pltpu.VMEM((nc, S5_CHUNK * LANES), BF16),
                        pltpu.VMEM((nc, ns2), F32),
                        pltpu.VMEM((nc, ns2), F32),
                        pltpu.VMEM((S5_CHUNK * LANES, LANES), BF16),
                        pltpu.VMEM((S5_CHUNK * LANES, ns2), BF16),
                        pltpu.VMEM((S5_CHUNK, ns2, LANES), BF16)],
        compiler_params=_cparams(("arbitrary",)),
        name="s5_scan",
    )(proj, kd, bz, cm, a_re, a_im, d_skip.reshape(DEPTH, 1, BRANCH_WIDTH))
    tm, tn = 512, 512
    return pl.pallas_call(
        _s5_glu_kernel,
        grid=(BRANCH_WIDTH // tn, s // tm),
        in_specs=[pl.BlockSpec((tm, BRANCH_WIDTH), lambda j, i: (i, 0)),
                  pl.BlockSpec((tm, tn), lambda j, i: (i, j)),
                  pl.BlockSpec((1, BRANCH_WIDTH, tn), lambda j, i: (layer, 0, j)),
                  pl.BlockSpec((1, 1, tn), lambda j, i: (layer, 0, j))],
        out_specs=pl.BlockSpec((tm, tn), lambda j, i: (i, j)),
        out_shape=jax.ShapeDtypeStruct((s, BRANCH_WIDTH), BF16),
        scratch_shapes=[pltpu.VMEM((BRANCH_WIDTH, tn), BF16)],
        compiler_params=_cparams(("arbitrary", "arbitrary")),
        name="s5_glu",
    )(y, y, w_glu, b_glu.reshape(DEPTH, 1, BRANCH_WIDTH))


def _gla_kernel(q_ref, k_ref, v_ref, r_ref, tail_ref, wg_ref, bg_ref, ng_ref, o_ref, st_ref):
    dk = GLA_KEY // GLA_HEADS
    dv = GLA_VAL // GLA_HEADS
    cs = GLA_CHUNK
    sub = GLA_SUB
    nt = (((1,), (1,)), ((), ()))
    tn = (((0,), (0,)), ((), ()))

    @pl.when(pl.program_id(0) == 0)
    def _():
        st_ref[...] = jnp.zeros_like(st_ref)

    gate_in = jnp.dot(tail_ref[:, 0:GLA_GATE_RANK], wg_ref[0].astype(BF16),
                      preferred_element_type=F32) + bg_ref[0]
    log_a = jax.nn.log_sigmoid(gate_in) / GLA_GATE_TAU
    ri = lax.broadcasted_iota(jnp.int32, (cs, cs), 0)
    ci = lax.broadcasted_iota(jnp.int32, (cs, cs), 1)
    tril = (ri >= ci).astype(F32)
    bcum_all = jnp.dot(tril, log_a, preferred_element_type=F32, precision=HIGHEST)
    sub_row = lax.broadcasted_iota(jnp.int32, (sub, sub), 0)
    sub_col = lax.broadcasted_iota(jnp.int32, (sub, sub), 1)

    for h in range(GLA_HEADS):
        q = q_ref[:, h * dk:(h + 1) * dk].astype(F32) * dk ** -0.5
        k = k_ref[:, h * dk:(h + 1) * dk].astype(F32)
        v_bf = v_ref[:, h * dv:(h + 1) * dv]
        bc = bcum_all[:, h * dk:(h + 1) * dk]
        state = st_ref[h]
        o_inter = lax.dot_general((q * jnp.exp(bc)).astype(BF16), state.astype(BF16), nt,
                                  preferred_element_type=F32)
        parts = []
        for i in range(cs // sub):
            lo = i * sub
            b_i, q_i, k_i = bc[lo:lo + sub], q[lo:lo + sub], k[lo:lo + sub]
            o_i = o_inter[lo:lo + sub]
            if i > 0:
                ref = bc[lo - 1:lo]
                q_h = (q_i * jnp.exp(b_i - ref)).astype(BF16)
                k_h = (k[:lo] * jnp.exp(ref - bc[:lo])).astype(BF16)
                attn = lax.dot_general(q_h, k_h, nt, preferred_element_type=F32)
                o_i = o_i + jnp.dot(attn.astype(BF16), v_bf[:lo], preferred_element_type=F32)
            diag = jnp.where(sub_col == sub_row, jnp.sum(q_i * k_i, axis=-1, keepdims=True), 0.0)
            for dlt in range(1, sub):
                b_s = pltpu.roll(b_i, dlt, 0)
                k_s = pltpu.roll(k_i, dlt, 0)
                e = jnp.exp(jnp.minimum(b_i - b_s, 0.0))
                a = jnp.sum(q_i * k_s * e, axis=-1, keepdims=True)
                diag = jnp.where(sub_col == sub_row - dlt, a, diag)
            o_i = o_i + jnp.dot(diag.astype(BF16), v_bf[lo:lo + sub], preferred_element_type=F32)
            parts.append(o_i)
        o = jnp.concatenate(parts, axis=0)
        b_last = bc[cs - 1:cs]
        k_dec = (k * jnp.exp(b_last - bc)).astype(BF16)
        st_ref[h] = state * jnp.exp(b_last) + lax.dot_general(v_bf, k_dec, tn,
                                                             preferred_element_type=F32)
        o = o * lax.rsqrt(jnp.mean(o * o, axis=-1, keepdims=True) + NORM_EPS)
        o = o * ng_ref[0, :, h * dv:(h + 1) * dv]
        r = r_ref[:, h * dv:(h + 1) * dv].astype(F32)
        o_ref[:, h * dv:(h + 1) * dv] = (o * (r * jax.nn.sigmoid(r))).astype(o_ref.dtype)


def _gla(proj, w_gate, b_gate, norm_g, layer):
    s = proj.shape[0]
    cs = GLA_CHUNK
    return pl.pallas_call(
        _gla_kernel,
        grid=(s // cs,),
        in_specs=[pl.BlockSpec((cs, GLA_KEY), lambda c: (c, 4 * BRANCH_WIDTH // GLA_KEY)),
                  pl.BlockSpec((cs, GLA_KEY), lambda c: (c, 4 * BRANCH_WIDTH // GLA_KEY + 1)),
                  pl.BlockSpec((cs, GLA_VAL), lambda c: (c, 5)),
                  pl.BlockSpec((cs, GLA_VAL), lambda c: (c, 6)),
                  pl.BlockSpec((cs, TAIL_W), lambda c: (c, TAIL_COL // TAIL_W)),
                  pl.BlockSpec((1, GLA_GATE_RANK, GLA_KEY), lambda c: (layer, 0, 0)),
                  pl.BlockSpec((1, 1, GLA_KEY), lambda c: (layer, 0, 0)),
                  pl.BlockSpec((1, 1, GLA_VAL), lambda c: (layer, 0, 0))],
        out_specs=pl.BlockSpec((cs, GLA_VAL), lambda c: (c, 0)),
        out_shape=jax.ShapeDtypeStruct((s, GLA_VAL), BF16),
        scratch_shapes=[pltpu.VMEM((GLA_HEADS, GLA_VAL // GLA_HEADS, GLA_KEY // GLA_HEADS), F32)],
        compiler_params=_cparams(("arbitrary",)),
        name="gla",
    )(proj, proj, proj, proj, proj, w_gate, b_gate.reshape(DEPTH, 1, GLA_KEY),
      norm_g.reshape(DEPTH, 1, GLA_VAL))


def _merge_kernel(ya_ref, ys_ref, yg_ref, tail_ref, wb_ref, wg0_ref, wg1_ref, wg2_ref,
                  bg0_ref, bg1_ref, bg2_ref, o_ref, wbb_ref, wgb_ref):
    wg_refs = (wg0_ref, wg1_ref, wg2_ref)
    bg_refs = (bg0_ref, bg1_ref, bg2_ref)

    @pl.when(pl.program_id(1) == 0)
    def _():
        wbb_ref[...] = wb_ref[0].astype(BF16)
        for n in range(N_BRANCH):
            wgb_ref[n] = wg_refs[n][0].astype(BF16)

    mz = tail_ref[:, GLA_GATE_RANK:GLA_GATE_RANK + MERGE_RANK]
    acc = None
    for n, y_ref in enumerate((ya_ref, ys_ref, yg_ref)):
        up = jnp.dot(y_ref[...], wbb_ref[n], preferred_element_type=F32)
        gate = jax.nn.sigmoid(jnp.dot(mz, wgb_ref[n], preferred_element_type=F32) + bg_refs[n][0])
        acc = gate * up if acc is None else acc + gate * up
    o_ref[...] = acc.astype(o_ref.dtype)


def _merge(y_att, y_s5, y_gla, proj, w_branch, w_merge_gate, b_merge_gate, layer, tm=512, tn=512):
    s = proj.shape[0]
    n_col = D_MODEL // tn
    bg = b_merge_gate.reshape(DEPTH, 1, N_BRANCH * D_MODEL)
    ybs = pl.BlockSpec((tm, BRANCH_WIDTH), lambda j, i: (i, 0))

    def gate_col(n):
        return lambda j, i: (layer, 0, n * n_col + j)

    return pl.pallas_call(
        _merge_kernel,
        grid=(n_col, s // tm),
        in_specs=[ybs, ybs, ybs,
                  pl.BlockSpec((tm, TAIL_W), lambda j, i: (i, TAIL_COL // TAIL_W)),
                  pl.BlockSpec((1, N_BRANCH, BRANCH_WIDTH, tn), lambda j, i: (layer, 0, 0, j))]
                 + [pl.BlockSpec((1, MERGE_RANK, tn), gate_col(n)) for n in range(N_BRANCH)]
                 + [pl.BlockSpec((1, 1, tn), gate_col(n)) for n in range(N_BRANCH)],
        out_specs=pl.BlockSpec((tm, tn), lambda j, i: (i, j)),
        out_shape=jax.ShapeDtypeStruct((s, D_MODEL), BF16),
        scratch_shapes=[pltpu.VMEM((N_BRANCH, BRANCH_WIDTH, tn), BF16),
                        pltpu.VMEM((N_BRANCH, MERGE_RANK, tn), BF16)],
        compiler_params=_cparams(("arbitrary", "arbitrary")),
        name="merge",
    )(y_att, y_s5, y_gla, proj, w_branch, w_merge_gate, w_merge_gate, w_merge_gate, bg, bg, bg)


def _swiglu_kernel(a_ref, w1_ref, w3_ref, o_ref, w1b_ref, w3b_ref):
    @pl.when(pl.program_id(1) == 0)
    def _():
        w1b_ref[...] = w1_ref[0].astype(BF16)
        w3b_ref[...] = w3_ref[0].astype(BF16)

    a = a_ref[...]
    g = jnp.dot(a, w1b_ref[...], preferred_element_type=F32)
    u = jnp.dot(a, w3b_ref[...], preferred_element_type=F32)
    o_ref[...] = (g * jax.nn.sigmoid(g) * u).astype(o_ref.dtype)


def _swiglu_hidden(a, w1_stack, w3_stack, layer, tm=512, tn=512):
    m, k = a.shape
    n = w1_stack.shape[2]
    wspec = pl.BlockSpec((1, k, tn), lambda j, i: (layer, 0, j))
    return pl.pallas_call(
        _swiglu_kernel,
        grid=(n // tn, m // tm),
        in_specs=[pl.BlockSpec((tm, k), lambda j, i: (i, 0)), wspec, wspec],
        out_specs=pl.BlockSpec((tm, tn), lambda j, i: (i, j)),
        out_shape=jax.ShapeDtypeStruct((m, n), BF16),
        scratch_shapes=[pltpu.VMEM((k, tn), BF16), pltpu.VMEM((k, tn), BF16)],
        compiler_params=_cparams(("arbitrary", "arbitrary")),
        name="swiglu_hidden",
    )(a, w1_stack, w3_stack)


def _moe_plan(route):
    s = route.shape[0]
    tile = MOE_TILE
    n_tiles = (TOP_K * s) // tile + N_EXPERTS
    ids = route[:, ROUTE_ID:ROUTE_ID + TOP_K].astype(jnp.int32)
    onehot = jnp.sum(jax.nn.one_hot(ids, N_EXPERTS, dtype=jnp.int32), axis=1)
    before = jnp.cumsum(onehot, axis=0) - onehot
    counts = jnp.sum(onehot, axis=0)
    padded = (counts + tile - 1) // tile * tile
    ends = jnp.cumsum(padded)
    offsets = ends - padded
    pos = offsets[ids] + jnp.take_along_axis(before, ids, axis=1)
    tile_start = jnp.arange(n_tiles, dtype=jnp.int32) * tile
    tile_expert = jnp.minimum(jnp.sum(tile_start[:, None] >= ends[None, :], axis=1), N_EXPERTS - 1)
    n_active = (ends[-1] // tile).reshape(1)
    tokens = jnp.broadcast_to(jnp.arange(s, dtype=jnp.int32)[:, None], pos.shape)
    row_token = jnp.zeros((n_tiles * tile,), jnp.int32).at[pos.reshape(-1)].set(tokens.reshape(-1))
    pos1 = pos[:, 0].reshape(s // LANES, LANES)
    pos2 = pos[:, 1].reshape(s // LANES, LANES)
    return (pos1, pos2, row_token.reshape(-1, LANES), tile_expert.astype(jnp.int32),
            n_active.astype(jnp.int32), n_tiles)


def _moe_dispatch_kernel(rt_ref, na_ref, h_ref, o_ref, buf_ref, sem):
    i = pl.program_id(0)
    n_active = na_ref[0]

    def start_tile(tile):
        slot = tile % 2
        _start_rows(lambda r: (_row_copy(h_ref, buf_ref.at[slot], sem.at[slot],
                                         _smem_at(rt_ref, tile * MOE_TILE + r), r),), MOE_TILE)

    @pl.when(jnp.logical_and(i == 0, n_active > 0))
    def _():
        start_tile(i)

    @pl.when(i + 1 < n_active)
    def _():
        start_tile(i + 1)

    @pl.when(i < n_active)
    def _():
        slot = i % 2
        _wait_rows(h_ref, buf_ref.at[slot], sem.at[slot])
        o_ref[...] = buf_ref[slot].astype(o_ref.dtype)

    @pl.when(i >= n_active)
    def _():
        o_ref[...] = jnp.zeros_like(o_ref)


def _moe_dispatch(h, row_token, n_active, n_tiles):
    tile = MOE_TILE
    return pl.pallas_call(
        _moe_dispatch_kernel,
        grid_spec=pltpu.PrefetchScalarGridSpec(
            num_scalar_prefetch=2, grid=(n_tiles,),
            in_specs=[pl.BlockSpec(memory_space=pl.ANY)],
            out_specs=pl.BlockSpec((tile, D_MODEL), lambda i, rt, na: (i, 0)),
            scratch_shapes=[pltpu.VMEM((2, tile, D_MODEL), F32), pltpu.SemaphoreType.DMA((2,))]),
        out_shape=jax.ShapeDtypeStruct((n_tiles * tile, D_MODEL), BF16),
        compiler_params=_cparams(("arbitrary",)),
        name="moe_dispatch",
    )(row_token, n_active, h)


def _new_expert_panel(te_ref):
    i = pl.program_id(1)
    return jnp.logical_or(i == 0, te_ref[i] != te_ref[jnp.maximum(i - 1, 0)])


def _moe_hidden_kernel(te_ref, na_ref, a_ref, w1_ref, w3_ref, o_ref, w1b_ref, w3b_ref):
    @pl.when(_new_expert_panel(te_ref))
    def _():
        w1b_ref[...] = w1_ref[0, 0].astype(BF16)
        w3b_ref[...] = w3_ref[0, 0].astype(BF16)

    @pl.when(pl.program_id(1) < na_ref[0])
    def _():
        a = a_ref[...]
        g = jnp.dot(a, w1b_ref[...], preferred_element_type=F32)
        u = jnp.dot(a, w3b_ref[...], preferred_element_type=F32)
        o_ref[...] = (g * jax.nn.sigmoid(g) * u).astype(o_ref.dtype)

    @pl.when(pl.program_id(1) >= na_ref[0])
    def _():
        o_ref[...] = jnp.zeros_like(o_ref)


def _moe_down_kernel(te_ref, na_ref, a_ref, w_ref, o_ref, wb_ref):
    @pl.when(_new_expert_panel(te_ref))
    def _():
        wb_ref[...] = w_ref[0, 0].astype(BF16)

    @pl.when(pl.program_id(1) < na_ref[0])
    def _():
        o_ref[...] = jnp.dot(a_ref[...], wb_ref[...], preferred_element_type=F32).astype(o_ref.dtype)

    @pl.when(pl.program_id(1) >= na_ref[0])
    def _():
        o_ref[...] = jnp.zeros_like(o_ref)


def _moe_experts(xs, tile_expert, n_active, w1_stack, w3_stack, w2_stack, layer, n_tiles,
                 tn_hidden=256, tn_down=1024):
    tile = MOE_TILE
    rows = n_tiles * tile
    w_in_spec = pl.BlockSpec((1, 1, D_MODEL, tn_hidden), lambda n, i, te, na: (layer, te[i], 0, n))
    hid = pl.pallas_call(
        _moe_hidden_kernel,
        grid_spec=pltpu.PrefetchScalarGridSpec(
            num_scalar_prefetch=2, grid=(D_FF_EXPERT // tn_hidden, n_tiles),
            in_specs=[pl.BlockSpec((tile, D_MODEL), lambda n, i, te, na: (i, 0)), w_in_spec, w_in_spec],
            out_specs=pl.BlockSpec((tile, tn_hidden), lambda n, i, te, na: (i, n)),
            scratch_shapes=[pltpu.VMEM((D_MODEL, tn_hidden), BF16), pltpu.VMEM((D_MODEL, tn_hidden), BF16)]),
        out_shape=jax.ShapeDtypeStruct((rows, D_FF_EXPERT), BF16),
        compiler_params=_cparams(("arbitrary", "arbitrary")),
        name="moe_hidden",
    )(tile_expert, n_active, xs, w1_stack, w3_stack)
    return pl.pallas_call(
        _moe_down_kernel,
        grid_spec=pltpu.PrefetchScalarGridSpec(
            num_scalar_prefetch=2, grid=(D_MODEL // tn_down, n_tiles),
            in_specs=[pl.BlockSpec((tile, D_FF_EXPERT), lambda n, i, te, na: (i, 0)),
                      pl.BlockSpec((1, 1, D_FF_EXPERT, tn_down), lambda n, i, te, na: (layer, te[i], 0, n))],
            out_specs=pl.BlockSpec((tile, tn_down), lambda n, i, te, na: (i, n)),
            scratch_shapes=[pltpu.VMEM((D_FF_EXPERT, tn_down), BF16)]),
        out_shape=jax.ShapeDtypeStruct((rows, D_MODEL), F32),
        compiler_params=_cparams(("arbitrary", "arbitrary")),
        name="moe_down",
    )(tile_expert, n_active, hid, w2_stack)


def kernel(x, c, w_cond, b_cond, rel_bias, w_mod, b_mod, w_in, s5_lambda_re, s5_lambda_im, s5_log_dt, s5_b_re, s5_b_im, s5_c_re, s5_c_im, s5_d, s5_w_glu, s5_b_glu, gla_w_gate, gla_b_gate, gla_norm_g, w_branch, w_merge_gate, b_merge_gate, w_out, ln1_g, ln1_b, ffn_w1, ffn_w3, ffn_w2, router_w, router_b, exp_w1, exp_w3, exp_w2, ln2_g, ln2_b):
    bsz, seq, _ = x.shape
    assert bsz == 1
    mod = _conditioning(c, w_cond, b_cond, w_mod, b_mod)
    xs = x.reshape(seq, D_MODEL)
    w_in_t = jnp.swapaxes(w_in, 1, 2)
    hm = _modulate(xs, mod[0, 1], mod[0, 0])
    s5_tables = jax.vmap(_s5_tables)(s5_lambda_re, s5_lambda_im, s5_log_dt, s5_b_re, s5_b_im, s5_c_re, s5_c_im)
    for l in range(DEPTH):
        shift_f, scale_f, gate_m, gate_f = mod[l, 3], mod[l, 4], mod[l, 2], mod[l, 5]
        proj = _matmul(hm, w_in_t, l, tm=1024, tn=512, n_out=D_IN_PAD, w_transposed=True,
                       name="in_proj")
        y_att = _moba(proj, rel_bias)
        y_s5 = _s5(proj, s5_tables, s5_d, s5_w_glu, s5_b_glu, l)
        y_gla = _gla(proj, gla_w_gate, gla_b_gate, gla_norm_g, l)
        merged = _merge(y_att, y_s5, y_gla, proj, w_branch, w_merge_gate, b_merge_gate, l)
        y = _matmul(merged, w_out, l, tm=1024, tn=512, name="out_proj")
        dense = l % 2 == 0
        router = None if dense else (router_w[l // 2], router_b[l // 2])
        outs = _deepnorm_ln(xs, y, gate_m, ln1_g[l], ln1_b[l], nxt=(scale_f, shift_f), router=router)
        xs, hf = outs[0], outs[1]
        if dense:
            hid = _swiglu_hidden(hf, ffn_w1, ffn_w3, l // 2)
            f = _matmul(hid, ffn_w2, l // 2, tm=256, tn=512, name="ffn_down")
            moe = None
        else:
            route = outs[2]
            pos1, pos2, row_token, tile_expert, n_active, n_tiles = _moe_plan(route)
            xsorted = _moe_dispatch(hf, row_token, n_active, n_tiles)
            ys = _moe_experts(xsorted, tile_expert, n_active, exp_w1, exp_w3, exp_w2, l // 2, n_tiles)
            f = None
            moe = (ys, pos1, pos2, route)
        nxt = (mod[l + 1, 1], mod[l + 1, 0]) if l + 1 < DEPTH else None
        outs = _deepnorm_ln(xs, f, gate_f, ln2_g[l], ln2_b[l], nxt=nxt, moe=moe)
        xs = outs[0]
        if nxt is not None:
            hm = outs[1]
    return xs.reshape(bsz, seq, D_MODEL)
```

```python
import functools
import math

import jax
import jax.numpy as jnp
from jax import lax
from jax.experimental import pallas as pl
from jax.experimental.pallas import tpu as pltpu

F32 = jnp.float32
BF16 = jnp.bfloat16
HIGHEST = lax.Precision.HIGHEST

D_MODEL = 4096
DEPTH = 4
BRANCH_WIDTH = 1024
N_BRANCH = 3
ATT_HEADS = 8
ATT_HEAD_DIM = 128
MOBA_BLOCK = 256
MOBA_TOPK = 3
MOBA_HEAD_GROUP = 4
MOBA_ONES_ROWS = 16
REL_BUCKETS = 32
REL_MAX_DIST = 128
S5_GROUP = 16
S5_GROUPS = 64
S5_STATE = 64
S5_CHUNK = 16
S5_TILE_GROUPS = 8
S5_TILE_STATE = S5_TILE_GROUPS * S5_STATE
GLA_HEADS = 4
GLA_KEY = 512
GLA_VAL = 1024
GLA_GATE_RANK = 16
GLA_GATE_TAU = 16.0
GLA_CHUNK = 64
GLA_SUB = 16
MERGE_RANK = 256
COND_RANK = 512
D_FF = 8192
N_EXPERTS = 8
TOP_K = 2
ROUTE_ID = 8
ROUTE_W = 10
MOE_TILE = 512
DMA_ISSUE_UNROLL = 8
D_FF_EXPERT = 1792
DN_ALPHA = (2 * DEPTH) ** 0.25
LN_EPS = 1e-5
NORM_EPS = 1e-6

D_IN = 7440
D_IN_PAD = 7680
TAIL_COL = 7168
TAIL_W = 512
LANES = 128
VMEM_LIMIT = 56 * 1024 * 1024
NEG_INF = float("-inf")


def _cparams(sem):
    return pltpu.CompilerParams(dimension_semantics=sem, vmem_limit_bytes=VMEM_LIMIT)


def _cond_kernel(c_ref, w_ref, b_ref, o_ref):
    z = jnp.dot(c_ref[...], w_ref[...], preferred_element_type=F32, precision=HIGHEST) + b_ref[...]
    o_ref[...] = z * jax.nn.sigmoid(z)


def _mod_kernel(cond_ref, w_ref, b_ref, o_ref):
    o_ref[0] = jnp.dot(cond_ref[...], w_ref[0], preferred_element_type=F32,
                       precision=HIGHEST) + b_ref[0]


def _conditioning(c, w_cond, b_cond, w_mod, b_mod):
    c8 = jnp.broadcast_to(c, (8, D_MODEL))
    cond = pl.pallas_call(
        _cond_kernel,
        out_shape=jax.ShapeDtypeStruct((8, COND_RANK), F32),
        compiler_params=_cparams(None),
        name="cond",
    )(c8, w_cond, b_cond.reshape(1, COND_RANK))
    n_mod = 6 * D_MODEL
    tn = 3072
    mod = pl.pallas_call(
        _mod_kernel,
        grid=(DEPTH, n_mod // tn),
        in_specs=[pl.BlockSpec((8, COND_RANK), lambda l, n: (0, 0)),
                  pl.BlockSpec((1, COND_RANK, tn), lambda l, n: (l, 0, n)),
                  pl.BlockSpec((1, 1, tn), lambda l, n: (l, 0, n))],
        out_specs=pl.BlockSpec((1, 8, tn), lambda l, n: (l, 0, n)),
        out_shape=jax.ShapeDtypeStruct((DEPTH, 8, n_mod), F32),
        compiler_params=_cparams(("arbitrary", "arbitrary")),
        name="mod",
    )(cond, w_mod, b_mod.reshape(DEPTH, 1, n_mod))
    return mod[:, 0, :].reshape(DEPTH, 6, 1, D_MODEL)


def _modulate_kernel(x_ref, scale_ref, shift_ref, o_ref):
    o_ref[...] = (x_ref[...] * (1.0 + scale_ref[...]) + shift_ref[...]).astype(o_ref.dtype)


def _modulate(x, scale, shift, tm=512):
    s = x.shape[0]
    vec = pl.BlockSpec((1, D_MODEL), lambda m: (0, 0))
    return pl.pallas_call(
        _modulate_kernel,
        grid=(s // tm,),
        in_specs=[pl.BlockSpec((tm, D_MODEL), lambda m: (m, 0)), vec, vec],
        out_specs=pl.BlockSpec((tm, D_MODEL), lambda m: (m, 0)),
        out_shape=jax.ShapeDtypeStruct((s, D_MODEL), BF16),
        compiler_params=_cparams(("arbitrary",)),
        name="modulate",
    )(x, scale, shift)


def _route_top2(logits):
    lane = lax.broadcasted_iota(jnp.int32, logits.shape, 1)
    m1 = jnp.max(logits, axis=-1, keepdims=True)
    i1 = jnp.min(jnp.where(logits == m1, lane, LANES), axis=-1, keepdims=True)
    rest = jnp.where(lane == i1, NEG_INF, logits)
    m2 = jnp.max(rest, axis=-1, keepdims=True)
    i2 = jnp.min(jnp.where(rest == m2, lane, LANES), axis=-1, keepdims=True)
    e2 = jnp.exp(m2 - m1)
    denom = 1.0 + e2
    rec = jnp.where(lane == ROUTE_ID, i1.astype(F32), 0.0)
    rec = jnp.where(lane == ROUTE_ID + 1, i2.astype(F32), rec)
    rec = jnp.where(lane == ROUTE_W, 1.0 / denom, rec)
    return jnp.where(lane == ROUTE_W + 1, e2 / denom, rec)


def _row_copy(src_ref, dst_ref, sem, src_row, dst_row):
    return pltpu.make_async_copy(src_ref.at[pl.ds(src_row, 1)], dst_ref.at[pl.ds(dst_row, 1)], sem)


def _smem_at(ref, i):
    return ref[i // LANES, i % LANES]


def _start_rows(copies_of, n_rows):
    def start(r, carry):
        for cp in copies_of(r):
            cp.start()
        return carry

    lax.fori_loop(0, n_rows, start, 0, unroll=DMA_ISSUE_UNROLL)


def _wait_rows(src_ref, dst_ref, sem):
    pltpu.make_async_copy(src_ref.at[pl.ds(0, dst_ref.shape[0])], dst_ref, sem).wait()


def _ln_kernel(*refs, has_next, has_router, moe_combine, tm):
    pos = 0
    if moe_combine:
        pos1_ref, pos2_ref, x_ref, route_ref, ys_ref = refs[:5]
        pos = 5
    else:
        x_ref, y_ref = refs[:2]
        pos = 2
    gate_ref, g_ref, b_ref = refs[pos:pos + 3]
    pos += 3
    if has_next:
        scale_ref, shift_ref = refs[pos:pos + 2]
        pos += 2
    if has_router:
        rw_ref, rb_ref = refs[pos:pos + 2]
        pos += 2
    xo_ref = refs[pos]
    pos += 1
    if moe_combine:
        buf1_ref, buf2_ref, sem = refs[-3:]
        base = pl.program_id(0) * tm

        _start_rows(lambda r: (_row_copy(ys_ref, buf1_ref, sem, _smem_at(pos1_ref, base + r), r),
                               _row_copy(ys_ref, buf2_ref, sem, _smem_at(pos2_ref, base + r), r)), tm)
        _wait_rows(ys_ref, buf1_ref, sem)
        _wait_rows(ys_ref, buf2_ref, sem)
        route = route_ref[...]
        lane = lax.broadcasted_iota(jnp.int32, route.shape, 1)
        w1 = jnp.sum(jnp.where(lane == ROUTE_W, route, 0.0), axis=-1, keepdims=True)
        w2 = jnp.sum(jnp.where(lane == ROUTE_W + 1, route, 0.0), axis=-1, keepdims=True)
        y = w1 * buf1_ref[...] + w2 * buf2_ref[...]
    else:
        y = y_ref[...].astype(F32)
    z = DN_ALPHA * x_ref[...] + (1.0 + gate_ref[...]) * y
    mu = jnp.mean(z, axis=-1, keepdims=True)
    zc = z - mu
    var = jnp.mean(zc * zc, axis=-1, keepdims=True)
    xn = zc * lax.rsqrt(var + LN_EPS) * g_ref[...] + b_ref[...]
    xo_ref[...] = xn
    if has_next:
        ho_ref = refs[pos]
        pos += 1
        h = xn * (1.0 + scale_ref[...]) + shift_ref[...]
        ho_ref[...] = h.astype(ho_ref.dtype)
        if has_router:
            co_ref = refs[pos]
            logits = jnp.dot(h, rw_ref[...], preferred_element_type=F32, precision=HIGHEST)
            lane = lax.broadcasted_iota(jnp.int32, logits.shape, 1)
            logits = jnp.where(lane < N_EXPERTS, logits + rb_ref[...], NEG_INF)
            co_ref[...] = _route_top2(logits)


def _deepnorm_ln(x, y, gate, g, b, nxt=None, router=None, moe=None, tm=256):
    s = x.shape[0]
    n_pre = 0 if moe is None else 2
    imap = (lambda m: (m, 0)) if moe is None else (lambda m, p1, p2: (m, 0))
    vmap = (lambda m: (0, 0)) if moe is None else (lambda m, p1, p2: (0, 0))
    row = pl.BlockSpec((tm, D_MODEL), imap)
    vec = pl.BlockSpec((1, D_MODEL), vmap)
    scratch = []
    if moe is None:
        args = [x, y]
        in_specs = [row, row]
    else:
        ys, pos1, pos2, route = moe
        args = [pos1, pos2, x, route, ys]
        in_specs = [row, pl.BlockSpec((tm, LANES), imap), pl.BlockSpec(memory_space=pl.ANY)]
        scratch = [pltpu.VMEM((tm, D_MODEL), F32), pltpu.VMEM((tm, D_MODEL), F32),
                   pltpu.SemaphoreType.DMA(())]
    args += [gate, g.reshape(1, D_MODEL), b.reshape(1, D_MODEL)]
    in_specs += [vec, vec, vec]
    out_shape = [jax.ShapeDtypeStruct((s, D_MODEL), F32)]
    out_specs = [row]
    if nxt is not None:
        args += [nxt[0], nxt[1]]
        in_specs += [vec, vec]
        out_shape.append(jax.ShapeDtypeStruct((s, D_MODEL), BF16 if router is None else F32))
        out_specs.append(row)
    if router is not None:
        rw, rb = router
        rw_pad = jnp.pad(rw, ((0, 0), (0, LANES - N_EXPERTS)))
        rb_pad = jnp.pad(rb, (0, LANES - N_EXPERTS)).reshape(1, LANES)
        args += [rw_pad, rb_pad]
        in_specs += [pl.BlockSpec((D_MODEL, LANES), vmap), pl.BlockSpec((1, LANES), vmap)]
        out_shape.append(jax.ShapeDtypeStruct((s, LANES), F32))
        out_specs.append(pl.BlockSpec((tm, LANES), imap))
    return pl.pallas_call(
        functools.partial(_ln_kernel, has_next=nxt is not None, has_router=router is not None,
                          moe_combine=moe is not None, tm=tm),
        grid_spec=pltpu.PrefetchScalarGridSpec(
            num_scalar_prefetch=n_pre, grid=(s // tm,), in_specs=in_specs, out_specs=out_specs,
            scratch_shapes=scratch),
        out_shape=out_shape,
        compiler_params=_cparams(("arbitrary",)),
        name="deepnorm_ln",
    )(*args)


def _matmul_kernel(a_ref, w_ref, o_ref, wb_ref, *, n_valid, tn, w_transposed):
    @pl.when(pl.program_id(1) == 0)
    def _():
        w = w_ref[0]
        wb_ref[...] = (w.T if w_transposed else w).astype(BF16)

    acc = jnp.dot(a_ref[...], wb_ref[...], preferred_element_type=F32)
    if n_valid is not None:
        col = pl.program_id(0) * tn + lax.broadcasted_iota(jnp.int32, acc.shape, 1)
        acc = jnp.where(col < n_valid, acc, 0.0)
    o_ref[...] = acc.astype(o_ref.dtype)


def _matmul(a, w_stack, layer, tm, tn, out_dtype=BF16, n_out=None, w_transposed=False, name="matmul"):
    m, k = a.shape
    n = w_stack.shape[1 if w_transposed else 2]
    n_out = n if n_out is None else n_out
    if w_transposed:
        w_spec = pl.BlockSpec((1, tn, k), lambda j, i: (layer, j, 0))
    else:
        w_spec = pl.BlockSpec((1, k, tn), lambda j, i: (layer, 0, j))
    return pl.pallas_call(
        functools.partial(_matmul_kernel, n_valid=None if n_out == n else n, tn=tn,
                          w_transposed=w_transposed),
        grid=(n_out // tn, m // tm),
        in_specs=[pl.BlockSpec((tm, k), lambda j, i: (i, 0)), w_spec],
        out_specs=pl.BlockSpec((tm, tn), lambda j, i: (i, j)),
        out_shape=jax.ShapeDtypeStruct((m, n_out), out_dtype),
        scratch_shapes=[pltpu.VMEM((k, tn), BF16)],
        compiler_params=_cparams(("arbitrary", "arbitrary")),
        name=name,
    )(a, w_stack)


def _rel_bucket(dist):
    n = jnp.maximum(dist, 0)
    max_exact = REL_BUCKETS // 2
    nf = jnp.maximum(n, 1).astype(F32)
    large = max_exact + (jnp.log(nf / max_exact) / math.log(REL_MAX_DIST / max_exact)
                         * (REL_BUCKETS - max_exact)).astype(jnp.int32)
    large = jnp.minimum(large, REL_BUCKETS - 1)
    return jnp.where(n < max_exact, n, large)


def _moba_kernel(relb_ref, q_ref, k_ref, vt_ref, o_ref,
                 kmean_ref, bown_ref, bprev_ref, sel_ref, m_ref, acc_ref, *, nb):
    g = pl.program_id(0)
    j = pl.program_id(1)
    blk = MOBA_BLOCK
    dh = ATT_HEAD_DIM
    scale = dh ** -0.5
    ln2 = math.log(2.0)
    nt = (((1,), (1,)), ((), ()))
    heads = range(MOBA_HEAD_GROUP)
    key_i = lax.broadcasted_iota(jnp.int32, (blk, blk), 0)
    qry_i = lax.broadcasted_iota(jnp.int32, (blk, blk), 1)

    @pl.when(j == 0)
    def _():
        for hh in heads:
            head = g * MOBA_HEAD_GROUP + hh
            kf = k_ref[:, hh * dh:(hh + 1) * dh].astype(F32).reshape(nb, blk, dh)
            kmean_ref[hh] = jnp.mean(kf, axis=1)
            for ref, off in ((bown_ref, 0), (bprev_ref, blk)):
                bucket = _rel_bucket(qry_i - key_i + off)
                bias = jnp.zeros((blk, blk), F32)
                for b in range(REL_BUCKETS):
                    bias = jnp.where(bucket == b, relb_ref[head, b], bias)
                ref[hh] = bias

    row0 = pl.multiple_of(j * blk, blk)
    blk_i = lax.broadcasted_iota(jnp.int32, (nb, blk), 0)
    q2 = []
    for hh in heads:
        q = q_ref[:, hh * dh:(hh + 1) * dh]
        q2.append((q.astype(F32) * (scale / ln2)).astype(BF16))
        score = lax.dot_general(kmean_ref[hh], q.astype(F32), nt,
                                preferred_element_type=F32, precision=HIGHEST)
        sc = jnp.where(blk_i < j, score, NEG_INF)
        seladd = jnp.full((nb, blk), NEG_INF, F32)
        for _ in range(MOBA_TOPK):
            mx = jnp.max(sc, axis=0, keepdims=True)
            cand = jnp.where(sc == mx, blk_i, nb)
            cand = jnp.where(mx > NEG_INF, cand, nb)
            idx = jnp.min(cand, axis=0, keepdims=True)
            pick = blk_i == idx
            seladd = jnp.where(pick, 0.0, seladd)
            sc = jnp.where(pick, NEG_INF, sc)
        sel_ref[hh] = seladd

    def scores(kb_row0, hh):
        return lax.dot_general(k_ref[pl.ds(kb_row0, blk), hh * dh:(hh + 1) * dh], q2[hh], nt,
                               preferred_element_type=F32)

    def weighted_values(kb, hh, p):
        return jnp.dot(vt_ref[kb, hh], p.astype(BF16), preferred_element_type=F32)

    kb_prev = jnp.maximum(j - 1, 0)
    r_prev = pl.multiple_of(kb_prev * blk, blk)
    s_own = [jnp.where(key_i <= qry_i, scores(row0, hh) * ln2 + bown_ref[hh], NEG_INF) for hh in heads]
    s_prev = [scores(r_prev, hh) * ln2 + bprev_ref[hh] for hh in heads]
    m_own = [jnp.max(s, axis=0, keepdims=True) for s in s_own]
    m_prev = [jnp.max(s, axis=0, keepdims=True) for s in s_prev]
    acc_own = [weighted_values(j, hh, jnp.exp(s_own[hh] - m_own[hh])) for hh in heads]
    acc_prev = [weighted_values(kb_prev, hh, jnp.exp(s_prev[hh] - m_prev[hh])) for hh in heads]
    for hh in heads:
        m_blk = m_prev[hh] + sel_ref[hh, pl.ds(kb_prev, 1), :]
        m_new = jnp.maximum(m_own[hh], m_blk)
        m_ref[hh] = m_new
        acc_ref[hh] = jnp.exp(m_own[hh] - m_new) * acc_own[hh] + jnp.exp(m_blk - m_new) * acc_prev[hh]

    def merge(hh, kb, m_blk, acc_blk):
        m_blk = m_blk + sel_ref[hh, pl.ds(kb, 1), :]
        m_old = m_ref[hh]
        m_new = jnp.maximum(m_old, m_blk)
        m_ref[hh] = m_new
        acc_ref[hh] = jnp.exp(m_old - m_new) * acc_ref[hh] + jnp.exp(m_blk - m_new) * acc_blk

    def far_blocks(pair, carry):
        chains = [(2 * pair + second, hh, second) for second in (0, 1) for hh in heads]
        s2 = [scores(pl.multiple_of(kb * blk, blk), hh) for kb, hh, _ in chains]
        m2 = [jnp.max(s, axis=0, keepdims=True) for s in s2]
        acc_blk = [weighted_values(kb, hh, jnp.exp2(s - m)) for (kb, hh, _), s, m in zip(chains, s2, m2)]
        second_ok = jnp.where(2 * pair + 1 < j - 1, 0.0, NEG_INF)
        for (kb, hh, second), m, acc in zip(chains, m2, acc_blk):
            m_blk = m * ln2 + relb_ref[g * MOBA_HEAD_GROUP + hh, REL_BUCKETS - 1]
            merge(hh, kb, m_blk + second_ok if second else m_blk, acc)
        return carry

    lax.fori_loop(0, j // 2, far_blocks, 0)
    for hh in heads:
        acc = acc_ref[hh]
        o_ref[:, hh * dh:(hh + 1) * dh] = (acc[0:dh] / acc[dh:dh + 1]).T.astype(o_ref.dtype)


def _moba(proj, rel_bias):
    s = proj.shape[0]
    nb = s // MOBA_BLOCK
    hg = MOBA_HEAD_GROUP
    gw = hg * ATT_HEAD_DIM
    n_groups = ATT_HEADS // hg
    dhp = ATT_HEAD_DIM + MOBA_ONES_ROWS
    v_t = proj[:, 2 * BRANCH_WIDTH:3 * BRANCH_WIDTH].reshape(nb, MOBA_BLOCK, ATT_HEADS, ATT_HEAD_DIM)
    v_t = jnp.concatenate([v_t.transpose(0, 2, 3, 1),
                           jnp.ones((nb, ATT_HEADS, MOBA_ONES_ROWS, MOBA_BLOCK), BF16)], axis=2)
    return pl.pallas_call(
        functools.partial(_moba_kernel, nb=nb),
        grid=(n_groups, nb),
        in_specs=[pl.BlockSpec(memory_space=pltpu.SMEM),
                  pl.BlockSpec((MOBA_BLOCK, gw), lambda g, j: (j, g)),
                  pl.BlockSpec((s, gw), lambda g, j: (0, n_groups + g)),
                  pl.BlockSpec((nb, hg, dhp, MOBA_BLOCK), lambda g, j: (0, g, 0, 0))],
        out_specs=pl.BlockSpec((MOBA_BLOCK, gw), lambda g, j: (j, g)),
        out_shape=jax.ShapeDtypeStruct((s, BRANCH_WIDTH), BF16),
        scratch_shapes=[pltpu.VMEM((hg, nb, ATT_HEAD_DIM), F32),
                        pltpu.VMEM((hg, MOBA_BLOCK, MOBA_BLOCK), F32),
                        pltpu.VMEM((hg, MOBA_BLOCK, MOBA_BLOCK), F32),
                        pltpu.VMEM((hg, nb, MOBA_BLOCK), F32),
                        pltpu.VMEM((hg, 1, MOBA_BLOCK), F32),
                        pltpu.VMEM((hg, dhp, MOBA_BLOCK), F32)],
        compiler_params=_cparams(("arbitrary", "arbitrary")),
        name="moba",
    )(rel_bias.T, proj, proj, v_t)


def _s5_tables(lam_re, lam_im, log_dt, b_re, b_im, c_re, c_im):
    t_len = S5_CHUNK
    g_cnt, p_cnt, h_cnt = S5_GROUPS, S5_STATE, S5_GROUP
    tg = S5_TILE_GROUPS
    nt = g_cnt // tg
    dt = jnp.exp(log_dt)[:, None]
    ar, ai = lam_re * dt, lam_im * dt

    def lam_pow(steps):
        st = steps.astype(F32)[:, None, None]
        mag = jnp.exp(st * ar)
        return mag * jnp.cos(st * ai), mag * jnp.sin(st * ai)

    pr, pi = lam_pow(jnp.arange(t_len + 1))
    qr, qi = lam_pow((t_len - 1) - jnp.arange(t_len))
    nr, ni = pr[1] - 1.0, pi[1]
    den = lam_re * lam_re + lam_im * lam_im
    rr, ri = (nr * lam_re + ni * lam_im) / den, (ni * lam_re - nr * lam_im) / den
    bbr = rr[..., None] * b_re - ri[..., None] * b_im
    bbi = rr[..., None] * b_im + ri[..., None] * b_re

    def c_times(xr, xi):
        return (c_re[None] * xr[:, :, None, :] - c_im[None] * xi[:, :, None, :],
                c_re[None] * xi[:, :, None, :] + c_im[None] * xr[:, :, None, :])

    cqr, cqi = c_times(qr, qi)
    kc = (jnp.einsum('tghp,gpk->tgkh', cqr, bbr, precision=HIGHEST)
          - jnp.einsum('tghp,gpk->tgkh', cqi, bbi, precision=HIGHEST))
    kd = kc.reshape(t_len, nt, tg, h_cnt, h_cnt).transpose(1, 0, 3, 2, 4).reshape(nt, t_len, h_cnt, LANES)

    def state_minor(x):
        return x.reshape(-1, nt, 1, tg * p_cnt).transpose(1, 0, 2, 3)

    def b_state_minor(x):
        return x.reshape(nt, tg, p_cnt, h_cnt).transpose(0, 3, 1, 2).reshape(nt, 1, h_cnt, tg * p_cnt)

    q_re, q_im = state_minor(qr), state_minor(qi)
    bb_re, bb_im = b_state_minor(bbr), b_state_minor(bbi)
    bz = jnp.concatenate([q_re * bb_re - q_im * bb_im, q_re * bb_im + q_im * bb_re], axis=3)

    def c_state_minor(x):
        return x.reshape(nt, tg, h_cnt, p_cnt).transpose(0, 2, 1, 3).reshape(nt, 1, h_cnt, tg * p_cnt)

    p_re, p_im = state_minor(pr[1:]), state_minor(pi[1:])
    cc_re, cc_im = c_state_minor(c_re), c_state_minor(c_im)
    cm = jnp.concatenate([cc_re * p_re - cc_im * p_im, -(cc_re * p_im + cc_im * p_re)], axis=3)
    a_re = pr[t_len].reshape(nt, 1, tg * p_cnt)
    a_im = pi[t_len].reshape(nt, 1, tg * p_cnt)
    return kd.astype(BF16), bz.astype(BF16), cm.astype(BF16), a_re, a_im


def _s5_kernel(u_ref, kd_ref, bz_ref, cm_ref, are_ref, aim_ref, d_ref, y_ref,
               uf_ref, ucat_ref, z_ref, hc_ref, kdf_ref, bzf_ref, cmf_ref, *, nc):
    t_len = S5_CHUNK
    ns = S5_TILE_STATE
    tg, hs, ps = S5_TILE_GROUPS, S5_GROUP, S5_STATE
    chan_grp = lax.broadcasted_iota(jnp.int32, (LANES, LANES), 0) // hs
    lane_grp = lax.broadcasted_iota(jnp.int32, (LANES, LANES), 1) // hs
    state_grp = (lax.broadcasted_iota(jnp.int32, (LANES, 2 * ns), 1) % ns) // ps
    chan_grp_w = lax.broadcasted_iota(jnp.int32, (LANES, 2 * ns), 0) // hs
    zero = jnp.zeros((), BF16)
    for s in range(t_len):
        rows = slice(s * LANES, (s + 1) * LANES)
        kdf_ref[rows, :] = jnp.where(chan_grp == lane_grp, jnp.concatenate([kd_ref[0, 0, s]] * tg, axis=0), zero)
        bzf_ref[rows, :] = jnp.where(chan_grp_w == state_grp, jnp.concatenate([bz_ref[0, 0, s]] * tg, axis=0), zero)
        cmf_ref[s] = jnp.where(chan_grp_w == state_grp, jnp.concatenate([cm_ref[0, 0, s]] * tg, axis=0), zero)
    uf_ref[...] = u_ref[...].astype(F32)
    for s in range(t_len):
        ucat_ref[:, s * LANES:(s + 1) * LANES] = uf_ref[pl.ds(s, nc, stride=t_len), :].astype(BF16)
    z_ref[...] = jnp.dot(ucat_ref[...], bzf_ref[...], preferred_element_type=F32)
    a_re = are_ref[0, 0]
    a_im = aim_ref[0, 0]

    def step(c, carry):
        h_re, h_im = carry
        hc_ref[pl.ds(c, 1), 0:ns] = h_re
        hc_ref[pl.ds(c, 1), ns:2 * ns] = h_im
        z_re = z_ref[pl.ds(c, 1), 0:ns]
        z_im = z_ref[pl.ds(c, 1), ns:2 * ns]
        return (a_re * h_re - a_im * h_im + z_re, a_re * h_im + a_im * h_re + z_im)

    zero_row = jnp.zeros((1, ns), F32)
    lax.fori_loop(0, nc, step, (zero_row, zero_row), unroll=8)
    hc = hc_ref[...].astype(BF16)
    d_skip = d_ref[0]
    for t in range(t_len):
        acc = lax.dot_general(hc, cmf_ref[t], (((1,), (1,)), ((), ())), preferred_element_type=F32)
        acc += jnp.dot(ucat_ref[:, 0:(t + 1) * LANES],
                       kdf_ref[(t_len - 1 - t) * LANES:t_len * LANES, :],
                       preferred_element_type=F32)
        acc += d_skip * uf_ref[pl.ds(t, nc, stride=t_len), :]
        y_ref[pl.ds(t, nc, stride=t_len), :] = jax.nn.gelu(acc)


def _s5_glu_kernel(y_ref, yn_ref, w_ref, b_ref, o_ref, wb_ref):
    @pl.when(pl.program_id(1) == 0)
    def _():
        wb_ref[...] = w_ref[0].astype(BF16)

    z = jnp.dot(y_ref[...].astype(BF16), wb_ref[...], preferred_element_type=F32) + b_ref[0]
    o_ref[...] = (yn_ref[...] * jax.nn.sigmoid(z)).astype(o_ref.dtype)


def _s5(proj, tables, d_skip, w_glu, b_glu, layer):
    s = proj.shape[0]
    nc = s // S5_CHUNK
    nt = S5_GROUPS // S5_TILE_GROUPS
    kd, bz, cm, a_re, a_im = tables
    u_col = 3 * BRANCH_WIDTH // LANES
    ns2 = 2 * S5_TILE_STATE
    y = pl.pallas_call(
        functools.partial(_s5_kernel, nc=nc),
        grid=(nt,),
        in_specs=[pl.BlockSpec((s, LANES), lambda c: (0, u_col + c)),
                  pl.BlockSpec((1, 1, S5_CHUNK, S5_GROUP, LANES), lambda c: (layer, c, 0, 0, 0)),
                  pl.BlockSpec((1, 1, S5_CHUNK, S5_GROUP, ns2), lambda c: (layer, c, 0, 0, 0)),
                  pl.BlockSpec((1, 1, S5_CHUNK, S5_GROUP, ns2), lambda c: (layer, c, 0, 0, 0)),
                  pl.BlockSpec((1, 1, 1, S5_TILE_STATE), lambda c: (layer, c, 0, 0)),
                  pl.BlockSpec((1, 1, 1, S5_TILE_STATE), lambda c: (layer, c, 0, 0)),
                  pl.BlockSpec((1, 1, LANES), lambda c: (layer, 0, c))],
        out_specs=pl.BlockSpec((s, LANES), lambda c: (0, c)),
        out_shape=jax.ShapeDtypeStruct((s, BRANCH_WIDTH), F32),
        scratch_shapes=[pltpu.VMEM((s, LANES), F32),
                        pltpu.VMEM((nc, S5_CHUNK * LANES), BF16),
                        pltpu.VMEM((nc, ns2), F32),
                        pltpu.VMEM((nc, ns2), F32),
                        pltpu.VMEM((S5_CHUNK * LANES, LANES), BF16),
                        pltpu.VMEM((S5_CHUNK * LANES, ns2), BF16),
                        pltpu.VMEM((S5_CHUNK, LANES, ns2), BF16)],
        compiler_params=_cparams(("arbitrary",)),
        name="s5_scan",
    )(proj, kd, bz, cm, a_re, a_im, d_skip.reshape(DEPTH, 1, BRANCH_WIDTH))
    tm, tn = 512, 512
    return pl.pallas_call(
        _s5_glu_kernel,
        grid=(BRANCH_WIDTH // tn, s // tm),
        in_specs=[pl.BlockSpec((tm, BRANCH_WIDTH), lambda j, i: (i, 0)),
                  pl.BlockSpec((tm, tn), lambda j, i: (i, j)),
                  pl.BlockSpec((1, BRANCH_WIDTH, tn), lambda j, i: (layer, 0, j)),
                  pl.BlockSpec((1, 1, tn), lambda j, i: (layer, 0, j))],
        out_specs=pl.BlockSpec((tm, tn), lambda j, i: (i, j)),
        out_shape=jax.ShapeDtypeStruct((s, BRANCH_WIDTH), BF16),
        scratch_shapes=[pltpu.VMEM((BRANCH_WIDTH, tn), BF16)],
        compiler_params=_cparams(("arbitrary", "arbitrary")),
        name="s5_glu",
    )(y, y, w_glu, b_glu.reshape(DEPTH, 1, BRANCH_WIDTH))


def _gla_kernel(q_ref, k_ref, v_ref, r_ref, tail_ref, wg_ref, bg_ref, ng_ref, o_ref, st_ref):
    dk = GLA_KEY // GLA_HEADS
    dv = GLA_VAL // GLA_HEADS
    cs = GLA_CHUNK
    sub = GLA_SUB
    nt = (((1,), (1,)), ((), ()))
    tn = (((0,), (0,)), ((), ()))

    @pl.when(pl.program_id(0) == 0)
    def _():
        st_ref[...] = jnp.zeros_like(st_ref)

    gate_in = jnp.dot(tail_ref[:, 0:GLA_GATE_RANK], wg_ref[0].astype(BF16),
                      preferred_element_type=F32) + bg_ref[0]
    log_a = jax.nn.log_sigmoid(gate_in) / GLA_GATE_TAU
    ri = lax.broadcasted_iota(jnp.int32, (cs, cs), 0)
    ci = lax.broadcasted_iota(jnp.int32, (cs, cs), 1)
    tril = (ri >= ci).astype(F32)
    bcum_all = jnp.dot(tril, log_a, preferred_element_type=F32, precision=HIGHEST)
    sub_row = lax.broadcasted_iota(jnp.int32, (sub, sub), 0)
    sub_col = lax.broadcasted_iota(jnp.int32, (sub, sub), 1)

    for h in range(GLA_HEADS):
        q = q_ref[:, h * dk:(h + 1) * dk].astype(F32) * dk ** -0.5
        k = k_ref[:, h * dk:(h + 1) * dk].astype(F32)
        v_bf = v_ref[:, h * dv:(h + 1) * dv]
        bc = bcum_all[:, h * dk:(h + 1) * dk]
        state = st_ref[h]
        o_inter = lax.dot_general((q * jnp.exp(bc)).astype(BF16), state.astype(BF16), nt,
                                  preferred_element_type=F32)
        parts = []
        for i in range(cs // sub):
            lo = i * sub
            b_i, q_i, k_i = bc[lo:lo + sub], q[lo:lo + sub], k[lo:lo + sub]
            o_i = o_inter[lo:lo + sub]
            if i > 0:
                ref = bc[lo - 1:lo]
                q_h = (q_i * jnp.exp(b_i - ref)).astype(BF16)
                k_h = (k[:lo] * jnp.exp(ref - bc[:lo])).astype(BF16)
                attn = lax.dot_general(q_h, k_h, nt, preferred_element_type=F32)
                o_i = o_i + jnp.dot(attn.astype(BF16), v_bf[:lo], preferred_element_type=F32)
            diag = jnp.where(sub_col == sub_row, jnp.sum(q_i * k_i, axis=-1, keepdims=True), 0.0)
            for dlt in range(1, sub):
                b_s = pltpu.roll(b_i, dlt, 0)
                k_s = pltpu.roll(k_i, dlt, 0)
                e = jnp.exp(jnp.minimum(b_i - b_s, 0.0))
                a = jnp.sum(q_i * k_s * e, axis=-1, keepdims=True)
                diag = jnp.where(sub_col == sub_row - dlt, a, diag)
            o_i = o_i + jnp.dot(diag.astype(BF16), v_bf[lo:lo + sub], preferred_element_type=F32)
            parts.append(o_i)
        o = jnp.concatenate(parts, axis=0)
        b_last = bc[cs - 1:cs]
        k_dec = (k * jnp.exp(b_last - bc)).astype(BF16)
        st_ref[h] = state * jnp.exp(b_last) + lax.dot_general(v_bf, k_dec, tn,
                                                             preferred_element_type=F32)
        o = o * lax.rsqrt(jnp.mean(o * o, axis=-1, keepdims=True) + NORM_EPS)
        o = o * ng_ref[0, :, h * dv:(h + 1) * dv]
        r = r_ref[:, h * dv:(h + 1) * dv].astype(F32)
        o_ref[:, h * dv:(h + 1) * dv] = (o * (r * jax.nn.sigmoid(r))).astype(o_ref.dtype)


def _gla(proj, w_gate, b_gate, norm_g, layer):
    s = proj.shape[0]
    cs = GLA_CHUNK
    return pl.pallas_call(
        _gla_kernel,
        grid=(s // cs,),
        in_specs=[pl.BlockSpec((cs, GLA_KEY), lambda c: (c, 4 * BRANCH_WIDTH // GLA_KEY)),
                  pl.BlockSpec((cs, GLA_KEY), lambda c: (c, 4 * BRANCH_WIDTH // GLA_KEY + 1)),
                  pl.BlockSpec((cs, GLA_VAL), lambda c: (c, 5)),
                  pl.BlockSpec((cs, GLA_VAL), lambda c: (c, 6)),
                  pl.BlockSpec((cs, TAIL_W), lambda c: (c, TAIL_COL // TAIL_W)),
                  pl.BlockSpec((1, GLA_GATE_RANK, GLA_KEY), lambda c: (layer, 0, 0)),
                  pl.BlockSpec((1, 1, GLA_KEY), lambda c: (layer, 0, 0)),
                  pl.BlockSpec((1, 1, GLA_VAL), lambda c: (layer, 0, 0))],
        out_specs=pl.BlockSpec((cs, GLA_VAL), lambda c: (c, 0)),
        out_shape=jax.ShapeDtypeStruct((s, GLA_VAL), BF16),
        scratch_shapes=[pltpu.VMEM((GLA_HEADS, GLA_VAL // GLA_HEADS, GLA_KEY // GLA_HEADS), F32)],
        compiler_params=_cparams(("arbitrary",)),
        name="gla",
    )(proj, proj, proj, proj, proj, w_gate, b_gate.reshape(DEPTH, 1, GLA_KEY),
      norm_g.reshape(DEPTH, 1, GLA_VAL))


def _merge_kernel(ya_ref, ys_ref, yg_ref, tail_ref, wb_ref, wg0_ref, wg1_ref, wg2_ref,
                  bg0_ref, bg1_ref, bg2_ref, o_ref, wbb_ref, wgb_ref):
    wg_refs = (wg0_ref, wg1_ref, wg2_ref)
    bg_refs = (bg0_ref, bg1_ref, bg2_ref)

    @pl.when(pl.program_id(1) == 0)
    def _():
        wbb_ref[...] = wb_ref[0].astype(BF16)
        for n in range(N_BRANCH):
            wgb_ref[n] = wg_refs[n][0].astype(BF16)

    mz = tail_ref[:, GLA_GATE_RANK:GLA_GATE_RANK + MERGE_RANK]
    acc = None
    for n, y_ref in enumerate((ya_ref, ys_ref, yg_ref)):
        up = jnp.dot(y_ref[...], wbb_ref[n], preferred_element_type=F32)
        gate = jax.nn.sigmoid(jnp.dot(mz, wgb_ref[n], preferred_element_type=F32) + bg_refs[n][0])
        acc = gate * up if acc is None else acc + gate * up
    o_ref[...] = acc.astype(o_ref.dtype)


def _merge(y_att, y_s5, y_gla, proj, w_branch, w_merge_gate, b_merge_gate, layer, tm=512, tn=512):
    s = proj.shape[0]
    n_col = D_MODEL // tn
    bg = b_merge_gate.reshape(DEPTH, 1, N_BRANCH * D_MODEL)
    ybs = pl.BlockSpec((tm, BRANCH_WIDTH), lambda j, i: (i, 0))

    def gate_col(n):
        return lambda j, i: (layer, 0, n * n_col + j)

    return pl.pallas_call(
        _merge_kernel,
        grid=(n_col, s // tm),
        in_specs=[ybs, ybs, ybs,
                  pl.BlockSpec((tm, TAIL_W), lambda j, i: (i, TAIL_COL // TAIL_W)),
                  pl.BlockSpec((1, N_BRANCH, BRANCH_WIDTH, tn), lambda j, i: (layer, 0, 0, j))]
                 + [pl.BlockSpec((1, MERGE_RANK, tn), gate_col(n)) for n in range(N_BRANCH)]
                 + [pl.BlockSpec((1, 1, tn), gate_col(n)) for n in range(N_BRANCH)],
        out_specs=pl.BlockSpec((tm, tn), lambda j, i: (i, j)),
        out_shape=jax.ShapeDtypeStruct((s, D_MODEL), BF16),
        scratch_shapes=[pltpu.VMEM((N_BRANCH, BRANCH_WIDTH, tn), BF16),
                        pltpu.VMEM((N_BRANCH, MERGE_RANK, tn), BF16)],
        compiler_params=_cparams(("arbitrary", "arbitrary")),
        name="merge",
    )(y_att, y_s5, y_gla, proj, w_branch, w_merge_gate, w_merge_gate, w_merge_gate, bg, bg, bg)


def _swiglu_kernel(a_ref, w1_ref, w3_ref, o_ref, w1b_ref, w3b_ref):
    @pl.when(pl.program_id(1) == 0)
    def _():
        w1b_ref[...] = w1_ref[0].astype(BF16)
        w3b_ref[...] = w3_ref[0].astype(BF16)

    a = a_ref[...]
    g = jnp.dot(a, w1b_ref[...], preferred_element_type=F32)
    u = jnp.dot(a, w3b_ref[...], preferred_element_type=F32)
    o_ref[...] = (g * jax.nn.sigmoid(g) * u).astype(o_ref.dtype)


def _swiglu_hidden(a, w1_stack, w3_stack, layer, tm=512, tn=512):
    m, k = a.shape
    n = w1_stack.shape[2]
    wspec = pl.BlockSpec((1, k, tn), lambda j, i: (layer, 0, j))
    return pl.pallas_call(
        _swiglu_kernel,
        grid=(n // tn, m // tm),
        in_specs=[pl.BlockSpec((tm, k), lambda j, i: (i, 0)), wspec, wspec],
        out_specs=pl.BlockSpec((tm, tn), lambda j, i: (i, j)),
        out_shape=jax.ShapeDtypeStruct((m, n), BF16),
        scratch_shapes=[pltpu.VMEM((k, tn), BF16), pltpu.VMEM((k, tn), BF16)],
        compiler_params=_cparams(("arbitrary", "arbitrary")),
        name="swiglu_hidden",
    )(a, w1_stack, w3_stack)


def _moe_plan(route):
    s = route.shape[0]
    tile = MOE_TILE
    n_tiles = (TOP_K * s) // tile + N_EXPERTS
    ids = route[:, ROUTE_ID:ROUTE_ID + TOP_K].astype(jnp.int32)
    onehot = jnp.sum(jax.nn.one_hot(ids, N_EXPERTS, dtype=jnp.int32), axis=1)
    before = jnp.cumsum(onehot, axis=0) - onehot
    counts = jnp.sum(onehot, axis=0)
    padded = (counts + tile - 1) // tile * tile
    ends = jnp.cumsum(padded)
    offsets = ends - padded
    pos = offsets[ids] + jnp.take_along_axis(before, ids, axis=1)
    tile_start = jnp.arange(n_tiles, dtype=jnp.int32) * tile
    tile_expert = jnp.minimum(jnp.sum(tile_start[:, None] >= ends[None, :], axis=1), N_EXPERTS - 1)
    n_active = (ends[-1] // tile).reshape(1)
    tokens = jnp.broadcast_to(jnp.arange(s, dtype=jnp.int32)[:, None], pos.shape)
    row_token = jnp.zeros((n_tiles * tile,), jnp.int32).at[pos.reshape(-1)].set(tokens.reshape(-1))
    pos1 = pos[:, 0].reshape(s // LANES, LANES)
    pos2 = pos[:, 1].reshape(s // LANES, LANES)
    return (pos1, pos2, row_token.reshape(-1, LANES), tile_expert.astype(jnp.int32),
            n_active.astype(jnp.int32), n_tiles)


def _moe_dispatch_kernel(rt_ref, na_ref, h_ref, o_ref, buf_ref, sem):
    i = pl.program_id(0)
    n_active = na_ref[0]

    def start_tile(tile):
        slot = tile % 2
        _start_rows(lambda r: (_row_copy(h_ref, buf_ref.at[slot], sem.at[slot],
                                         _smem_at(rt_ref, tile * MOE_TILE + r), r),), MOE_TILE)

    @pl.when(jnp.logical_and(i == 0, n_active > 0))
    def _():
        start_tile(i)

    @pl.when(i + 1 < n_active)
    def _():
        start_tile(i + 1)

    @pl.when(i < n_active)
    def _():
        slot = i % 2
        _wait_rows(h_ref, buf_ref.at[slot], sem.at[slot])
        o_ref[...] = buf_ref[slot].astype(o_ref.dtype)

    @pl.when(i >= n_active)
    def _():
        o_ref[...] = jnp.zeros_like(o_ref)


def _moe_dispatch(h, row_token, n_active, n_tiles):
    tile = MOE_TILE
    return pl.pallas_call(
        _moe_dispatch_kernel,
        grid_spec=pltpu.PrefetchScalarGridSpec(
            num_scalar_prefetch=2, grid=(n_tiles,),
            in_specs=[pl.BlockSpec(memory_space=pl.ANY)],
            out_specs=pl.BlockSpec((tile, D_MODEL), lambda i, rt, na: (i, 0)),
            scratch_shapes=[pltpu.VMEM((2, tile, D_MODEL), F32), pltpu.SemaphoreType.DMA((2,))]),
        out_shape=jax.ShapeDtypeStruct((n_tiles * tile, D_MODEL), BF16),
        compiler_params=_cparams(("arbitrary",)),
        name="moe_dispatch",
    )(row_token, n_active, h)


def _new_expert_panel(te_ref):
    i = pl.program_id(1)
    return jnp.logical_or(i == 0, te_ref[i] != te_ref[jnp.maximum(i - 1, 0)])


def _moe_hidden_kernel(te_ref, na_ref, a_ref, w1_ref, w3_ref, o_ref, w1b_ref, w3b_ref):
    @pl.when(_new_expert_panel(te_ref))
    def _():
        w1b_ref[...] = w1_ref[0, 0].astype(BF16)
        w3b_ref[...] = w3_ref[0, 0].astype(BF16)

    @pl.when(pl.program_id(1) < na_ref[0])
    def _():
        a = a_ref[...]
        g = jnp.dot(a, w1b_ref[...], preferred_element_type=F32)
        u = jnp.dot(a, w3b_ref[...], preferred_element_type=F32)
        o_ref[...] = (g * jax.nn.sigmoid(g) * u).astype(o_ref.dtype)

    @pl.when(pl.program_id(1) >= na_ref[0])
    def _():
        o_ref[...] = jnp.zeros_like(o_ref)


def _moe_down_kernel(te_ref, na_ref, a_ref, w_ref, o_ref, wb_ref):
    @pl.when(_new_expert_panel(te_ref))
    def _():
        wb_ref[...] = w_ref[0, 0].astype(BF16)

    @pl.when(pl.program_id(1) < na_ref[0])
    def _():
        o_ref[...] = jnp.dot(a_ref[...], wb_ref[...], preferred_element_type=F32).astype(o_ref.dtype)

    @pl.when(pl.program_id(1) >= na_ref[0])
    def _():
        o_ref[...] = jnp.zeros_like(o_ref)


def _moe_experts(xs, tile_expert, n_active, w1_stack, w3_stack, w2_stack, layer, n_tiles,
                 tn_hidden=256, tn_down=1024):
    tile = MOE_TILE
    rows = n_tiles * tile
    w_in_spec = pl.BlockSpec((1, 1, D_MODEL, tn_hidden), lambda n, i, te, na: (layer, te[i], 0, n))
    hid = pl.pallas_call(
        _moe_hidden_kernel,
        grid_spec=pltpu.PrefetchScalarGridSpec(
            num_scalar_prefetch=2, grid=(D_FF_EXPERT // tn_hidden, n_tiles),
            in_specs=[pl.BlockSpec((tile, D_MODEL), lambda n, i, te, na: (i, 0)), w_in_spec, w_in_spec],
            out_specs=pl.BlockSpec((tile, tn_hidden), lambda n, i, te, na: (i, n)),
            scratch_shapes=[pltpu.VMEM((D_MODEL, tn_hidden), BF16), pltpu.VMEM((D_MODEL, tn_hidden), BF16)]),
        out_shape=jax.ShapeDtypeStruct((rows, D_FF_EXPERT), BF16),
        compiler_params=_cparams(("arbitrary", "arbitrary")),
        name="moe_hidden",
    )(tile_expert, n_active, xs, w1_stack, w3_stack)
    return pl.pallas_call(
        _moe_down_kernel,
        grid_spec=pltpu.PrefetchScalarGridSpec(
            num_scalar_prefetch=2, grid=(D_MODEL // tn_down, n_tiles),
            in_specs=[pl.BlockSpec((tile, D_FF_EXPERT), lambda n, i, te, na: (i, 0)),
                      pl.BlockSpec((1, 1, D_FF_EXPERT, tn_down), lambda n, i, te, na: (layer, te[i], 0, n))],
            out_specs=pl.BlockSpec((tile, tn_down), lambda n, i, te, na: (i, n)),
            scratch_shapes=[pltpu.VMEM((D_FF_EXPERT, tn_down), BF16)]),
        out_shape=jax.ShapeDtypeStruct((rows, D_MODEL), F32),
        compiler_params=_cparams(("arbitrary", "arbitrary")),
        name="moe_down",
    )(tile_expert, n_active, hid, w2_stack)


def kernel(x, c, w_cond, b_cond, rel_bias, w_mod, b_mod, w_in, s5_lambda_re, s5_lambda_im, s5_log_dt, s5_b_re, s5_b_im, s5_c_re, s5_c_im, s5_d, s5_w_glu, s5_b_glu, gla_w_gate, gla_b_gate, gla_norm_g, w_branch, w_merge_gate, b_merge_gate, w_out, ln1_g, ln1_b, ffn_w1, ffn_w3, ffn_w2, router_w, router_b, exp_w1, exp_w3, exp_w2, ln2_g, ln2_b):
    bsz, seq, _ = x.shape
    assert bsz == 1
    mod = _conditioning(c, w_cond, b_cond, w_mod, b_mod)
    xs = x.reshape(seq, D_MODEL)
    w_in_t = jnp.swapaxes(w_in, 1, 2)
    hm = _modulate(xs, mod[0, 1], mod[0, 0])
    s5_tables = jax.vmap(_s5_tables)(s5_lambda_re, s5_lambda_im, s5_log_dt, s5_b_re, s5_b_im, s5_c_re, s5_c_im)
    for l in range(DEPTH):
        shift_f, scale_f, gate_m, gate_f = mod[l, 3], mod[l, 4], mod[l, 2], mod[l, 5]
        proj = _matmul(hm, w_in_t, l, tm=1024, tn=512, n_out=D_IN_PAD, w_transposed=True,
                       name="in_proj")
        y_att = _moba(proj, rel_bias)
        y_s5 = _s5(proj, s5_tables, s5_d, s5_w_glu, s5_b_glu, l)
        y_gla = _gla(proj, gla_w_gate, gla_b_gate, gla_norm_g, l)
        merged = _merge(y_att, y_s5, y_gla, proj, w_branch, w_merge_gate, b_merge_gate, l)
        y = _matmul(merged, w_out, l, tm=1024, tn=512, name="out_proj")
        dense = l % 2 == 0
        router = None if dense else (router_w[l // 2], router_b[l // 2])
        outs = _deepnorm_ln(xs, y, gate_m, ln1_g[l], ln1_b[l], nxt=(scale_f, shift_f), router=router)
        xs, hf = outs[0], outs[1]
        if dense:
            hid = _swiglu_hidden(hf, ffn_w1, ffn_w3, l // 2)
            f = _matmul(hid, ffn_w2, l // 2, tm=256, tn=512, name="ffn_down")
            moe = None
        else:
            route = outs[2]
            pos1, pos2, row_token, tile_expert, n_active, n_tiles = _moe_plan(route)
            xsorted = _moe_dispatch(hf, row_token, n_active, n_tiles)
            ys = _moe_experts(xsorted, tile_expert, n_active, exp_w1, exp_w3, exp_w2, l // 2, n_tiles)
            f = None
            moe = (ys, pos1, pos2, route)
        nxt = (mod[l + 1, 1], mod[l + 1, 0]) if l + 1 < DEPTH else None
        outs = _deepnorm_ln(xs, f, gate_f, ln2_g[l], ln2_b[l], nxt=nxt, moe=moe)
        xs = outs[0]
        if nxt is not None:
            hm = outs[1]
    return xs.reshape(bsz, seq, D_MODEL)
```

```python
import functools
import math

import jax
import jax.numpy as jnp
from jax import lax
from jax.experimental import pallas as pl
from jax.experimental.pallas import tpu as pltpu

F32 = jnp.float32
BF16 = jnp.bfloat16
HIGHEST = lax.Precision.HIGHEST

D_MODEL = 4096
DEPTH = 4
BRANCH_WIDTH = 1024
N_BRANCH = 3
ATT_HEADS = 8
ATT_HEAD_DIM = 128
MOBA_BLOCK = 256
MOBA_TOPK = 3
MOBA_HEAD_GROUP = 4
MOBA_ONES_ROWS = 16
REL_BUCKETS = 32
REL_MAX_DIST = 128
S5_GROUP = 16
S5_GROUPS = 64
S5_STATE = 64
S5_CHUNK = 16
S5_TILE_GROUPS = 8
S5_TILE_STATE = S5_TILE_GROUPS * S5_STATE
GLA_HEADS = 4
GLA_KEY = 512
GLA_VAL = 1024
GLA_GATE_RANK = 16
GLA_GATE_TAU = 16.0
GLA_CHUNK = 64
GLA_SUB = 16
MERGE_RANK = 256
COND_RANK = 512
D_FF = 8192
N_EXPERTS = 8
TOP_K = 2
ROUTE_ID = 8
ROUTE_W = 10
MOE_TILE = 512
DMA_ISSUE_UNROLL = 8
D_FF_EXPERT = 1792
DN_ALPHA = (2 * DEPTH) ** 0.25
LN_EPS = 1e-5
NORM_EPS = 1e-6

D_IN = 7440
D_IN_PAD = 7680
TAIL_COL = 7168
TAIL_W = 512
LANES = 128
VMEM_LIMIT = 56 * 1024 * 1024
NEG_INF = float("-inf")


def _cparams(sem):
    return pltpu.CompilerParams(dimension_semantics=sem, vmem_limit_bytes=VMEM_LIMIT)


def _cond_kernel(c_ref, w_ref, b_ref, o_ref):
    z = jnp.dot(c_ref[...], w_ref[...], preferred_element_type=F32, precision=HIGHEST) + b_ref[...]
    o_ref[...] = z * jax.nn.sigmoid(z)


def _mod_kernel(cond_ref, w_ref, b_ref, o_ref):
    o_ref[0] = jnp.dot(cond_ref[...], w_ref[0], preferred_element_type=F32,
                       precision=HIGHEST) + b_ref[0]


def _conditioning(c, w_cond, b_cond, w_mod, b_mod):
    c8 = jnp.broadcast_to(c, (8, D_MODEL))
    cond = pl.pallas_call(
        _cond_kernel,
        out_shape=jax.ShapeDtypeStruct((8, COND_RANK), F32),
        compiler_params=_cparams(None),
        name="cond",
    )(c8, w_cond, b_cond.reshape(1, COND_RANK))
    n_mod = 6 * D_MODEL
    tn = 3072
    mod = pl.pallas_call(
        _mod_kernel,
        grid=(DEPTH, n_mod // tn),
        in_specs=[pl.BlockSpec((8, COND_RANK), lambda l, n: (0, 0)),
                  pl.BlockSpec((1, COND_RANK, tn), lambda l, n: (l, 0, n)),
                  pl.BlockSpec((1, 1, tn), lambda l, n: (l, 0, n))],
        out_specs=pl.BlockSpec((1, 8, tn), lambda l, n: (l, 0, n)),
        out_shape=jax.ShapeDtypeStruct((DEPTH, 8, n_mod), F32),
        compiler_params=_cparams(("arbitrary", "arbitrary")),
        name="mod",
    )(cond, w_mod, b_mod.reshape(DEPTH, 1, n_mod))
    return mod[:, 0, :].reshape(DEPTH, 6, 1, D_MODEL)


def _modulate_kernel(x_ref, scale_ref, shift_ref, o_ref):
    o_ref[...] = (x_ref[...] * (1.0 + scale_ref[...]) + shift_ref[...]).astype(o_ref.dtype)


def _modulate(x, scale, shift, tm=512):
    s = x.shape[0]
    vec = pl.BlockSpec((1, D_MODEL), lambda m: (0, 0))
    return pl.pallas_call(
        _modulate_kernel,
        grid=(s // tm,),
        in_specs=[pl.BlockSpec((tm, D_MODEL), lambda m: (m, 0)), vec, vec],
        out_specs=pl.BlockSpec((tm, D_MODEL), lambda m: (m, 0)),
        out_shape=jax.ShapeDtypeStruct((s, D_MODEL), BF16),
        compiler_params=_cparams(("arbitrary",)),
        name="modulate",
    )(x, scale, shift)


def _route_top2(logits):
    lane = lax.broadcasted_iota(jnp.int32, logits.shape, 1)
    m1 = jnp.max(logits, axis=-1, keepdims=True)
    i1 = jnp.min(jnp.where(logits == m1, lane, LANES), axis=-1, keepdims=True)
    rest = jnp.where(lane == i1, NEG_INF, logits)
    m2 = jnp.max(rest, axis=-1, keepdims=True)
    i2 = jnp.min(jnp.where(rest == m2, lane, LANES), axis=-1, keepdims=True)
    e2 = jnp.exp(m2 - m1)
    denom = 1.0 + e2
    rec = jnp.where(lane == ROUTE_ID, i1.astype(F32), 0.0)
    rec = jnp.where(lane == ROUTE_ID + 1, i2.astype(F32), rec)
    rec = jnp.where(lane == ROUTE_W, 1.0 / denom, rec)
    return jnp.where(lane == ROUTE_W + 1, e2 / denom, rec)


def _row_copy(src_ref, dst_ref, sem, src_row, dst_row):
    return pltpu.make_async_copy(src_ref.at[pl.ds(src_row, 1)], dst_ref.at[pl.ds(dst_row, 1)], sem)


def _smem_at(ref, i):
    return ref[i // LANES, i % LANES]


def _start_rows(copies_of, n_rows):
    def start(r, carry):
        for cp in copies_of(r):
            cp.start()
        return carry

    lax.fori_loop(0, n_rows, start, 0, unroll=DMA_ISSUE_UNROLL)


def _wait_rows(src_ref, dst_ref, sem):
    pltpu.make_async_copy(src_ref.at[pl.ds(0, dst_ref.shape[0])], dst_ref, sem).wait()


def _ln_kernel(*refs, has_next, has_router, moe_combine, tm):
    pos = 0
    if moe_combine:
        pos1_ref, pos2_ref, x_ref, route_ref, ys_ref = refs[:5]
        pos = 5
    else:
        x_ref, y_ref = refs[:2]
        pos = 2
    gate_ref, g_ref, b_ref = refs[pos:pos + 3]
    pos += 3
    if has_next:
        scale_ref, shift_ref = refs[pos:pos + 2]
        pos += 2
    if has_router:
        rw_ref, rb_ref = refs[pos:pos + 2]
        pos += 2
    xo_ref = refs[pos]
    pos += 1
    if moe_combine:
        buf1_ref, buf2_ref, sem = refs[-3:]
        base = pl.program_id(0) * tm

        _start_rows(lambda r: (_row_copy(ys_ref, buf1_ref, sem, _smem_at(pos1_ref, base + r), r),
                               _row_copy(ys_ref, buf2_ref, sem, _smem_at(pos2_ref, base + r), r)), tm)
        _wait_rows(ys_ref, buf1_ref, sem)
        _wait_rows(ys_ref, buf2_ref, sem)
        route = route_ref[...]
        lane = lax.broadcasted_iota(jnp.int32, route.shape, 1)
        w1 = jnp.sum(jnp.where(lane == ROUTE_W, route, 0.0), axis=-1, keepdims=True)
        w2 = jnp.sum(jnp.where(lane == ROUTE_W + 1, route, 0.0), axis=-1, keepdims=True)
        y = w1 * buf1_ref[...] + w2 * buf2_ref[...]
    else:
        y = y_ref[...].astype(F32)
    z = DN_ALPHA * x_ref[...] + (1.0 + gate_ref[...]) * y
    mu = jnp.mean(z, axis=-1, keepdims=True)
    zc = z - mu
    var = jnp.mean(zc * zc, axis=-1, keepdims=True)
    xn = zc * lax.rsqrt(var + LN_EPS) * g_ref[...] + b_ref[...]
    xo_ref[...] = xn
    if has_next:
        ho_ref = refs[pos]
        pos += 1
        h = xn * (1.0 + scale_ref[...]) + shift_ref[...]
        ho_ref[...] = h.astype(ho_ref.dtype)
        if has_router:
            co_ref = refs[pos]
            logits = jnp.dot(h, rw_ref[...], preferred_element_type=F32, precision=HIGHEST)
            lane = lax.broadcasted_iota(jnp.int32, logits.shape, 1)
            logits = jnp.where(lane < N_EXPERTS, logits + rb_ref[...], NEG_INF)
            co_ref[...] = _route_top2(logits)


def _deepnorm_ln(x, y, gate, g, b, nxt=None, router=None, moe=None, tm=256):
    s = x.shape[0]
    n_pre = 0 if moe is None else 2
    imap = (lambda m: (m, 0)) if moe is None else (lambda m, p1, p2: (m, 0))
    vmap = (lambda m: (0, 0)) if moe is None else (lambda m, p1, p2: (0, 0))
    row = pl.BlockSpec((tm, D_MODEL), imap)
    vec = pl.BlockSpec((1, D_MODEL), vmap)
    scratch = []
    if moe is None:
        args = [x, y]
        in_specs = [row, row]
    else:
        ys, pos1, pos2, route = moe
        args = [pos1, pos2, x, route, ys]
        in_specs = [row, pl.BlockSpec((tm, LANES), imap), pl.BlockSpec(memory_space=pl.ANY)]
        scratch = [pltpu.VMEM((tm, D_MODEL), F32), pltpu.VMEM((tm, D_MODEL), F32),
                   pltpu.SemaphoreType.DMA(())]
    args += [gate, g.reshape(1, D_MODEL), b.reshape(1, D_MODEL)]
    in_specs += [vec, vec, vec]
    out_shape = [jax.ShapeDtypeStruct((s, D_MODEL), F32)]
    out_specs = [row]
    if nxt is not None:
        args += [nxt[0], nxt[1]]
        in_specs += [vec, vec]
        out_shape.append(jax.ShapeDtypeStruct((s, D_MODEL), BF16 if router is None else F32))
        out_specs.append(row)
    if router is not None:
        rw, rb = router
        rw_pad = jnp.pad(rw, ((0, 0), (0, LANES - N_EXPERTS)))
        rb_pad = jnp.pad(rb, (0, LANES - N_EXPERTS)).reshape(1, LANES)
        args += [rw_pad, rb_pad]
        in_specs += [pl.BlockSpec((D_MODEL, LANES), vmap), pl.BlockSpec((1, LANES), vmap)]
        out_shape.append(jax.ShapeDtypeStruct((s, LANES), F32))
        out_specs.append(pl.BlockSpec((tm, LANES), imap))
    return pl.pallas_call(
        functools.partial(_ln_kernel, has_next=nxt is not None, has_router=router is not None,
                          moe_combine=moe is not None, tm=tm),
        grid_spec=pltpu.PrefetchScalarGridSpec(
            num_scalar_prefetch=n_pre, grid=(s // tm,), in_specs=in_specs, out_specs=out_specs,
            scratch_shapes=scratch),
        out_shape=out_shape,
        compiler_params=_cparams(("arbitrary",)),
        name="deepnorm_ln",
    )(*args)


def _matmul_kernel(a_ref, w_ref, o_ref, wb_ref, *, n_valid, tn, w_transposed):
    @pl.when(pl.program_id(1) == 0)
    def _():
        w = w_ref[0]
        wb_ref[...] = (w.T if w_transposed else w).astype(BF16)

    acc = jnp.dot(a_ref[...], wb_ref[...], preferred_element_type=F32)
    if n_valid is not None:
        col = pl.program_id(0) * tn + lax.broadcasted_iota(jnp.int32, acc.shape, 1)
        acc = jnp.where(col < n_valid, acc, 0.0)
    o_ref[...] = acc.astype(o_ref.dtype)


def _matmul(a, w_stack, layer, tm, tn, out_dtype=BF16, n_out=None, w_transposed=False, name="matmul"):
    m, k = a.shape
    n = w_stack.shape[1 if w_transposed else 2]
    n_out = n if n_out is None else n_out
    if w_transposed:
        w_spec = pl.BlockSpec((1, tn, k), lambda j, i: (layer, j, 0))
    else:
        w_spec = pl.BlockSpec((1, k, tn), lambda j, i: (layer, 0, j))
    return pl.pallas_call(
        functools.partial(_matmul_kernel, n_valid=None if n_out == n else n, tn=tn,
                          w_transposed=w_transposed),
        grid=(n_out // tn, m // tm),
        in_specs=[pl.BlockSpec((tm, k), lambda j, i: (i, 0)), w_spec],
        out_specs=pl.BlockSpec((tm, tn), lambda j, i: (i, j)),
        out_shape=jax.ShapeDtypeStruct((m, n_out), out_dtype),
        scratch_shapes=[pltpu.VMEM((k, tn), BF16)],
        compiler_params=_cparams(("arbitrary", "arbitrary")),
        name=name,
    )(a, w_stack)


def _rel_bucket(dist):
    n = jnp.maximum(dist, 0)
    max_exact = REL_BUCKETS // 2
    nf = jnp.maximum(n, 1).astype(F32)
    large = max_exact + (jnp.log(nf / max_exact) / math.log(REL_MAX_DIST / max_exact)
                         * (REL_BUCKETS - max_exact)).astype(jnp.int32)
    large = jnp.minimum(large, REL_BUCKETS - 1)
    return jnp.where(n < max_exact, n, large)


def _moba_kernel(relb_ref, q_ref, k_ref, vt_ref, o_ref,
                 kmean_ref, bown_ref, bprev_ref, sel_ref, m_ref, acc_ref, *, nb):
    g = pl.program_id(0)
    j = pl.program_id(1)
    blk = MOBA_BLOCK
    dh = ATT_HEAD_DIM
    scale = dh ** -0.5
    ln2 = math.log(2.0)
    nt = (((1,), (1,)), ((), ()))
    heads = range(MOBA_HEAD_GROUP)
    key_i = lax.broadcasted_iota(jnp.int32, (blk, blk), 0)
    qry_i = lax.broadcasted_iota(jnp.int32, (blk, blk), 1)

    @pl.when(j == 0)
    def _():
        for hh in heads:
            head = g * MOBA_HEAD_GROUP + hh
            kf = k_ref[:, hh * dh:(hh + 1) * dh].astype(F32).reshape(nb, blk, dh)
            kmean_ref[hh] = jnp.mean(kf, axis=1)
            for ref, off in ((bown_ref, 0), (bprev_ref, blk)):
                bucket = _rel_bucket(qry_i - key_i + off)
                bias = jnp.zeros((blk, blk), F32)
                for b in range(REL_BUCKETS):
                    bias = jnp.where(bucket == b, relb_ref[head, b], bias)
                ref[hh] = bias

    row0 = pl.multiple_of(j * blk, blk)
    blk_i = lax.broadcasted_iota(jnp.int32, (nb, blk), 0)
    q2 = []
    for hh in heads:
        q = q_ref[:, hh * dh:(hh + 1) * dh]
        q2.append((q.astype(F32) * (scale / ln2)).astype(BF16))
        score = lax.dot_general(kmean_ref[hh], q.astype(F32), nt,
                                preferred_element_type=F32, precision=HIGHEST)
        sc = jnp.where(blk_i < j, score, NEG_INF)
        seladd = jnp.full((nb, blk), NEG_INF, F32)
        for _ in range(MOBA_TOPK):
            mx = jnp.max(sc, axis=0, keepdims=True)
            cand = jnp.where(sc == mx, blk_i, nb)
            cand = jnp.where(mx > NEG_INF, cand, nb)
            idx = jnp.min(cand, axis=0, keepdims=True)
            pick = blk_i == idx
            seladd = jnp.where(pick, 0.0, seladd)
            sc = jnp.where(pick, NEG_INF, sc)
        sel_ref[hh] = seladd

    def scores(kb_row0, hh):
        return lax.dot_general(k_ref[pl.ds(kb_row0, blk), hh * dh:(hh + 1) * dh], q2[hh], nt,
                               preferred_element_type=F32)

    def weighted_values(kb, hh, p):
        return jnp.dot(vt_ref[kb, hh], p.astype(BF16), preferred_element_type=F32)

    kb_prev = jnp.maximum(j - 1, 0)
    r_prev = pl.multiple_of(kb_prev * blk, blk)
    s_own = [jnp.where(key_i <= qry_i, scores(row0, hh) * ln2 + bown_ref[hh], NEG_INF) for hh in heads]
    s_prev = [scores(r_prev, hh) * ln2 + bprev_ref[hh] for hh in heads]
    m_own = [jnp.max(s, axis=0, keepdims=True) for s in s_own]
    m_prev = [jnp.max(s, axis=0, keepdims=True) for s in s_prev]
    acc_own = [weighted_values(j, hh, jnp.exp(s_own[hh] - m_own[hh])) for hh in heads]
    acc_prev = [weighted_values(kb_prev, hh, jnp.exp(s_prev[hh] - m_prev[hh])) for hh in heads]
    for hh in heads:
        m_blk = m_prev[hh] + sel_ref[hh, pl.ds(kb_prev, 1), :]
        m_new = jnp.maximum(m_own[hh], m_blk)
        m_ref[hh] = m_new
        acc_ref[hh] = jnp.exp(m_own[hh] - m_new) * acc_own[hh] + jnp.exp(m_blk - m_new) * acc_prev[hh]

    def merge(hh, kb, m_blk, acc_blk):
        m_blk = m_blk + sel_ref[hh, pl.ds(kb, 1), :]
        m_old = m_ref[hh]
        m_new = jnp.maximum(m_old, m_blk)
        m_ref[hh] = m_new
        acc_ref[hh] = jnp.exp(m_old - m_new) * acc_ref[hh] + jnp.exp(m_blk - m_new) * acc_blk

    def far_blocks(pair, carry):
        chains = [(2 * pair + second, hh, second) for second in (0, 1) for hh in heads]
        s2 = [scores(pl.multiple_of(kb * blk, blk), hh) for kb, hh, _ in chains]
        m2 = [jnp.max(s, axis=0, keepdims=True) for s in s2]
        acc_blk = [weighted_values(kb, hh, jnp.exp2(s - m)) for (kb, hh, _), s, m in zip(chains, s2, m2)]
        second_ok = jnp.where(2 * pair + 1 < j - 1, 0.0, NEG_INF)
        for (kb, hh, second), m, acc in zip(chains, m2, acc_blk):
            m_blk = m * ln2 + relb_ref[g * MOBA_HEAD_GROUP + hh, REL_BUCKETS - 1]
            merge(hh, kb, m_blk + second_ok if second else m_blk, acc)
        return carry

    lax.fori_loop(0, j // 2, far_blocks, 0)
    for hh in heads:
        acc = acc_ref[hh]
        o_ref[:, hh * dh:(hh + 1) * dh] = (acc[0:dh] / acc[dh:dh + 1]).T.astype(o_ref.dtype)


def _moba(proj, rel_bias):
    s = proj.shape[0]
    nb = s // MOBA_BLOCK
    hg = MOBA_HEAD_GROUP
    gw = hg * ATT_HEAD_DIM
    n_groups = ATT_HEADS // hg
    dhp = ATT_HEAD_DIM + MOBA_ONES_ROWS
    v_t = proj[:, 2 * BRANCH_WIDTH:3 * BRANCH_WIDTH].reshape(nb, MOBA_BLOCK, ATT_HEADS, ATT_HEAD_DIM)
    v_t = jnp.concatenate([v_t.transpose(0, 2, 3, 1),
                           jnp.ones((nb, ATT_HEADS, MOBA_ONES_ROWS, MOBA_BLOCK), BF16)], axis=2)
    return pl.pallas_call(
        functools.partial(_moba_kernel, nb=nb),
        grid=(n_groups, nb),
        in_specs=[pl.BlockSpec(memory_space=pltpu.SMEM),
                  pl.BlockSpec((MOBA_BLOCK, gw), lambda g, j: (j, g)),
                  pl.BlockSpec((s, gw), lambda g, j: (0, n_groups + g)),
                  pl.BlockSpec((nb, hg, dhp, MOBA_BLOCK), lambda g, j: (0, g, 0, 0))],
        out_specs=pl.BlockSpec((MOBA_BLOCK, gw), lambda g, j: (j, g)),
        out_shape=jax.ShapeDtypeStruct((s, BRANCH_WIDTH), BF16),
        scratch_shapes=[pltpu.VMEM((hg, nb, ATT_HEAD_DIM), F32),
                        pltpu.VMEM((hg, MOBA_BLOCK, MOBA_BLOCK), F32),
                        pltpu.VMEM((hg, MOBA_BLOCK, MOBA_BLOCK), F32),
                        pltpu.VMEM((hg, nb, MOBA_BLOCK), F32),
                        pltpu.VMEM((hg, 1, MOBA_BLOCK), F32),
                        pltpu.VMEM((hg, dhp, MOBA_BLOCK), F32)],
        compiler_params=_cparams(("arbitrary", "arbitrary")),
        name="moba",
    )(rel_bias.T, proj, proj, v_t)


def _s5_tables(lam_re, lam_im, log_dt, b_re, b_im, c_re, c_im):
    t_len = S5_CHUNK
    g_cnt, p_cnt, h_cnt = S5_GROUPS, S5_STATE, S5_GROUP
    tg = S5_TILE_GROUPS
    nt = g_cnt // tg
    dt = jnp.exp(log_dt)[:, None]
    ar, ai = lam_re * dt, lam_im * dt

    def lam_pow(steps):
        st = steps.astype(F32)[:, None, None]
        mag = jnp.exp(st * ar)
        return mag * jnp.cos(st * ai), mag * jnp.sin(st * ai)

    pr, pi = lam_pow(jnp.arange(t_len + 1))
    qr, qi = lam_pow((t_len - 1) - jnp.arange(t_len))
    nr, ni = pr[1] - 1.0, pi[1]
    den = lam_re * lam_re + lam_im * lam_im
    rr, ri = (nr * lam_re + ni * lam_im) / den, (ni * lam_re - nr * lam_im) / den
    bbr = rr[..., None] * b_re - ri[..., None] * b_im
    bbi = rr[..., None] * b_im + ri[..., None] * b_re

    c_re_t, c_im_t = c_re.transpose(0, 2, 1), c_im.transpose(0, 2, 1)
    cq_re = c_re_t[None] * qr[..., None] - c_im_t[None] * qi[..., None]
    cq_im = c_re_t[None] * qi[..., None] + c_im_t[None] * qr[..., None]
    kc = jnp.sum(cq_re[:, :, :, None, :] * bbr[None, :, :, :, None]
                 - cq_im[:, :, :, None, :] * bbi[None, :, :, :, None], axis=2)
    kd = kc.reshape(t_len, nt, tg, h_cnt, h_cnt).transpose(1, 0, 3, 2, 4).reshape(nt, t_len, h_cnt, LANES)

    def state_minor(x):
        return x.reshape(-1, nt, 1, tg * p_cnt).transpose(1, 0, 2, 3)

    def b_state_minor(x):
        return x.reshape(nt, tg, p_cnt, h_cnt).transpose(0, 3, 1, 2).reshape(nt, 1, h_cnt, tg * p_cnt)

    q_re, q_im = state_minor(qr), state_minor(qi)
    bb_re, bb_im = b_state_minor(bbr), b_state_minor(bbi)
    bz = jnp.concatenate([q_re * bb_re - q_im * bb_im, q_re * bb_im + q_im * bb_re], axis=3)

    def c_state_minor(x):
        return x.reshape(nt, tg, h_cnt, p_cnt).transpose(0, 2, 1, 3).reshape(nt, 1, h_cnt, tg * p_cnt)

    p_re, p_im = state_minor(pr[1:]), state_minor(pi[1:])
    cc_re, cc_im = c_state_minor(c_re), c_state_minor(c_im)
    cm = jnp.concatenate([cc_re * p_re - cc_im * p_im, -(cc_re * p_im + cc_im * p_re)], axis=3)
    a_re = pr[t_len].reshape(nt, 1, tg * p_cnt)
    a_im = pi[t_len].reshape(nt, 1, tg * p_cnt)
    return kd.astype(BF16), bz.astype(BF16), cm.astype(BF16), a_re, a_im


def _s5_kernel(u_ref, kd_ref, bz_ref, cm_ref, are_ref, aim_ref, d_ref, y_ref,
               uf_ref, ucat_ref, z_ref, hc_ref, kdf_ref, bzf_ref, cmf_ref, *, nc):
    t_len = S5_CHUNK
    ns = S5_TILE_STATE
    tg, hs, ps = S5_TILE_GROUPS, S5_GROUP, S5_STATE
    chan_grp = lax.broadcasted_iota(jnp.int32, (LANES, LANES), 0) // hs
    lane_grp = lax.broadcasted_iota(jnp.int32, (LANES, LANES), 1) // hs
    state_grp = (lax.broadcasted_iota(jnp.int32, (LANES, 2 * ns), 1) % ns) // ps
    chan_grp_w = lax.broadcasted_iota(jnp.int32, (LANES, 2 * ns), 0) // hs
    zero = jnp.zeros((), BF16)
    for s in range(t_len):
        rows = slice(s * LANES, (s + 1) * LANES)
        kdf_ref[rows, :] = jnp.where(chan_grp == lane_grp, jnp.concatenate([kd_ref[0, 0, s]] * tg, axis=0), zero)
        bzf_ref[rows, :] = jnp.where(chan_grp_w == state_grp, jnp.concatenate([bz_ref[0, 0, s]] * tg, axis=0), zero)
        cmf_ref[s] = jnp.where(chan_grp_w == state_grp, jnp.concatenate([cm_ref[0, 0, s]] * tg, axis=0), zero)
    uf_ref[...] = u_ref[...].astype(F32)
    for s in range(t_len):
        ucat_ref[:, s * LANES:(s + 1) * LANES] = uf_ref[pl.ds(s, nc, stride=t_len), :].astype(BF16)
    z_ref[...] = jnp.dot(ucat_ref[...], bzf_ref[...], preferred_element_type=F32)
    a_re = are_ref[0, 0]
    a_im = aim_ref[0, 0]

    def step(c, carry):
        h_re, h_im = carry
        hc_ref[pl.ds(c, 1), 0:ns] = h_re
        hc_ref[pl.ds(c, 1), ns:2 * ns] = h_im
        z_re = z_ref[pl.ds(c, 1), 0:ns]
        z_im = z_ref[pl.ds(c, 1), ns:2 * ns]
        return (a_re * h_re - a_im * h_im + z_re, a_re * h_im + a_im * h_re + z_im)

    zero_row = jnp.zeros((1, ns), F32)
    lax.fori_loop(0, nc, step, (zero_row, zero_row), unroll=8)
    hc = hc_ref[...].astype(BF16)
    d_skip = d_ref[0]
    for t in range(t_len):
        acc = lax.dot_general(hc, cmf_ref[t], (((1,), (1,)), ((), ())), preferred_element_type=F32)
        acc += jnp.dot(ucat_ref[:, 0:(t + 1) * LANES],
                       kdf_ref[(t_len - 1 - t) * LANES:t_len * LANES, :],
                       preferred_element_type=F32)
        acc += d_skip * uf_ref[pl.ds(t, nc, stride=t_len), :]
        y_ref[pl.ds(t, nc, stride=t_len), :] = jax.nn.gelu(acc)


def _s5_glu_kernel(y_ref, yn_ref, w_ref, b_ref, o_ref, wb_ref):
    @pl.when(pl.program_id(1) == 0)
    def _():
        wb_ref[...] = w_ref[0].astype(BF16)

    z = jnp.dot(y_ref[...].astype(BF16), wb_ref[...], preferred_element_type=F32) + b_ref[0]
    o_ref[...] = (yn_ref[...] * jax.nn.sigmoid(z)).astype(o_ref.dtype)


def _s5(proj, tables, d_skip, w_glu, b_glu, layer):
    s = proj.shape[0]
    nc = s // S5_CHUNK
    nt = S5_GROUPS // S5_TILE_GROUPS
    kd, bz, cm, a_re, a_im = tables
    u_col = 3 * BRANCH_WIDTH // LANES
    ns2 = 2 * S5_TILE_STATE
    y = pl.pallas_call(
        functools.partial(_s5_kernel, nc=nc),
        grid=(nt,),
        in_specs=[pl.BlockSpec((s, LANES), lambda c: (0, u_col + c)),
                  pl.BlockSpec((1, 1, S5_CHUNK, S5_GROUP, LANES), lambda c: (layer, c, 0, 0, 0)),
                  pl.BlockSpec((1, 1, S5_CHUNK, S5_GROUP, ns2), lambda c: (layer, c, 0, 0, 0)),
                  pl.BlockSpec((1, 1, S5_CHUNK, S5_GROUP, ns2), lambda c: (layer, c, 0, 0, 0)),
                  pl.BlockSpec((1, 1, 1, S5_TILE_STATE), lambda c: (layer, c, 0, 0)),
                  pl.BlockSpec((1, 1, 1, S5_TILE_STATE), lambda c: (layer, c, 0, 0)),
                  pl.BlockSpec((1, 1, LANES), lambda c: (layer, 0, c))],
        out_specs=pl.BlockSpec((s, LANES), lambda c: (0, c)),
        out_shape=jax.ShapeDtypeStruct((s, BRANCH_WIDTH), F32),
        scratch_shapes=[pltpu.VMEM((s, LANES), F32),
                        pltpu.VMEM((nc, S5_CHUNK * LANES), BF16),
                        pltpu.VMEM((nc, ns2), F32),
                        pltpu.VMEM((nc, ns2), F32),
                        pltpu.VMEM((S5_CHUNK * LANES, LANES), BF16),
                        pltpu.VMEM((S5_CHUNK * LANES, ns2), BF16),
                        pltpu.VMEM((S5_CHUNK, LANES, ns2), BF16)],
        compiler_params=_cparams(("arbitrary",)),
        name="s5_scan",
    )(proj, kd, bz, cm, a_re, a_im, d_skip.reshape(DEPTH, 1, BRANCH_WIDTH))
    tm, tn = 512, 512
    return pl.pallas_call(
        _s5_glu_kernel,
        grid=(BRANCH_WIDTH // tn, s // tm),
        in_specs=[pl.BlockSpec((tm, BRANCH_WIDTH), lambda j, i: (i, 0)),
                  pl.BlockSpec((tm, tn), lambda j, i: (i, j)),
                  pl.BlockSpec((1, BRANCH_WIDTH, tn), lambda j, i: (layer, 0, j)),
                  pl.BlockSpec((1, 1, tn), lambda j, i: (layer, 0, j))],
        out_specs=pl.BlockSpec((tm, tn), lambda j, i: (i, j)),
        out_shape=jax.ShapeDtypeStruct((s, BRANCH_WIDTH), BF16),
        scratch_shapes=[pltpu.VMEM((BRANCH_WIDTH, tn), BF16)],
        compiler_params=_cparams(("arbitrary", "arbitrary")),
        name="s5_glu",
    )(y, y, w_glu, b_glu.reshape(DEPTH, 1, BRANCH_WIDTH))


def _gla_kernel(q_ref, k_ref, v_ref, r_ref, tail_ref, wg_ref, bg_ref, ng_ref, o_ref, st_ref):
    dk = GLA_KEY // GLA_HEADS
    dv = GLA_VAL // GLA_HEADS
    cs = GLA_CHUNK
    sub = GLA_SUB
    nt = (((1,), (1,)), ((), ()))
    tn = (((0,), (0,)), ((), ()))

    @pl.when(pl.program_id(0) == 0)
    def _():
        st_ref[...] = jnp.zeros_like(st_ref)

    gate_in = jnp.dot(tail_ref[:, 0:GLA_GATE_RANK], wg_ref[0].astype(BF16),
                      preferred_element_type=F32) + bg_ref[0]
    log_a = jax.nn.log_sigmoid(gate_in) / GLA_GATE_TAU
    ri = lax.broadcasted_iota(jnp.int32, (cs, cs), 0)
    ci = lax.broadcasted_iota(jnp.int32, (cs, cs), 1)
    tril = (ri >= ci).astype(F32)
    bcum_all = jnp.dot(tril, log_a, preferred_element_type=F32, precision=HIGHEST)
    sub_row = lax.broadcasted_iota(jnp.int32, (sub, sub), 0)
    sub_col = lax.broadcasted_iota(jnp.int32, (sub, sub), 1)

    for h in range(GLA_HEADS):
        q = q_ref[:, h * dk:(h + 1) * dk].astype(F32) * dk ** -0.5
        k = k_ref[:, h * dk:(h + 1) * dk].astype(F32)
        v_bf = v_ref[:, h * dv:(h + 1) * dv]
        bc = bcum_all[:, h * dk:(h + 1) * dk]
        state = st_ref[h]
        o_inter = lax.dot_general((q * jnp.exp(bc)).astype(BF16), state.astype(BF16), nt,
                                  preferred_element_type=F32)
        parts = []
        for i in range(cs // sub):
            lo = i * sub
            b_i, q_i, k_i = bc[lo:lo + sub], q[lo:lo + sub], k[lo:lo + sub]
            o_i = o_inter[lo:lo + sub]
            if i > 0:
                ref = bc[lo - 1:lo]
                q_h = (q_i * jnp.exp(b_i - ref)).astype(BF16)
                k_h = (k[:lo] * jnp.exp(ref - bc[:lo])).astype(BF16)
                attn = lax.dot_general(q_h, k_h, nt, preferred_element_type=F32)
                o_i = o_i + jnp.dot(attn.astype(BF16), v_bf[:lo], preferred_element_type=F32)
            diag = jnp.where(sub_col == sub_row, jnp.sum(q_i * k_i, axis=-1, keepdims=True), 0.0)
            for dlt in range(1, sub):
                b_s = pltpu.roll(b_i, dlt, 0)
                k_s = pltpu.roll(k_i, dlt, 0)
                e = jnp.exp(jnp.minimum(b_i - b_s, 0.0))
                a = jnp.sum(q_i * k_s * e, axis=-1, keepdims=True)
                diag = jnp.where(sub_col == sub_row - dlt, a, diag)
            o_i = o_i + jnp.dot(diag.astype(BF16), v_bf[lo:lo + sub], preferred_element_type=F32)
            parts.append(o_i)
        o = jnp.concatenate(parts, axis=0)
        b_last = bc[cs - 1:cs]
        k_dec = (k * jnp.exp(b_last - bc)).astype(BF16)
        st_ref[h] = state * jnp.exp(b_last) + lax.dot_general(v_bf, k_dec, tn,
                                                             preferred_element_type=F32)
        o = o * lax.rsqrt(jnp.mean(o * o, axis=-1, keepdims=True) + NORM_EPS)
        o = o * ng_ref[0, :, h * dv:(h + 1) * dv]
        r = r_ref[:, h * dv:(h + 1) * dv].astype(F32)
        o_ref[:, h * dv:(h + 1) * dv] = (o * (r * jax.nn.sigmoid(r))).astype(o_ref.dtype)


def _gla(proj, w_gate, b_gate, norm_g, layer):
    s = proj.shape[0]
    cs = GLA_CHUNK
    return pl.pallas_call(
        _gla_kernel,
        grid=(s // cs,),
        in_specs=[pl.BlockSpec((cs, GLA_KEY), lambda c: (c, 4 * BRANCH_WIDTH // GLA_KEY)),
                  pl.BlockSpec((cs, GLA_KEY), lambda c: (c, 4 * BRANCH_WIDTH // GLA_KEY + 1)),
                  pl.BlockSpec((cs, GLA_VAL), lambda c: (c, 5)),
                  pl.BlockSpec((cs, GLA_VAL), lambda c: (c, 6)),
                  pl.BlockSpec((cs, TAIL_W), lambda c: (c, TAIL_COL // TAIL_W)),
                  pl.BlockSpec((1, GLA_GATE_RANK, GLA_KEY), lambda c: (layer, 0, 0)),
                  pl.BlockSpec((1, 1, GLA_KEY), lambda c: (layer, 0, 0)),
                  pl.BlockSpec((1, 1, GLA_VAL), lambda c: (layer, 0, 0))],
        out_specs=pl.BlockSpec((cs, GLA_VAL), lambda c: (c, 0)),
        out_shape=jax.ShapeDtypeStruct((s, GLA_VAL), BF16),
        scratch_shapes=[pltpu.VMEM((GLA_HEADS, GLA_VAL // GLA_HEADS, GLA_KEY // GLA_HEADS), F32)],
        compiler_params=_cparams(("arbitrary",)),
        name="gla",
    )(proj, proj, proj, proj, proj, w_gate, b_gate.reshape(DEPTH, 1, GLA_KEY),
      norm_g.reshape(DEPTH, 1, GLA_VAL))


def _merge_kernel(ya_ref, ys_ref, yg_ref, tail_ref, wb_ref, wg0_ref, wg1_ref, wg2_ref,
                  bg0_ref, bg1_ref, bg2_ref, o_ref, wbb_ref, wgb_ref):
    wg_refs = (wg0_ref, wg1_ref, wg2_ref)
    bg_refs = (bg0_ref, bg1_ref, bg2_ref)

    @pl.when(pl.program_id(1) == 0)
    def _():
        wbb_ref[...] = wb_ref[0].astype(BF16)
        for n in range(N_BRANCH):
            wgb_ref[n] = wg_refs[n][0].astype(BF16)

    mz = tail_ref[:, GLA_GATE_RANK:GLA_GATE_RANK + MERGE_RANK]
    acc = None
    for n, y_ref in enumerate((ya_ref, ys_ref, yg_ref)):
        up = jnp.dot(y_ref[...], wbb_ref[n], preferred_element_type=F32)
        gate = jax.nn.sigmoid(jnp.dot(mz, wgb_ref[n], preferred_element_type=F32) + bg_refs[n][0])
        acc = gate * up if acc is None else acc + gate * up
    o_ref[...] = acc.astype(o_ref.dtype)


def _merge(y_att, y_s5, y_gla, proj, w_branch, w_merge_gate, b_merge_gate, layer, tm=512, tn=512):
    s = proj.shape[0]
    n_col = D_MODEL // tn
    bg = b_merge_gate.reshape(DEPTH, 1, N_BRANCH * D_MODEL)
    ybs = pl.BlockSpec((tm, BRANCH_WIDTH), lambda j, i: (i, 0))

    def gate_col(n):
        return lambda j, i: (layer, 0, n * n_col + j)

    return pl.pallas_call(
        _merge_kernel,
        grid=(n_col, s // tm),
        in_specs=[ybs, ybs, ybs,
                  pl.BlockSpec((tm, TAIL_W), lambda j, i: (i, TAIL_COL // TAIL_W)),
                  pl.BlockSpec((1, N_BRANCH, BRANCH_WIDTH, tn), lambda j, i: (layer, 0, 0, j))]
                 + [pl.BlockSpec((1, MERGE_RANK, tn), gate_col(n)) for n in range(N_BRANCH)]
                 + [pl.BlockSpec((1, 1, tn), gate_col(n)) for n in range(N_BRANCH)],
        out_specs=pl.BlockSpec((tm, tn), lambda j, i: (i, j)),
        out_shape=jax.ShapeDtypeStruct((s, D_MODEL), BF16),
        scratch_shapes=[pltpu.VMEM((N_BRANCH, BRANCH_WIDTH, tn), BF16),
                        pltpu.VMEM((N_BRANCH, MERGE_RANK, tn), BF16)],
        compiler_params=_cparams(("arbitrary", "arbitrary")),
        name="merge",
    )(y_att, y_s5, y_gla, proj, w_branch, w_merge_gate, w_merge_gate, w_merge_gate, bg, bg, bg)


def _swiglu_kernel(a_ref, w1_ref, w3_ref, o_ref, w1b_ref, w3b_ref):
    @pl.when(pl.program_id(1) == 0)
    def _():
        w1b_ref[...] = w1_ref[0].astype(BF16)
        w3b_ref[...] = w3_ref[0].astype(BF16)

    a = a_ref[...]
    g = jnp.dot(a, w1b_ref[...], preferred_element_type=F32)
    u = jnp.dot(a, w3b_ref[...], preferred_element_type=F32)
    o_ref[...] = (g * jax.nn.sigmoid(g) * u).astype(o_ref.dtype)


def _swiglu_hidden(a, w1_stack, w3_stack, layer, tm=512, tn=512):
    m, k = a.shape
    n = w1_stack.shape[2]
    wspec = pl.BlockSpec((1, k, tn), lambda j, i: (layer, 0, j))
    return pl.pallas_call(
        _swiglu_kernel,
        grid=(n // tn, m // tm),
        in_specs=[pl.BlockSpec((tm, k), lambda j, i: (i, 0)), wspec, wspec],
        out_specs=pl.BlockSpec((tm, tn), lambda j, i: (i, j)),
        out_shape=jax.ShapeDtypeStruct((m, n), BF16),
        scratch_shapes=[pltpu.VMEM((k, tn), BF16), pltpu.VMEM((k, tn), BF16)],
        compiler_params=_cparams(("arbitrary", "arbitrary")),
        name="swiglu_hidden",
    )(a, w1_stack, w3_stack)


def _moe_plan(route):
    s = route.shape[0]
    tile = MOE_TILE
    n_tiles = (TOP_K * s) // tile + N_EXPERTS
    ids = route[:, ROUTE_ID:ROUTE_ID + TOP_K].astype(jnp.int32)
    onehot = jnp.sum(jax.nn.one_hot(ids, N_EXPERTS, dtype=jnp.int32), axis=1)
    before = jnp.cumsum(onehot, axis=0) - onehot
    counts = jnp.sum(onehot, axis=0)
    padded = (counts + tile - 1) // tile * tile
    ends = jnp.cumsum(padded)
    offsets = ends - padded
    pos = offsets[ids] + jnp.take_along_axis(before, ids, axis=1)
    tile_start = jnp.arange(n_tiles, dtype=jnp.int32) * tile
    tile_expert = jnp.minimum(jnp.sum(tile_start[:, None] >= ends[None, :], axis=1), N_EXPERTS - 1)
    n_active = (ends[-1] // tile).reshape(1)
    tokens = jnp.broadcast_to(jnp.arange(s, dtype=jnp.int32)[:, None], pos.shape)
    row_token = jnp.zeros((n_tiles * tile,), jnp.int32).at[pos.reshape(-1)].set(tokens.reshape(-1))
    pos1 = pos[:, 0].reshape(s // LANES, LANES)
    pos2 = pos[:, 1].reshape(s // LANES, LANES)
    return (pos1, pos2, row_token.reshape(-1, LANES), tile_expert.astype(jnp.int32),
            n_active.astype(jnp.int32), n_tiles)


def _moe_dispatch_kernel(rt_ref, na_ref, h_ref, o_ref, buf_ref, sem):
    i = pl.program_id(0)
    n_active = na_ref[0]

    def start_tile(tile):
        slot = tile % 2
        _start_rows(lambda r: (_row_copy(h_ref, buf_ref.at[slot], sem.at[slot],
                                         _smem_at(rt_ref, tile * MOE_TILE + r), r),), MOE_TILE)

    @pl.when(jnp.logical_and(i == 0, n_active > 0))
    def _():
        start_tile(i)

    @pl.when(i + 1 < n_active)
    def _():
        start_tile(i + 1)

    @pl.when(i < n_active)
    def _():
        slot = i % 2
        _wait_rows(h_ref, buf_ref.at[slot], sem.at[slot])
        o_ref[...] = buf_ref[slot].astype(o_ref.dtype)

    @pl.when(i >= n_active)
    def _():
        o_ref[...] = jnp.zeros_like(o_ref)


def _moe_dispatch(h, row_token, n_active, n_tiles):
    tile = MOE_TILE
    return pl.pallas_call(
        _moe_dispatch_kernel,
        grid_spec=pltpu.PrefetchScalarGridSpec(
            num_scalar_prefetch=2, grid=(n_tiles,),
            in_specs=[pl.BlockSpec(memory_space=pl.ANY)],
            out_specs=pl.BlockSpec((tile, D_MODEL), lambda i, rt, na: (i, 0)),
            scratch_shapes=[pltpu.VMEM((2, tile, D_MODEL), F32), pltpu.SemaphoreType.DMA((2,))]),
        out_shape=jax.ShapeDtypeStruct((n_tiles * tile, D_MODEL), BF16),
        compiler_params=_cparams(("arbitrary",)),
        name="moe_dispatch",
    )(row_token, n_active, h)


def _new_expert_panel(te_ref):
    i = pl.program_id(1)
    return jnp.logical_or(i == 0, te_ref[i] != te_ref[jnp.maximum(i - 1, 0)])


def _moe_hidden_kernel(te_ref, na_ref, a_ref, w1_ref, w3_ref, o_ref, w1b_ref, w3b_ref):
    @pl.when(_new_expert_panel(te_ref))
    def _():
        w1b_ref[...] = w1_ref[0, 0].astype(BF16)
        w3b_ref[...] = w3_ref[0, 0].astype(BF16)

    @pl.when(pl.program_id(1) < na_ref[0])
    def _():
        a = a_ref[...]
        g = jnp.dot(a, w1b_ref[...], preferred_element_type=F32)
        u = jnp.dot(a, w3b_ref[...], preferred_element_type=F32)
        o_ref[...] = (g * jax.nn.sigmoid(g) * u).astype(o_ref.dtype)

    @pl.when(pl.program_id(1) >= na_ref[0])
    def _():
        o_ref[...] = jnp.zeros_like(o_ref)


def _moe_down_kernel(te_ref, na_ref, a_ref, w_ref, o_ref, wb_ref):
    @pl.when(_new_expert_panel(te_ref))
    def _():
        wb_ref[...] = w_ref[0, 0].astype(BF16)

    @pl.when(pl.program_id(1) < na_ref[0])
    def _():
        o_ref[...] = jnp.dot(a_ref[...], wb_ref[...], preferred_element_type=F32).astype(o_ref.dtype)

    @pl.when(pl.program_id(1) >= na_ref[0])
    def _():
        o_ref[...] = jnp.zeros_like(o_ref)


def _moe_experts(xs, tile_expert, n_active, w1_stack, w3_stack, w2_stack, layer, n_tiles,
                 tn_hidden=256, tn_down=1024):
    tile = MOE_TILE
    rows = n_tiles * tile
    w_in_spec = pl.BlockSpec((1, 1, D_MODEL, tn_hidden), lambda n, i, te, na: (layer, te[i], 0, n))
    hid = pl.pallas_call(
        _moe_hidden_kernel,
        grid_spec=pltpu.PrefetchScalarGridSpec(
            num_scalar_prefetch=2, grid=(D_FF_EXPERT // tn_hidden, n_tiles),
            in_specs=[pl.BlockSpec((tile, D_MODEL), lambda n, i, te, na: (i, 0)), w_in_spec, w_in_spec],
            out_specs=pl.BlockSpec((tile, tn_hidden), lambda n, i, te, na: (i, n)),
            scratch_shapes=[pltpu.VMEM((D_MODEL, tn_hidden), BF16), pltpu.VMEM((D_MODEL, tn_hidden), BF16)]),
        out_shape=jax.ShapeDtypeStruct((rows, D_FF_EXPERT), BF16),
        compiler_params=_cparams(("arbitrary", "arbitrary")),
        name="moe_hidden",
    )(tile_expert, n_active, xs, w1_stack, w3_stack)
    return pl.pallas_call(
        _moe_down_kernel,
        grid_spec=pltpu.PrefetchScalarGridSpec(
            num_scalar_prefetch=2, grid=(D_MODEL // tn_down, n_tiles),
            in_specs=[pl.BlockSpec((tile, D_FF_EXPERT), lambda n, i, te, na: (i, 0)),
                      pl.BlockSpec((1, 1, D_FF_EXPERT, tn_down), lambda n, i, te, na: (layer, te[i], 0, n))],
            out_specs=pl.BlockSpec((tile, tn_down), lambda n, i, te, na: (i, n)),
            scratch_shapes=[pltpu.VMEM((D_FF_EXPERT, tn_down), BF16)]),
        out_shape=jax.ShapeDtypeStruct((rows, D_MODEL), F32),
        compiler_params=_cparams(("arbitrary", "arbitrary")),
        name="moe_down",
    )(tile_expert, n_active, hid, w2_stack)


def kernel(x, c, w_cond, b_cond, rel_bias, w_mod, b_mod, w_in, s5_lambda_re, s5_lambda_im, s5_log_dt, s5_b_re, s5_b_im, s5_c_re, s5_c_im, s5_d, s5_w_glu, s5_b_glu, gla_w_gate, gla_b_gate, gla_norm_g, w_branch, w_merge_gate, b_merge_gate, w_out, ln1_g, ln1_b, ffn_w1, ffn_w3, ffn_w2, router_w, router_b, exp_w1, exp_w3, exp_w2, ln2_g, ln2_b):
    bsz, seq, _ = x.shape
    assert bsz == 1
    mod = _conditioning(c, w_cond, b_cond, w_mod, b_mod)
    xs = x.reshape(seq, D_MODEL)
    w_in_t = jnp.swapaxes(w_in, 1, 2)
    hm = _modulate(xs, mod[0, 1], mod[0, 0])
    s5_tables = jax.vmap(_s5_tables)(s5_lambda_re, s5_lambda_im, s5_log_dt, s5_b_re, s5_b_im, s5_c_re, s5_c_im)
    for l in range(DEPTH):
        shift_f, scale_f, gate_m, gate_f = mod[l, 3], mod[l, 4], mod[l, 2], mod[l, 5]
        proj = _matmul(hm, w_in_t, l, tm=1024, tn=512, n_out=D_IN_PAD, w_transposed=True,
                       name="in_proj")
        y_att = _moba(proj, rel_bias)
        y_s5 = _s5(proj, s5_tables, s5_d, s5_w_glu, s5_b_glu, l)
        y_gla = _gla(proj, gla_w_gate, gla_b_gate, gla_norm_g, l)
        merged = _merge(y_att, y_s5, y_gla, proj, w_branch, w_merge_gate, b_merge_gate, l)
        y = _matmul(merged, w_out, l, tm=1024, tn=512, name="out_proj")
        dense = l % 2 == 0
        router = None if dense else (router_w[l // 2], router_b[l // 2])
        outs = _deepnorm_ln(xs, y, gate_m, ln1_g[l], ln1_b[l], nxt=(scale_f, shift_f), router=router)
        xs, hf = outs[0], outs[1]
        if dense:
            hid = _swiglu_hidden(hf, ffn_w1, ffn_w3, l // 2)
            f = _matmul(hid, ffn_w2, l // 2, tm=256, tn=512, name="ffn_down")
            moe = None
        else:
            route = outs[2]
            pos1, pos2, row_token, tile_expert, n_active, n_tiles = _moe_plan(route)
            xsorted = _moe_dispatch(hf, row_token, n_active, n_tiles)
            ys = _moe_experts(xsorted, tile_expert, n_active, exp_w1, exp_w3, exp_w2, l // 2, n_tiles)
            f = None
            moe = (ys, pos1, pos2, route)
        nxt = (mod[l + 1, 1], mod[l + 1, 0]) if l + 1 < DEPTH else None
        outs = _deepnorm_ln(xs, f, gate_f, ln2_g[l], ln2_b[l], nxt=nxt, moe=moe)
        xs = outs[0]
        if nxt is not None:
            hm = outs[1]
    return xs.reshape(bsz, seq, D_MODEL)
```

```python
import functools
import math

import jax
import jax.numpy as jnp
from jax import lax
from jax.experimental import pallas as pl
from jax.experimental.pallas import tpu as pltpu

F32 = jnp.float32
BF16 = jnp.bfloat16
HIGHEST = lax.Precision.HIGHEST

D_MODEL = 4096
DEPTH = 4
BRANCH_WIDTH = 1024
N_BRANCH = 3
ATT_HEADS = 8
ATT_HEAD_DIM = 128
MOBA_BLOCK = 256
MOBA_TOPK = 3
MOBA_HEAD_GROUP = 8
MOBA_ONES_ROWS = 16
REL_BUCKETS = 32
REL_MAX_DIST = 128
S5_GROUP = 16
S5_GROUPS = 64
S5_STATE = 64
S5_CHUNK = 16
S5_TILE_GROUPS = 8
S5_TILE_STATE = S5_TILE_GROUPS * S5_STATE
GLA_HEADS = 4
GLA_KEY = 512
GLA_VAL = 1024
GLA_GATE_RANK = 16
GLA_GATE_TAU = 16.0
GLA_CHUNK = 64
GLA_SUB = 16
MERGE_RANK = 256
COND_RANK = 512
D_FF = 8192
N_EXPERTS = 8
TOP_K = 2
ROUTE_ID = 8
ROUTE_W = 10
MOE_TILE = 512
DMA_ISSUE_UNROLL = 8
D_FF_EXPERT = 1792
DN_ALPHA = (2 * DEPTH) ** 0.25
LN_EPS = 1e-5
NORM_EPS = 1e-6

D_IN = 7440
D_IN_PAD = 7680
TAIL_COL = 7168
TAIL_W = 512
LANES = 128
VMEM_LIMIT = 56 * 1024 * 1024
NEG_INF = float("-inf")


def _cparams(sem):
    return pltpu.CompilerParams(dimension_semantics=sem, vmem_limit_bytes=VMEM_LIMIT)


def _cond_kernel(c_ref, w_ref, b_ref, o_ref):
    z = jnp.dot(c_ref[...], w_ref[...], preferred_element_type=F32, precision=HIGHEST) + b_ref[...]
    o_ref[...] = z * jax.nn.sigmoid(z)


def _mod_kernel(cond_ref, w_ref, b_ref, o_ref):
    o_ref[0] = jnp.dot(cond_ref[...], w_ref[0], preferred_element_type=F32,
                       precision=HIGHEST) + b_ref[0]


def _conditioning(c, w_cond, b_cond, w_mod, b_mod):
    c8 = jnp.broadcast_to(c, (8, D_MODEL))
    cond = pl.pallas_call(
        _cond_kernel,
        out_shape=jax.ShapeDtypeStruct((8, COND_RANK), F32),
        compiler_params=_cparams(None),
        name="cond",
    )(c8, w_cond, b_cond.reshape(1, COND_RANK))
    n_mod = 6 * D_MODEL
    tn = 3072
    mod = pl.pallas_call(
        _mod_kernel,
        grid=(DEPTH, n_mod // tn),
        in_specs=[pl.BlockSpec((8, COND_RANK), lambda l, n: (0, 0)),
                  pl.BlockSpec((1, COND_RANK, tn), lambda l, n: (l, 0, n)),
                  pl.BlockSpec((1, 1, tn), lambda l, n: (l, 0, n))],
        out_specs=pl.BlockSpec((1, 8, tn), lambda l, n: (l, 0, n)),
        out_shape=jax.ShapeDtypeStruct((DEPTH, 8, n_mod), F32),
        compiler_params=_cparams(("arbitrary", "arbitrary")),
        name="mod",
    )(cond, w_mod, b_mod.reshape(DEPTH, 1, n_mod))
    return mod[:, 0, :].reshape(DEPTH, 6, 1, D_MODEL)


def _modulate_kernel(x_ref, scale_ref, shift_ref, o_ref):
    o_ref[...] = (x_ref[...] * (1.0 + scale_ref[...]) + shift_ref[...]).astype(o_ref.dtype)


def _modulate(x, scale, shift, tm=512):
    s = x.shape[0]
    vec = pl.BlockSpec((1, D_MODEL), lambda m: (0, 0))
    return pl.pallas_call(
        _modulate_kernel,
        grid=(s // tm,),
        in_specs=[pl.BlockSpec((tm, D_MODEL), lambda m: (m, 0)), vec, vec],
        out_specs=pl.BlockSpec((tm, D_MODEL), lambda m: (m, 0)),
        out_shape=jax.ShapeDtypeStruct((s, D_MODEL), BF16),
        compiler_params=_cparams(("arbitrary",)),
        name="modulate",
    )(x, scale, shift)


def _route_top2(logits):
    lane = lax.broadcasted_iota(jnp.int32, logits.shape, 1)
    m1 = jnp.max(logits, axis=-1, keepdims=True)
    i1 = jnp.min(jnp.where(logits == m1, lane, LANES), axis=-1, keepdims=True)
    rest = jnp.where(lane == i1, NEG_INF, logits)
    m2 = jnp.max(rest, axis=-1, keepdims=True)
    i2 = jnp.min(jnp.where(rest == m2, lane, LANES), axis=-1, keepdims=True)
    e2 = jnp.exp(m2 - m1)
    denom = 1.0 + e2
    rec = jnp.where(lane == ROUTE_ID, i1.astype(F32), 0.0)
    rec = jnp.where(lane == ROUTE_ID + 1, i2.astype(F32), rec)
    rec = jnp.where(lane == ROUTE_W, 1.0 / denom, rec)
    return jnp.where(lane == ROUTE_W + 1, e2 / denom, rec)


def _row_copy(src_ref, dst_ref, sem, src_row, dst_row):
    return pltpu.make_async_copy(src_ref.at[pl.ds(src_row, 1)], dst_ref.at[pl.ds(dst_row, 1)], sem)


def _smem_at(ref, i):
    return ref[i // LANES, i % LANES]


def _start_rows(copies_of, n_rows):
    def start(r, carry):
        for cp in copies_of(r):
            cp.start()
        return carry

    lax.fori_loop(0, n_rows, start, 0, unroll=DMA_ISSUE_UNROLL)


def _wait_rows(src_ref, dst_ref, sem):
    pltpu.make_async_copy(src_ref.at[pl.ds(0, dst_ref.shape[0])], dst_ref, sem).wait()


def _ln_kernel(*refs, has_next, has_router, moe_combine, tm):
    pos = 0
    if moe_combine:
        pos1_ref, pos2_ref, x_ref, route_ref, ys_ref = refs[:5]
        pos = 5
    else:
        x_ref, y_ref = refs[:2]
        pos = 2
    gate_ref, g_ref, b_ref = refs[pos:pos + 3]
    pos += 3
    if has_next:
        scale_ref, shift_ref = refs[pos:pos + 2]
        pos += 2
    if has_router:
        rw_ref, rb_ref = refs[pos:pos + 2]
        pos += 2
    xo_ref = refs[pos]
    pos += 1
    if moe_combine:
        buf1_ref, buf2_ref, sem = refs[-3:]
        base = pl.program_id(0) * tm

        _start_rows(lambda r: (_row_copy(ys_ref, buf1_ref, sem, _smem_at(pos1_ref, base + r), r),
                               _row_copy(ys_ref, buf2_ref, sem, _smem_at(pos2_ref, base + r), r)), tm)
        _wait_rows(ys_ref, buf1_ref, sem)
        _wait_rows(ys_ref, buf2_ref, sem)
        route = route_ref[...]
        lane = lax.broadcasted_iota(jnp.int32, route.shape, 1)
        w1 = jnp.sum(jnp.where(lane == ROUTE_W, route, 0.0), axis=-1, keepdims=True)
        w2 = jnp.sum(jnp.where(lane == ROUTE_W + 1, route, 0.0), axis=-1, keepdims=True)
        y = w1 * buf1_ref[...] + w2 * buf2_ref[...]
    else:
        y = y_ref[...].astype(F32)
    z = DN_ALPHA * x_ref[...] + (1.0 + gate_ref[...]) * y
    mu = jnp.mean(z, axis=-1, keepdims=True)
    zc = z - mu
    var = jnp.mean(zc * zc, axis=-1, keepdims=True)
    xn = zc * lax.rsqrt(var + LN_EPS) * g_ref[...] + b_ref[...]
    xo_ref[...] = xn
    if has_next:
        ho_ref = refs[pos]
        pos += 1
        h = xn * (1.0 + scale_ref[...]) + shift_ref[...]
        ho_ref[...] = h.astype(ho_ref.dtype)
        if has_router:
            co_ref = refs[pos]
            logits = jnp.dot(h, rw_ref[...], preferred_element_type=F32, precision=HIGHEST)
            lane = lax.broadcasted_iota(jnp.int32, logits.shape, 1)
            logits = jnp.where(lane < N_EXPERTS, logits + rb_ref[...], NEG_INF)
            co_ref[...] = _route_top2(logits)


def _deepnorm_ln(x, y, gate, g, b, nxt=None, router=None, moe=None, tm=256):
    s = x.shape[0]
    n_pre = 0 if moe is None else 2
    imap = (lambda m: (m, 0)) if moe is None else (lambda m, p1, p2: (m, 0))
    vmap = (lambda m: (0, 0)) if moe is None else (lambda m, p1, p2: (0, 0))
    row = pl.BlockSpec((tm, D_MODEL), imap)
    vec = pl.BlockSpec((1, D_MODEL), vmap)
    scratch = []
    if moe is None:
        args = [x, y]
        in_specs = [row, row]
    else:
        ys, pos1, pos2, route = moe
        args = [pos1, pos2, x, route, ys]
        in_specs = [row, pl.BlockSpec((tm, LANES), imap), pl.BlockSpec(memory_space=pl.ANY)]
        scratch = [pltpu.VMEM((tm, D_MODEL), F32), pltpu.VMEM((tm, D_MODEL), F32),
                   pltpu.SemaphoreType.DMA(())]
    args += [gate, g.reshape(1, D_MODEL), b.reshape(1, D_MODEL)]
    in_specs += [vec, vec, vec]
    out_shape = [jax.ShapeDtypeStruct((s, D_MODEL), F32)]
    out_specs = [row]
    if nxt is not None:
        args += [nxt[0], nxt[1]]
        in_specs += [vec, vec]
        out_shape.append(jax.ShapeDtypeStruct((s, D_MODEL), BF16 if router is None else F32))
        out_specs.append(row)
    if router is not None:
        rw, rb = router
        rw_pad = jnp.pad(rw, ((0, 0), (0, LANES - N_EXPERTS)))
        rb_pad = jnp.pad(rb, (0, LANES - N_EXPERTS)).reshape(1, LANES)
        args += [rw_pad, rb_pad]
        in_specs += [pl.BlockSpec((D_MODEL, LANES), vmap), pl.BlockSpec((1, LANES), vmap)]
        out_shape.append(jax.ShapeDtypeStruct((s, LANES), F32))
        out_specs.append(pl.BlockSpec((tm, LANES), imap))
    return pl.pallas_call(
        functools.partial(_ln_kernel, has_next=nxt is not None, has_router=router is not None,
                          moe_combine=moe is not None, tm=tm),
        grid_spec=pltpu.PrefetchScalarGridSpec(
            num_scalar_prefetch=n_pre, grid=(s // tm,), in_specs=in_specs, out_specs=out_specs,
            scratch_shapes=scratch),
        out_shape=out_shape,
        compiler_params=_cparams(("arbitrary",)),
        name="deepnorm_ln",
    )(*args)


def _matmul_kernel(a_ref, w_ref, o_ref, wb_ref, *, n_valid, tn, w_transposed):
    @pl.when(pl.program_id(1) == 0)
    def _():
        w = w_ref[0]
        wb_ref[...] = (w.T if w_transposed else w).astype(BF16)

    acc = jnp.dot(a_ref[...], wb_ref[...], preferred_element_type=F32)
    if n_valid is not None:
        col = pl.program_id(0) * tn + lax.broadcasted_iota(jnp.int32, acc.shape, 1)
        acc = jnp.where(col < n_valid, acc, 0.0)
    o_ref[...] = acc.astype(o_ref.dtype)


def _matmul(a, w_stack, layer, tm, tn, out_dtype=BF16, n_out=None, w_transposed=False, name="matmul"):
    m, k = a.shape
    n = w_stack.shape[1 if w_transposed else 2]
    n_out = n if n_out is None else n_out
    if w_transposed:
        w_spec = pl.BlockSpec((1, tn, k), lambda j, i: (layer, j, 0))
    else:
        w_spec = pl.BlockSpec((1, k, tn), lambda j, i: (layer, 0, j))
    return pl.pallas_call(
        functools.partial(_matmul_kernel, n_valid=None if n_out == n else n, tn=tn,
                          w_transposed=w_transposed),
        grid=(n_out // tn, m // tm),
        in_specs=[pl.BlockSpec((tm, k), lambda j, i: (i, 0)), w_spec],
        out_specs=pl.BlockSpec((tm, tn), lambda j, i: (i, j)),
        out_shape=jax.ShapeDtypeStruct((m, n_out), out_dtype),
        scratch_shapes=[pltpu.VMEM((k, tn), BF16)],
        compiler_params=_cparams(("arbitrary", "arbitrary")),
        name=name,
    )(a, w_stack)


def _rel_bucket(dist):
    n = jnp.maximum(dist, 0)
    max_exact = REL_BUCKETS // 2
    nf = jnp.maximum(n, 1).astype(F32)
    large = max_exact + (jnp.log(nf / max_exact) / math.log(REL_MAX_DIST / max_exact)
                         * (REL_BUCKETS - max_exact)).astype(jnp.int32)
    large = jnp.minimum(large, REL_BUCKETS - 1)
    return jnp.where(n < max_exact, n, large)


def _moba_kernel(relb_ref, q_ref, k_ref, vt_ref, o_ref,
                 kmean_ref, bown_ref, bprev_ref, sel_ref, m_ref, acc_ref, *, nb):
    g = pl.program_id(0)
    j = pl.program_id(1)
    blk = MOBA_BLOCK
    dh = ATT_HEAD_DIM
    scale = dh ** -0.5
    ln2 = math.log(2.0)
    nt = (((1,), (1,)), ((), ()))
    heads = range(MOBA_HEAD_GROUP)
    key_i = lax.broadcasted_iota(jnp.int32, (blk, blk), 0)
    qry_i = lax.broadcasted_iota(jnp.int32, (blk, blk), 1)

    @pl.when(j == 0)
    def _():
        for hh in heads:
            head = g * MOBA_HEAD_GROUP + hh
            kf = k_ref[:, hh * dh:(hh + 1) * dh].astype(F32).reshape(nb, blk, dh)
            kmean_ref[hh] = jnp.mean(kf, axis=1)
            for ref, off in ((bown_ref, 0), (bprev_ref, blk)):
                bucket = _rel_bucket(qry_i - key_i + off)
                bias = jnp.zeros((blk, blk), F32)
                for b in range(REL_BUCKETS):
                    bias = jnp.where(bucket == b, relb_ref[head, b], bias)
                ref[hh] = bias

    row0 = pl.multiple_of(j * blk, blk)
    blk_i = lax.broadcasted_iota(jnp.int32, (nb, blk), 0)
    q2 = []
    for hh in heads:
        q = q_ref[:, hh * dh:(hh + 1) * dh]
        q2.append((q.astype(F32) * (scale / ln2)).astype(BF16))
        score = lax.dot_general(kmean_ref[hh], q.astype(F32), nt,
                                preferred_element_type=F32, precision=HIGHEST)
        sc = jnp.where(blk_i < j, score, NEG_INF)
        seladd = jnp.full((nb, blk), NEG_INF, F32)
        for _ in range(MOBA_TOPK):
            mx = jnp.max(sc, axis=0, keepdims=True)
            cand = jnp.where(sc == mx, blk_i, nb)
            cand = jnp.where(mx > NEG_INF, cand, nb)
            idx = jnp.min(cand, axis=0, keepdims=True)
            pick = blk_i == idx
            seladd = jnp.where(pick, 0.0, seladd)
            sc = jnp.where(pick, NEG_INF, sc)
        sel_ref[hh] = seladd

    def scores(kb_row0, hh):
        return lax.dot_general(k_ref[pl.ds(kb_row0, blk), hh * dh:(hh + 1) * dh], q2[hh], nt,
                               preferred_element_type=F32)

    def weighted_values(kb, hh, p):
        return jnp.dot(vt_ref[kb, hh], p.astype(BF16), preferred_element_type=F32)

    kb_prev = jnp.maximum(j - 1, 0)
    r_prev = pl.multiple_of(kb_prev * blk, blk)
    s_own = [jnp.where(key_i <= qry_i, scores(row0, hh) * ln2 + bown_ref[hh], NEG_INF) for hh in heads]
    s_prev = [scores(r_prev, hh) * ln2 + bprev_ref[hh] for hh in heads]
    m_own = [jnp.max(s, axis=0, keepdims=True) for s in s_own]
    m_prev = [jnp.max(s, axis=0, keepdims=True) for s in s_prev]
    acc_own = [weighted_values(j, hh, jnp.exp(s_own[hh] - m_own[hh])) for hh in heads]
    acc_prev = [weighted_values(kb_prev, hh, jnp.exp(s_prev[hh] - m_prev[hh])) for hh in heads]
    for hh in heads:
        m_blk = m_prev[hh] + sel_ref[hh, pl.ds(kb_prev, 1), :]
        m_new = jnp.maximum(m_own[hh], m_blk)
        m_ref[hh] = m_new
        acc_ref[hh] = jnp.exp(m_own[hh] - m_new) * acc_own[hh] + jnp.exp(m_blk - m_new) * acc_prev[hh]

    def merge(hh, kb, m_blk, acc_blk):
        m_blk = m_blk + sel_ref[hh, pl.ds(kb, 1), :]
        m_old = m_ref[hh]
        m_new = jnp.maximum(m_old, m_blk)
        m_ref[hh] = m_new
        acc_ref[hh] = jnp.exp(m_old - m_new) * acc_ref[hh] + jnp.exp(m_blk - m_new) * acc_blk

    def far_blocks(pair, carry):
        chains = [(2 * pair + second, hh, second) for second in (0, 1) for hh in heads]
        s2 = [scores(pl.multiple_of(kb * blk, blk), hh) for kb, hh, _ in chains]
        m2 = [jnp.max(s, axis=0, keepdims=True) for s in s2]
        acc_blk = [weighted_values(kb, hh, jnp.exp2(s - m)) for (kb, hh, _), s, m in zip(chains, s2, m2)]
        second_ok = jnp.where(2 * pair + 1 < j - 1, 0.0, NEG_INF)
        for (kb, hh, second), m, acc in zip(chains, m2, acc_blk):
            m_blk = m * ln2 + relb_ref[g * MOBA_HEAD_GROUP + hh, REL_BUCKETS - 1]
            merge(hh, kb, m_blk + second_ok if second else m_blk, acc)
        return carry

    lax.fori_loop(0, j // 2, far_blocks, 0)
    for hh in heads:
        acc = acc_ref[hh]
        o_ref[:, hh * dh:(hh + 1) * dh] = (acc[0:dh] / acc[dh:dh + 1]).T.astype(o_ref.dtype)


def _moba(proj, rel_bias):
    s = proj.shape[0]
    nb = s // MOBA_BLOCK
    hg = MOBA_HEAD_GROUP
    gw = hg * ATT_HEAD_DIM
    n_groups = ATT_HEADS // hg
    dhp = ATT_HEAD_DIM + MOBA_ONES_ROWS
    v_t = proj[:, 2 * BRANCH_WIDTH:3 * BRANCH_WIDTH].reshape(nb, MOBA_BLOCK, ATT_HEADS, ATT_HEAD_DIM)
    v_t = jnp.concatenate([v_t.transpose(0, 2, 3, 1),
                           jnp.ones((nb, ATT_HEADS, MOBA_ONES_ROWS, MOBA_BLOCK), BF16)], axis=2)
    return pl.pallas_call(
        functools.partial(_moba_kernel, nb=nb),
        grid=(n_groups, nb),
        in_specs=[pl.BlockSpec(memory_space=pltpu.SMEM),
                  pl.BlockSpec((MOBA_BLOCK, gw), lambda g, j: (j, g)),
                  pl.BlockSpec((s, gw), lambda g, j: (0, n_groups + g), pipeline_mode=pl.Buffered(1)),
                  pl.BlockSpec((nb, hg, dhp, MOBA_BLOCK), lambda g, j: (0, g, 0, 0),
                               pipeline_mode=pl.Buffered(1))],
        out_specs=pl.BlockSpec((MOBA_BLOCK, gw), lambda g, j: (j, g)),
        out_shape=jax.ShapeDtypeStruct((s, BRANCH_WIDTH), BF16),
        scratch_shapes=[pltpu.VMEM((hg, nb, ATT_HEAD_DIM), F32),
                        pltpu.VMEM((hg, MOBA_BLOCK, MOBA_BLOCK), F32),
                        pltpu.VMEM((hg, MOBA_BLOCK, MOBA_BLOCK), F32),
                        pltpu.VMEM((hg, nb, MOBA_BLOCK), F32),
                        pltpu.VMEM((hg, 1, MOBA_BLOCK), F32),
                        pltpu.VMEM((hg, dhp, MOBA_BLOCK), F32)],
        compiler_params=_cparams(("arbitrary", "arbitrary")),
        name="moba",
    )(rel_bias.T, proj, proj, v_t)


def _s5_tables(lam_re, lam_im, log_dt, b_re, b_im, c_re, c_im):
    t_len = S5_CHUNK
    g_cnt, p_cnt, h_cnt = S5_GROUPS, S5_STATE, S5_GROUP
    tg = S5_TILE_GROUPS
    nt = g_cnt // tg
    dt = jnp.exp(log_dt)[:, None]
    ar, ai = lam_re * dt, lam_im * dt

    def lam_pow(steps):
        st = steps.astype(F32)[:, None, None]
        mag = jnp.exp(st * ar)
        return mag * jnp.cos(st * ai), mag * jnp.sin(st * ai)

    pr, pi = lam_pow(jnp.arange(t_len + 1))
    qr, qi = lam_pow((t_len - 1) - jnp.arange(t_len))
    nr, ni = pr[1] - 1.0, pi[1]
    den = lam_re * lam_re + lam_im * lam_im
    rr, ri = (nr * lam_re + ni * lam_im) / den, (ni * lam_re - nr * lam_im) / den
    bbr = rr[..., None] * b_re - ri[..., None] * b_im
    bbi = rr[..., None] * b_im + ri[..., None] * b_re

    c_re_t, c_im_t = c_re.transpose(0, 2, 1), c_im.transpose(0, 2, 1)
    cq_re = c_re_t[None] * qr[..., None] - c_im_t[None] * qi[..., None]
    cq_im = c_re_t[None] * qi[..., None] + c_im_t[None] * qr[..., None]
    kc = jnp.sum(cq_re[:, :, :, None, :] * bbr[None, :, :, :, None]
                 - cq_im[:, :, :, None, :] * bbi[None, :, :, :, None], axis=2)
    kd = kc.reshape(t_len, nt, tg, h_cnt, h_cnt).transpose(1, 0, 3, 2, 4).reshape(nt, t_len, h_cnt, LANES)

    def state_minor(x):
        return x.reshape(-1, nt, 1, tg * p_cnt).transpose(1, 0, 2, 3)

    def b_state_minor(x):
        return x.reshape(nt, tg, p_cnt, h_cnt).transpose(0, 3, 1, 2).reshape(nt, 1, h_cnt, tg * p_cnt)

    q_re, q_im = state_minor(qr), state_minor(qi)
    bb_re, bb_im = b_state_minor(bbr), b_state_minor(bbi)
    bz = jnp.concatenate([q_re * bb_re - q_im * bb_im, q_re * bb_im + q_im * bb_re], axis=3)

    def c_state_minor(x):
        return x.reshape(nt, tg, h_cnt, p_cnt).transpose(0, 2, 1, 3).reshape(nt, 1, h_cnt, tg * p_cnt)

    p_re, p_im = state_minor(pr[1:]), state_minor(pi[1:])
    cc_re, cc_im = c_state_minor(c_re), c_state_minor(c_im)
    cm = jnp.concatenate([cc_re * p_re - cc_im * p_im, -(cc_re * p_im + cc_im * p_re)], axis=3)
    a_re = pr[t_len].reshape(nt, 1, tg * p_cnt)
    a_im = pi[t_len].reshape(nt, 1, tg * p_cnt)
    return kd.astype(BF16), bz.astype(BF16), cm.astype(BF16), a_re, a_im


def _s5_kernel(u_ref, kd_ref, bz_ref, cm_ref, are_ref, aim_ref, d_ref, y_ref,
               uf_ref, ucat_ref, z_ref, hc_ref, kdf_ref, bzf_ref, cmf_ref, *, nc):
    t_len = S5_CHUNK
    ns = S5_TILE_STATE
    tg, hs, ps = S5_TILE_GROUPS, S5_GROUP, S5_STATE
    chan_grp = lax.broadcasted_iota(jnp.int32, (LANES, LANES), 0) // hs
    lane_grp = lax.broadcasted_iota(jnp.int32, (LANES, LANES), 1) // hs
    state_grp = (lax.broadcasted_iota(jnp.int32, (LANES, 2 * ns), 1) % ns) // ps
    chan_grp_w = lax.broadcasted_iota(jnp.int32, (LANES, 2 * ns), 0) // hs
    zero = jnp.zeros((), BF16)
    for s in range(t_len):
        rows = slice(s * LANES, (s + 1) * LANES)
        kdf_ref[rows, :] = jnp.where(chan_grp == lane_grp, jnp.concatenate([kd_ref[0, 0, s]] * tg, axis=0), zero)
        bzf_ref[rows, :] = jnp.where(chan_grp_w == state_grp, jnp.concatenate([bz_ref[0, 0, s]] * tg, axis=0), zero)
        cmf_ref[s] = jnp.where(chan_grp_w == state_grp, jnp.concatenate([cm_ref[0, 0, s]] * tg, axis=0), zero)
    uf_ref[...] = u_ref[...].astype(F32)
    for s in range(t_len):
        ucat_ref[:, s * LANES:(s + 1) * LANES] = uf_ref[pl.ds(s, nc, stride=t_len), :].astype(BF16)
    z_ref[...] = jnp.dot(ucat_ref[...], bzf_ref[...], preferred_element_type=F32)
    a_re = are_ref[0, 0]
    a_im = aim_ref[0, 0]

    def step(c, carry):
        h_re, h_im = carry
        hc_ref[pl.ds(c, 1), 0:ns] = h_re
        hc_ref[pl.ds(c, 1), ns:2 * ns] = h_im
        z_re = z_ref[pl.ds(c, 1), 0:ns]
        z_im = z_ref[pl.ds(c, 1), ns:2 * ns]
        return (a_re * h_re - a_im * h_im + z_re, a_re * h_im + a_im * h_re + z_im)

    zero_row = jnp.zeros((1, ns), F32)
    lax.fori_loop(0, nc, step, (zero_row, zero_row), unroll=8)
    hc = hc_ref[...].astype(BF16)
    d_skip = d_ref[0]
    for t in range(t_len):
        acc = lax.dot_general(hc, cmf_ref[t], (((1,), (1,)), ((), ())), preferred_element_type=F32)
        acc += jnp.dot(ucat_ref[:, 0:(t + 1) * LANES],
                       kdf_ref[(t_len - 1 - t) * LANES:t_len * LANES, :],
                       preferred_element_type=F32)
        acc += d_skip * uf_ref[pl.ds(t, nc, stride=t_len), :]
        y_ref[pl.ds(t, nc, stride=t_len), :] = jax.nn.gelu(acc)


def _s5_glu_kernel(y_ref, yn_ref, w_ref, b_ref, o_ref, wb_ref):
    @pl.when(pl.program_id(1) == 0)
    def _():
        wb_ref[...] = w_ref[0].astype(BF16)

    z = jnp.dot(y_ref[...].astype(BF16), wb_ref[...], preferred_element_type=F32) + b_ref[0]
    o_ref[...] = (yn_ref[...] * jax.nn.sigmoid(z)).astype(o_ref.dtype)


def _s5(proj, tables, d_skip, w_glu, b_glu, layer):
    s = proj.shape[0]
    nc = s // S5_CHUNK
    nt = S5_GROUPS // S5_TILE_GROUPS
    kd, bz, cm, a_re, a_im = tables
    u_col = 3 * BRANCH_WIDTH // LANES
    ns2 = 2 * S5_TILE_STATE
    y = pl.pallas_call(
        functools.partial(_s5_kernel, nc=nc),
        grid=(nt,),
        in_specs=[pl.BlockSpec((s, LANES), lambda c: (0, u_col + c)),
                  pl.BlockSpec((1, 1, S5_CHUNK, S5_GROUP, LANES), lambda c: (layer, c, 0, 0, 0)),
                  pl.BlockSpec((1, 1, S5_CHUNK, S5_GROUP, ns2), lambda c: (layer, c, 0, 0, 0)),
                  pl.BlockSpec((1, 1, S5_CHUNK, S5_GROUP, ns2), lambda c: (layer, c, 0, 0, 0)),
                  pl.BlockSpec((1, 1, 1, S5_TILE_STATE), lambda c: (layer, c, 0, 0)),
                  pl.BlockSpec((1, 1, 1, S5_TILE_STATE), lambda c: (layer, c, 0, 0)),
                  pl.BlockSpec((1, 1, LANES), lambda c: (layer, 0, c))],
        out_specs=pl.BlockSpec((s, LANES), lambda c: (0, c)),
        out_shape=jax.ShapeDtypeStruct((s, BRANCH_WIDTH), F32),
        scratch_shapes=[pltpu.VMEM((s, LANES), F32),
                        pltpu.VMEM((nc, S5_CHUNK * LANES), BF16),
                        pltpu.VMEM((nc, ns2), F32),
                        pltpu.VMEM((nc, ns2), F32),
                        pltpu.VMEM((S5_CHUNK * LANES, LANES), BF16),
                        pltpu.VMEM((S5_CHUNK * LANES, ns2), BF16),
                        pltpu.VMEM((S5_CHUNK, LANES, ns2), BF16)],
        compiler_params=_cparams(("arbitrary",)),
        name="s5_scan",
    )(proj, kd, bz, cm, a_re, a_im, d_skip.reshape(DEPTH, 1, BRANCH_WIDTH))
    tm, tn = 512, 512
    return pl.pallas_call(
        _s5_glu_kernel,
        grid=(BRANCH_WIDTH // tn, s // tm),
        in_specs=[pl.BlockSpec((tm, BRANCH_WIDTH), lambda j, i: (i, 0)),
                  pl.BlockSpec((tm, tn), lambda j, i: (i, j)),
                  pl.BlockSpec((1, BRANCH_WIDTH, tn), lambda j, i: (layer, 0, j)),
                  pl.BlockSpec((1, 1, tn), lambda j, i: (layer, 0, j))],
        out_specs=pl.BlockSpec((tm, tn), lambda j, i: (i, j)),
        out_shape=jax.ShapeDtypeStruct((s, BRANCH_WIDTH), BF16),
        scratch_shapes=[pltpu.VMEM((BRANCH_WIDTH, tn), BF16)],
        compiler_params=_cparams(("arbitrary", "arbitrary")),
        name="s5_glu",
    )(y, y, w_glu, b_glu.reshape(DEPTH, 1, BRANCH_WIDTH))


def _gla_kernel(q_ref, k_ref, v_ref, r_ref, tail_ref, wg_ref, bg_ref, ng_ref, o_ref, st_ref):
    dk = GLA_KEY // GLA_HEADS
    dv = GLA_VAL // GLA_HEADS
    cs = GLA_CHUNK
    sub = GLA_SUB
    nt = (((1,), (1,)), ((), ()))
    tn = (((0,), (0,)), ((), ()))

    @pl.when(pl.program_id(0) == 0)
    def _():
        st_ref[...] = jnp.zeros_like(st_ref)

    gate_in = jnp.dot(tail_ref[:, 0:GLA_GATE_RANK], wg_ref[0].astype(BF16),
                      preferred_element_type=F32) + bg_ref[0]
    log_a = jax.nn.log_sigmoid(gate_in) / GLA_GATE_TAU
    ri = lax.broadcasted_iota(jnp.int32, (cs, cs), 0)
    ci = lax.broadcasted_iota(jnp.int32, (cs, cs), 1)
    tril = (ri >= ci).astype(F32)
    bcum_all = jnp.dot(tril, log_a, preferred_element_type=F32, precision=HIGHEST)
    sub_row = lax.broadcasted_iota(jnp.int32, (sub, sub), 0)
    sub_col = lax.broadcasted_iota(jnp.int32, (sub, sub), 1)

    for h in range(GLA_HEADS):
        q = q_ref[:, h * dk:(h + 1) * dk].astype(F32) * dk ** -0.5
        k = k_ref[:, h * dk:(h + 1) * dk].astype(F32)
        v_bf = v_ref[:, h * dv:(h + 1) * dv]
        bc = bcum_all[:, h * dk:(h + 1) * dk]
        state = st_ref[h]
        o_inter = lax.dot_general((q * jnp.exp(bc)).astype(BF16), state.astype(BF16), nt,
                                  preferred_element_type=F32)
        parts = []
        for i in range(cs // sub):
            lo = i * sub
            b_i, q_i, k_i = bc[lo:lo + sub], q[lo:lo + sub], k[lo:lo + sub]
            o_i = o_inter[lo:lo + sub]
            if i > 0:
                ref = bc[lo - 1:lo]
                q_h = (q_i * jnp.exp(b_i - ref)).astype(BF16)
                k_h = (k[:lo] * jnp.exp(ref - bc[:lo])).astype(BF16)
                attn = lax.dot_general(q_h, k_h, nt, preferred_element_type=F32)
                o_i = o_i + jnp.dot(attn.astype(BF16), v_bf[:lo], preferred_element_type=F32)
            diag = jnp.where(sub_col == sub_row, jnp.sum(q_i * k_i, axis=-1, keepdims=True), 0.0)
            for dlt in range(1, sub):
                b_s = pltpu.roll(b_i, dlt, 0)
                k_s = pltpu.roll(k_i, dlt, 0)
                e = jnp.exp(jnp.minimum(b_i - b_s, 0.0))
                a = jnp.sum(q_i * k_s * e, axis=-1, keepdims=True)
                diag = jnp.where(sub_col == sub_row - dlt, a, diag)
            o_i = o_i + jnp.dot(diag.astype(BF16), v_bf[lo:lo + sub], preferred_element_type=F32)
            parts.append(o_i)
        o = jnp.concatenate(parts, axis=0)
        b_last = bc[cs - 1:cs]
        k_dec = (k * jnp.exp(b_last - bc)).astype(BF16)
        st_ref[h] = state * jnp.exp(b_last) + lax.dot_general(v_bf, k_dec, tn,
                                                             preferred_element_type=F32)
        o = o * lax.rsqrt(jnp.mean(o * o, axis=-1, keepdims=True) + NORM_EPS)
        o = o * ng_ref[0, :, h * dv:(h + 1) * dv]
        r = r_ref[:, h * dv:(h + 1) * dv].astype(F32)
        o_ref[:, h * dv:(h + 1) * dv] = (o * (r * jax.nn.sigmoid(r))).astype(o_ref.dtype)


def _gla(proj, w_gate, b_gate, norm_g, layer):
    s = proj.shape[0]
    cs = GLA_CHUNK
    return pl.pallas_call(
        _gla_kernel,
        grid=(s // cs,),
        in_specs=[pl.BlockSpec((cs, GLA_KEY), lambda c: (c, 4 * BRANCH_WIDTH // GLA_KEY)),
                  pl.BlockSpec((cs, GLA_KEY), lambda c: (c, 4 * BRANCH_WIDTH // GLA_KEY + 1)),
                  pl.BlockSpec((cs, GLA_VAL), lambda c: (c, 5)),
                  pl.BlockSpec((cs, GLA_VAL), lambda c: (c, 6)),
                  pl.BlockSpec((cs, TAIL_W), lambda c: (c, TAIL_COL // TAIL_W)),
                  pl.BlockSpec((1, GLA_GATE_RANK, GLA_KEY), lambda c: (layer, 0, 0)),
                  pl.BlockSpec((1, 1, GLA_KEY), lambda c: (layer, 0, 0)),
                  pl.BlockSpec((1, 1, GLA_VAL), lambda c: (layer, 0, 0))],
        out_specs=pl.BlockSpec((cs, GLA_VAL), lambda c: (c, 0)),
        out_shape=jax.ShapeDtypeStruct((s, GLA_VAL), BF16),
        scratch_shapes=[pltpu.VMEM((GLA_HEADS, GLA_VAL // GLA_HEADS, GLA_KEY // GLA_HEADS), F32)],
        compiler_params=_cparams(("arbitrary",)),
        name="gla",
    )(proj, proj, proj, proj, proj, w_gate, b_gate.reshape(DEPTH, 1, GLA_KEY),
      norm_g.reshape(DEPTH, 1, GLA_VAL))


def _merge_kernel(ya_ref, ys_ref, yg_ref, tail_ref, wb_ref, wg0_ref, wg1_ref, wg2_ref,
                  bg0_ref, bg1_ref, bg2_ref, o_ref, wbb_ref, wgb_ref):
    wg_refs = (wg0_ref, wg1_ref, wg2_ref)
    bg_refs = (bg0_ref, bg1_ref, bg2_ref)

    @pl.when(pl.program_id(1) == 0)
    def _():
        wbb_ref[...] = wb_ref[0].astype(BF16)
        for n in range(N_BRANCH):
            wgb_ref[n] = wg_refs[n][0].astype(BF16)

    mz = tail_ref[:, GLA_GATE_RANK:GLA_GATE_RANK + MERGE_RANK]
    acc = None
    for n, y_ref in enumerate((ya_ref, ys_ref, yg_ref)):
        up = jnp.dot(y_ref[...], wbb_ref[n], preferred_element_type=F32)
        gate = jax.nn.sigmoid(jnp.dot(mz, wgb_ref[n], preferred_element_type=F32) + bg_refs[n][0])
        acc = gate * up if acc is None else acc + gate * up
    o_ref[...] = acc.astype(o_ref.dtype)


def _merge(y_att, y_s5, y_gla, proj, w_branch, w_merge_gate, b_merge_gate, layer, tm=512, tn=512):
    s = proj.shape[0]
    n_col = D_MODEL // tn
    bg = b_merge_gate.reshape(DEPTH, 1, N_BRANCH * D_MODEL)
    ybs = pl.BlockSpec((tm, BRANCH_WIDTH), lambda j, i: (i, 0))

    def gate_col(n):
        return lambda j, i: (layer, 0, n * n_col + j)

    return pl.pallas_call(
        _merge_kernel,
        grid=(n_col, s // tm),
        in_specs=[ybs, ybs, ybs,
                  pl.BlockSpec((tm, TAIL_W), lambda j, i: (i, TAIL_COL // TAIL_W)),
                  pl.BlockSpec((1, N_BRANCH, BRANCH_WIDTH, tn), lambda j, i: (layer, 0, 0, j))]
                 + [pl.BlockSpec((1, MERGE_RANK, tn), gate_col(n)) for n in range(N_BRANCH)]
                 + [pl.BlockSpec((1, 1, tn), gate_col(n)) for n in range(N_BRANCH)],
        out_specs=pl.BlockSpec((tm, tn), lambda j, i: (i, j)),
        out_shape=jax.ShapeDtypeStruct((s, D_MODEL), BF16),
        scratch_shapes=[pltpu.VMEM((N_BRANCH, BRANCH_WIDTH, tn), BF16),
                        pltpu.VMEM((N_BRANCH, MERGE_RANK, tn), BF16)],
        compiler_params=_cparams(("arbitrary", "arbitrary")),
        name="merge",
    )(y_att, y_s5, y_gla, proj, w_branch, w_merge_gate, w_merge_gate, w_merge_gate, bg, bg, bg)


def _swiglu_kernel(a_ref, w1_ref, w3_ref, o_ref, w1b_ref, w3b_ref):
    @pl.when(pl.program_id(1) == 0)
    def _():
        w1b_ref[...] = w1_ref[0].astype(BF16)
        w3b_ref[...] = w3_ref[0].astype(BF16)

    a = a_ref[...]
    g = jnp.dot(a, w1b_ref[...], preferred_element_type=F32)
    u = jnp.dot(a, w3b_ref[...], preferred_element_type=F32)
    o_ref[...] = (g * jax.nn.sigmoid(g) * u).astype(o_ref.dtype)


def _swiglu_hidden(a, w1_stack, w3_stack, layer, tm=512, tn=512):
    m, k = a.shape
    n = w1_stack.shape[2]
    wspec = pl.BlockSpec((1, k, tn), lambda j, i: (layer, 0, j))
    return pl.pallas_call(
        _swiglu_kernel,
        grid=(n // tn, m // tm),
        in_specs=[pl.BlockSpec((tm, k), lambda j, i: (i, 0)), wspec, wspec],
        out_specs=pl.BlockSpec((tm, tn), lambda j, i: (i, j)),
        out_shape=jax.ShapeDtypeStruct((m, n), BF16),
        scratch_shapes=[pltpu.VMEM((k, tn), BF16), pltpu.VMEM((k, tn), BF16)],
        compiler_params=_cparams(("arbitrary", "arbitrary")),
        name="swiglu_hidden",
    )(a, w1_stack, w3_stack)


def _moe_plan(route):
    s = route.shape[0]
    tile = MOE_TILE
    n_tiles = (TOP_K * s) // tile + N_EXPERTS
    ids = route[:, ROUTE_ID:ROUTE_ID + TOP_K].astype(jnp.int32)
    onehot = jnp.sum(jax.nn.one_hot(ids, N_EXPERTS, dtype=jnp.int32), axis=1)
    before = jnp.cumsum(onehot, axis=0) - onehot
    counts = jnp.sum(onehot, axis=0)
    padded = (counts + tile - 1) // tile * tile
    ends = jnp.cumsum(padded)
    offsets = ends - padded
    pos = offsets[ids] + jnp.take_along_axis(before, ids, axis=1)
    tile_start = jnp.arange(n_tiles, dtype=jnp.int32) * tile
    tile_expert = jnp.minimum(jnp.sum(tile_start[:, None] >= ends[None, :], axis=1), N_EXPERTS - 1)
    n_active = (ends[-1] // tile).reshape(1)
    tokens = jnp.broadcast_to(jnp.arange(s, dtype=jnp.int32)[:, None], pos.shape)
    row_token = jnp.zeros((n_tiles * tile,), jnp.int32).at[pos.reshape(-1)].set(tokens.reshape(-1))
    pos1 = pos[:, 0].reshape(s // LANES, LANES)
    pos2 = pos[:, 1].reshape(s // LANES, LANES)
    return (pos1, pos2, row_token.reshape(-1, LANES), tile_expert.astype(jnp.int32),
            n_active.astype(jnp.int32), n_tiles)


def _moe_dispatch_kernel(rt_ref, na_ref, h_ref, o_ref, buf_ref, sem):
    i = pl.program_id(0)
    n_active = na_ref[0]

    def start_tile(tile):
        slot = tile % 2
        _start_rows(lambda r: (_row_copy(h_ref, buf_ref.at[slot], sem.at[slot],
                                         _smem_at(rt_ref, tile * MOE_TILE + r), r),), MOE_TILE)

    @pl.when(jnp.logical_and(i == 0, n_active > 0))
    def _():
        start_tile(i)

    @pl.when(i + 1 < n_active)
    def _():
        start_tile(i + 1)

    @pl.when(i < n_active)
    def _():
        slot = i % 2
        _wait_rows(h_ref, buf_ref.at[slot], sem.at[slot])
        o_ref[...] = buf_ref[slot].astype(o_ref.dtype)

    @pl.when(i >= n_active)
    def _():
        o_ref[...] = jnp.zeros_like(o_ref)


def _moe_dispatch(h, row_token, n_active, n_tiles):
    tile = MOE_TILE
    return pl.pallas_call(
        _moe_dispatch_kernel,
        grid_spec=pltpu.PrefetchScalarGridSpec(
            num_scalar_prefetch=2, grid=(n_tiles,),
            in_specs=[pl.BlockSpec(memory_space=pl.ANY)],
            out_specs=pl.BlockSpec((tile, D_MODEL), lambda i, rt, na: (i, 0)),
            scratch_shapes=[pltpu.VMEM((2, tile, D_MODEL), F32), pltpu.SemaphoreType.DMA((2,))]),
        out_shape=jax.ShapeDtypeStruct((n_tiles * tile, D_MODEL), BF16),
        compiler_params=_cparams(("arbitrary",)),
        name="moe_dispatch",
    )(row_token, n_active, h)


def _new_expert_panel(te_ref):
    i = pl.program_id(1)
    return jnp.logical_or(i == 0, te_ref[i] != te_ref[jnp.maximum(i - 1, 0)])


def _moe_hidden_kernel(te_ref, na_ref, a_ref, w1_ref, w3_ref, o_ref, w1b_ref, w3b_ref):
    @pl.when(_new_expert_panel(te_ref))
    def _():
        w1b_ref[...] = w1_ref[0, 0].astype(BF16)
        w3b_ref[...] = w3_ref[0, 0].astype(BF16)

    @pl.when(pl.program_id(1) < na_ref[0])
    def _():
        a = a_ref[...]
        g = jnp.dot(a, w1b_ref[...], preferred_element_type=F32)
        u = jnp.dot(a, w3b_ref[...], preferred_element_type=F32)
        o_ref[...] = (g * jax.nn.sigmoid(g) * u).astype(o_ref.dtype)

    @pl.when(pl.program_id(1) >= na_ref[0])
    def _():
        o_ref[...] = jnp.zeros_like(o_ref)


def _moe_down_kernel(te_ref, na_ref, a_ref, w_ref, o_ref, wb_ref):
    @pl.when(_new_expert_panel(te_ref))
    def _():
        wb_ref[...] = w_ref[0, 0].astype(BF16)

    @pl.when(pl.program_id(1) < na_ref[0])
    def _():
        o_ref[...] = jnp.dot(a_ref[...], wb_ref[...], preferred_element_type=F32).astype(o_ref.dtype)

    @pl.when(pl.program_id(1) >= na_ref[0])
    def _():
        o_ref[...] = jnp.zeros_like(o_ref)


def _moe_experts(xs, tile_expert, n_active, w1_stack, w3_stack, w2_stack, layer, n_tiles,
                 tn_hidden=256, tn_down=1024):
    tile = MOE_TILE
    rows = n_tiles * tile
    w_in_spec = pl.BlockSpec((1, 1, D_MODEL, tn_hidden), lambda n, i, te, na: (layer, te[i], 0, n))
    hid = pl.pallas_call(
        _moe_hidden_kernel,
        grid_spec=pltpu.PrefetchScalarGridSpec(
            num_scalar_prefetch=2, grid=(D_FF_EXPERT // tn_hidden, n_tiles),
            in_specs=[pl.BlockSpec((tile, D_MODEL), lambda n, i, te, na: (i, 0)), w_in_spec, w_in_spec],
            out_specs=pl.BlockSpec((tile, tn_hidden), lambda n, i, te, na: (i, n)),
            scratch_shapes=[pltpu.VMEM((D_MODEL, tn_hidden), BF16), pltpu.VMEM((D_MODEL, tn_hidden), BF16)]),
        out_shape=jax.ShapeDtypeStruct((rows, D_FF_EXPERT), BF16),
        compiler_params=_cparams(("arbitrary", "arbitrary")),
        name="moe_hidden",
    )(tile_expert, n_active, xs, w1_stack, w3_stack)
    return pl.pallas_call(
        _moe_down_kernel,
        grid_spec=pltpu.PrefetchScalarGridSpec(
            num_scalar_prefetch=2, grid=(D_MODEL // tn_down, n_tiles),
            in_specs=[pl.BlockSpec((tile, D_FF_EXPERT), lambda n, i, te, na: (i, 0)),
                      pl.BlockSpec((1, 1, D_FF_EXPERT, tn_down), lambda n, i, te, na: (layer, te[i], 0, n))],
            out_specs=pl.BlockSpec((tile, tn_down), lambda n, i, te, na: (i, n)),
            scratch_shapes=[pltpu.VMEM((D_FF_EXPERT, tn_down), BF16)]),
        out_shape=jax.ShapeDtypeStruct((rows, D_MODEL), F32),
        compiler_params=_cparams(("arbitrary", "arbitrary")),
        name="moe_down",
    )(tile_expert, n_active, hid, w2_stack)


def kernel(x, c, w_cond, b_cond, rel_bias, w_mod, b_mod, w_in, s5_lambda_re, s5_lambda_im, s5_log_dt, s5_b_re, s5_b_im, s5_c_re, s5_c_im, s5_d, s5_w_glu, s5_b_glu, gla_w_gate, gla_b_gate, gla_norm_g, w_branch, w_merge_gate, b_merge_gate, w_out, ln1_g, ln1_b, ffn_w1, ffn_w3, ffn_w2, router_w, router_b, exp_w1, exp_w3, exp_w2, ln2_g, ln2_b):
    bsz, seq, _ = x.shape
    assert bsz == 1
    mod = _conditioning(c, w_cond, b_cond, w_mod, b_mod)
    xs = x.reshape(seq, D_MODEL)
    w_in_t = jnp.swapaxes(w_in, 1, 2)
    hm = _modulate(xs, mod[0, 1], mod[0, 0])
    s5_tables = jax.vmap(_s5_tables)(s5_lambda_re, s5_lambda_im, s5_log_dt, s5_b_re, s5_b_im, s5_c_re, s5_c_im)
    for l in range(DEPTH):
        shift_f, scale_f, gate_m, gate_f = mod[l, 3], mod[l, 4], mod[l, 2], mod[l, 5]
        proj = _matmul(hm, w_in_t, l, tm=1024, tn=512, n_out=D_IN_PAD, w_transposed=True,
                       name="in_proj")
        y_att = _moba(proj, rel_bias)
        y_s5 = _s5(proj, s5_tables, s5_d, s5_w_glu, s5_b_glu, l)
        y_gla = _gla(proj, gla_w_gate, gla_b_gate, gla_norm_g, l)
        merged = _merge(y_att, y_s5, y_gla, proj, w_branch, w_merge_gate, b_merge_gate, l)
        y = _matmul(merged, w_out, l, tm=1024, tn=512, name="out_proj")
        dense = l % 2 == 0
        router = None if dense else (router_w[l // 2], router_b[l // 2])
        outs = _deepnorm_ln(xs, y, gate_m, ln1_g[l], ln1_b[l], nxt=(scale_f, shift_f), router=router)
        xs, hf = outs[0], outs[1]
        if dense:
            hid = _swiglu_hidden(hf, ffn_w1, ffn_w3, l // 2)
            f = _matmul(hid, ffn_w2, l // 2, tm=256, tn=512, name="ffn_down")
            moe = None
        else:
            route = outs[2]
            pos1, pos2, row_token, tile_expert, n_active, n_tiles = _moe_plan(route)
            xsorted = _moe_dispatch(hf, row_token, n_active, n_tiles)
            ys = _moe_experts(xsorted, tile_expert, n_active, exp_w1, exp_w3, exp_w2, l // 2, n_tiles)
            f = None
            moe = (ys, pos1, pos2, route)
        nxt = (mod[l + 1, 1], mod[l + 1, 0]) if l + 1 < DEPTH else None
        outs = _deepnorm_ln(xs, f, gate_f, ln2_g[l], ln2_b[l], nxt=nxt, moe=moe)
        xs = outs[0]
        if nxt is not None:
            hm = outs[1]
    return xs.reshape(bsz, seq, D_MODEL)
```
